```python
import jax, jax.numpy as jnp
from jax import lax
import numpy as np

D_MODEL = 1024
BATCH = 8
SEQ = 8192
DEPTH = 2
DEC_BATCH = 32
DEC_SEQ = 32
PAST_LEN = 4096

CHUNK = 64
N_META = 16
N_HEADS = 8
D_NOPE = 64
D_ROPE = 32
D_QK = D_NOPE + D_ROPE
D_V = 64
Q_RANK = 384
KV_RANK = 256
ATTN_WIDTH = N_HEADS * D_V
POOL_WIDTH = D_MODEL - ATTN_WIDTH
POOL_WINDOWS = (2, 4, 8, 16)
N_POOL_GROUPS = len(POOL_WINDOWS)
POOL_GROUP = POOL_WIDTH // N_POOL_GROUPS
POOL_STATE = max(POOL_WINDOWS) - 1
D_IN = POOL_WIDTH + Q_RANK + KV_RANK + D_ROPE
D_FF = ((8 * D_MODEL + 2) // 3 + 255) // 256 * 256
ROPE_THETA = 10000.0
EPS = 1e-6
Q_BLOCK = 128

kernel_name = 'hybrid_pool_mla_stream_step'


def rms_norm(x, g):
    xf = x.astype(jnp.float32)
    y = xf * lax.rsqrt(jnp.mean(xf * xf, axis=-1, keepdims=True) + EPS)
    return (y * g.astype(jnp.float32)).astype(x.dtype)


def rope(x, pos):
    half = D_ROPE // 2
    inv = ROPE_THETA ** (-jnp.arange(half, dtype=jnp.float32) / half)
    ang = pos.astype(jnp.float32)[:, None] * inv[None, :]
    shape = (ang.shape[0],) + (1,) * (x.ndim - 3) + (half,)
    cos = jnp.cos(ang).reshape(shape)
    sin = jnp.sin(ang).reshape(shape)
    xf = x.astype(jnp.float32)
    x1, x2 = xf[..., :half], xf[..., half:]
    return jnp.concatenate([x1 * cos - x2 * sin, x1 * sin + x2 * cos], axis=-1).astype(x.dtype)


def pool_mix(u_ext, n_prev, w_pool, pool_scale):
    B, L, _ = u_ext.shape
    T = L - n_prev
    uf = u_ext.astype(jnp.float32)
    csum = jnp.concatenate([jnp.zeros((B, 1, POOL_WIDTH), jnp.float32), jnp.cumsum(uf, axis=1)], axis=1)
    t = jnp.arange(n_prev, L)
    means = []
    for g, w in enumerate(POOL_WINDOWS):
        sl = slice(g * POOL_GROUP, (g + 1) * POOL_GROUP)
        lo = jnp.maximum(t + 1 - w, 0)
        cnt = jnp.minimum(t + 1, w).astype(jnp.float32)
        means.append((csum[:, t + 1, sl] - csum[:, lo, sl]) / cnt[None, :, None])
    d = (jnp.concatenate(means, axis=-1) - uf[:, n_prev:]).reshape(B, T, N_POOL_GROUPS, POOL_GROUP)
    y = jnp.einsum('btgc,gcd->btgd', d, w_pool.astype(jnp.float32)).reshape(B, T, POOL_WIDTH)
    return (y * pool_scale.astype(jnp.float32)).astype(u_ext.dtype)


def block_causal_attention(q, k, v, q_chunk, k_chunk):
    scale = D_QK ** -0.5

    def attend(qb, qc):
        s = jnp.einsum('bqhd,bkhd->bhqk', qb, k, preferred_element_type=jnp.float32) * scale
        mask = qc[:, None] >= k_chunk[None, :]
        s = jnp.where(mask[None, None], s, -jnp.inf)
        p = jax.nn.softmax(s, axis=-1)
        return jnp.einsum('bhqk,bkhd->bqhd', p.astype(v.dtype), v)

    B, Lq, H, D = q.shape
    if Lq <= Q_BLOCK:
        return attend(q, q_chunk)
    nb = -(-Lq // Q_BLOCK)
    pad = nb * Q_BLOCK - Lq
    qp = jnp.pad(q, ((0, 0), (0, pad), (0, 0), (0, 0)))
    cp = jnp.pad(q_chunk, (0, pad), mode='edge')
    qb = qp.reshape(B, nb, Q_BLOCK, H, D).transpose(1, 0, 2, 3, 4)
    cb = cp.reshape(nb, Q_BLOCK)
    out = lax.map(lambda a: attend(a[0], a[1]), (qb, cb))
    out = out.transpose(1, 0, 2, 3, 4).reshape(B, nb * Q_BLOCK, H, D_V)
    return out[:, :Lq]


def mixer(h, pos, q_chunk, k_chunk, past_lat, past_kpe, past_pool,
          w_in, q_a_norm, w_uq, kv_a_norm, w_uk, w_uv, q_norm, k_norm, w_pool, pool_scale, w_o):
    B, T, _ = h.shape
    z = h @ w_in
    u, q_lat, kv_lat, k_pe = jnp.split(z, [POOL_WIDTH, POOL_WIDTH + Q_RANK, POOL_WIDTH + Q_RANK + KV_RANK], axis=-1)
    u_ext = jnp.concatenate([past_pool.astype(u.dtype), u], axis=1)
    pool_out = pool_mix(u_ext, u_ext.shape[1] - T, w_pool, pool_scale)
    new_pool = u_ext[:, -POOL_STATE:]
    q = (rms_norm(q_lat, q_a_norm) @ w_uq).reshape(B, T, N_HEADS, D_QK)
    q = rms_norm(jnp.concatenate([q[..., :D_NOPE], rope(q[..., D_NOPE:], pos)], axis=-1), q_norm)
    c_kv = rms_norm(kv_lat, kv_a_norm)
    k_pe = rope(k_pe, pos)
    lat_all = jnp.concatenate([past_lat.astype(c_kv.dtype), c_kv], axis=1)
    kpe_all = jnp.concatenate([past_kpe.astype(k_pe.dtype), k_pe], axis=1)
    Lk = lat_all.shape[1]
    k_nope = (lat_all @ w_uk).reshape(B, Lk, N_HEADS, D_NOPE)
    v = (lat_all @ w_uv).reshape(B, Lk, N_HEADS, D_V)
    k = jnp.concatenate([k_nope, jnp.broadcast_to(kpe_all[:, :, None, :], (B, Lk, N_HEADS, D_ROPE))], axis=-1)
    k = rms_norm(k, k_norm)
    attn = block_causal_attention(q, k, v, q_chunk, k_chunk).reshape(B, T, ATTN_WIDTH)
    y = jnp.concatenate([pool_out, attn], axis=-1) @ w_o
    return y, c_kv, k_pe, new_pool


def swiglu(h, w_gate, w_up, w_down):
    return (jax.nn.silu(h @ w_gate) * (h @ w_up)) @ w_down


def trunk(x, pos, q_chunk, k_chunk, past_lat, past_kpe, past_pool, weights):
    (norm_mix, w_in, q_a_norm, w_uq, kv_a_norm, w_uk, w_uv, q_norm, k_norm,
     w_pool, pool_scale, w_o, norm_ffn, w_gate, w_up, w_down) = weights
    lat_rows, kpe_rows, pool_rows = [], [], []
    for l in range(DEPTH):
        y, c_kv, k_pe, new_pool = mixer(
            rms_norm(x, norm_mix[l]), pos, q_chunk, k_chunk, past_lat[l], past_kpe[l], past_pool[l],
            w_in[l], q_a_norm[l], w_uq[l], kv_a_norm[l], w_uk[l], w_uv[l], q_norm[l], k_norm[l],
            w_pool[l], pool_scale[l], w_o[l])
        x = x + y
        x = x + swiglu(rms_norm(x, norm_ffn[l]), w_gate[l], w_up[l], w_down[l])
        lat_rows.append(c_kv)
        kpe_rows.append(k_pe)
        pool_rows.append(new_pool)
    return x, jnp.stack(lat_rows), jnp.stack(kpe_rows), jnp.stack(pool_rows)


def setup_inputs(seed: int = 0) -> dict:
    key = jax.random.key(seed)
    ks = jax.random.split(key, 22)
    f32 = jnp.float32

    def normal(k, shape):
        return jax.random.normal(k, shape, f32)

    def dense(k, shape, fan_in):
        return jax.random.normal(k, shape, f32) * fan_in ** -0.5

    def gain(k, shape):
        return 1.0 + 0.05 * jax.random.normal(k, shape, f32)

    return {
        'x_prompt': normal(ks[0], (BATCH, SEQ, D_MODEL)),
        'x_sample': normal(ks[1], (DEC_BATCH, DEC_SEQ, D_MODEL)),
        'cache_latent': normal(ks[2], (DEPTH, DEC_BATCH, N_META + PAST_LEN, KV_RANK)),
        'cache_krope': normal(ks[3], (DEPTH, DEC_BATCH, N_META + PAST_LEN, D_ROPE)),
        'state_pool': normal(ks[4], (DEPTH, DEC_BATCH, POOL_STATE, POOL_WIDTH)),
        'meta_tokens': normal(ks[5], (N_META, D_MODEL)),
        'norm_mix': gain(ks[6], (DEPTH, D_MODEL)),
        'w_in': dense(ks[7], (DEPTH, D_MODEL, D_IN), D_MODEL),
        'q_a_norm': gain(ks[8], (DEPTH, Q_RANK)),
        'w_uq': dense(ks[9], (DEPTH, Q_RANK, N_HEADS * D_QK), Q_RANK),
        'kv_a_norm': gain(ks[10], (DEPTH, KV_RANK)),
        'w_uk': dense(ks[11], (DEPTH, KV_RANK, N_HEADS * D_NOPE), KV_RANK),
        'w_uv': dense(ks[12], (DEPTH, KV_RANK, N_HEADS * D_V), KV_RANK),
        'q_norm': gain(ks[13], (DEPTH, D_QK)),
        'k_norm': gain(ks[14], (DEPTH, D_QK)),
        'w_pool': dense(ks[15], (DEPTH, N_POOL_GROUPS, POOL_GROUP, POOL_GROUP), POOL_GROUP),
        'pool_scale': gain(ks[16], (DEPTH, POOL_WIDTH)),
        'w_o': dense(ks[17], (DEPTH, D_MODEL, D_MODEL), D_MODEL),
        'norm_ffn': gain(ks[18], (DEPTH, D_MODEL)),
        'w_gate': dense(ks[19], (DEPTH, D_MODEL, D_FF), D_MODEL),
        'w_up': dense(ks[20], (DEPTH, D_MODEL, D_FF), D_MODEL),
        'w_down': dense(ks[21], (DEPTH, D_FF, D_MODEL), D_FF),
    }


def reference(x_prompt, x_sample, cache_latent, cache_krope, state_pool, meta_tokens,
              norm_mix, w_in, q_a_norm, w_uq, kv_a_norm, w_uk, w_uv, q_norm, k_norm,
              w_pool, pool_scale, w_o, norm_ffn, w_gate, w_up, w_down):
    weights = (norm_mix, w_in, q_a_norm, w_uq, kv_a_norm, w_uk, w_uv, q_norm, k_norm,
               w_pool, pool_scale, w_o, norm_ffn, w_gate, w_up, w_down)
    dt = x_prompt.dtype
    Bp, S, _ = x_prompt.shape
    meta = jnp.broadcast_to(meta_tokens.astype(dt)[None], (Bp, N_META, D_MODEL))
    xp = jnp.concatenate([meta, x_prompt], axis=1)
    pos_p = jnp.arange(N_META + S, dtype=jnp.int32)
    chunk_p = jnp.where(pos_p < N_META, -1, (pos_p - N_META) // CHUNK)
    empty_lat = jnp.zeros((DEPTH, Bp, 0, KV_RANK), dt)
    empty_kpe = jnp.zeros((DEPTH, Bp, 0, D_ROPE), dt)
    empty_pool = jnp.zeros((DEPTH, Bp, 0, POOL_WIDTH), dt)
    yp, lat_p, kpe_p, pool_p = trunk(xp, pos_p, chunk_p, chunk_p, empty_lat, empty_kpe, empty_pool, weights)
    y_prompt = yp[:, N_META:]
    T = x_sample.shape[1]
    past = cache_latent.shape[2] - N_META
    pos_s = N_META + past + jnp.arange(T, dtype=jnp.int32)
    k_idx = jnp.arange(N_META + past + T, dtype=jnp.int32)
    chunk_k = jnp.where(k_idx < N_META, -1, (k_idx - N_META) // CHUNK)
    chunk_q = (pos_s - N_META) // CHUNK
    y_sample, lat_s, kpe_s, pool_s = trunk(x_sample, pos_s, chunk_q, chunk_k,
                                           cache_latent, cache_krope, state_pool, weights)
    return (y_prompt, y_sample, lat_p, kpe_p, pool_p, lat_s, kpe_s, pool_s)
```

```python
import functools
import math

import jax
import jax.numpy as jnp
from jax import lax
from jax.experimental import pallas as pl
from jax.experimental.pallas import tpu as pltpu

CHUNK = 64
N_META = 16
N_HEADS = 8
D_NOPE = 64
D_ROPE = 32
D_QK = D_NOPE + D_ROPE
D_V = 64
Q_RANK = 384
KV_RANK = 256
POOL_WINDOWS = (2, 4, 8, 16)
POOL_GROUP = 128
POOL_WIDTH = POOL_GROUP * len(POOL_WINDOWS)
POOL_STATE = max(POOL_WINDOWS) - 1
ROPE_THETA = 10000.0
EPS = 1e-6

LANES = 128
HIST_ROWS = 16
VMEM_LIMIT_BYTES = 56 * 1024 * 1024
ROW_TILE = 512
ATTN_TILE = 512
MASKED = -1e30

_F32 = jnp.float32
_BF16 = jnp.bfloat16
_NT = (((1,), (1,)), ((), ()))


def _dot(a, b):
    return jnp.dot(a, b, preferred_element_type=_F32)


def _dot_nt(a, b):
    return lax.dot_general(a, b, _NT, preferred_element_type=_F32)


def _rms(x, g):
    return x * lax.rsqrt(jnp.mean(x * x, axis=-1, keepdims=True) + EPS) * g


def _perm(x, p):
    hi = x.astype(_BF16)
    lo = (x - hi.astype(_F32)).astype(_BF16)
    return _dot(hi, p) + _dot(lo, p)


def _head_norm(x, gain):
    ss = jnp.sum(x * x, axis=-1, keepdims=True)
    return x * lax.rsqrt(ss / D_QK + EPS) * gain


def _pre_kernel(x_ref, hist_ref, cosp_ref, sinp_ref, cos32_ref, sin32_ref,
                gmix_ref, win_ref, gq_ref, wuq_ref, gkv_ref, wuk_ref, wuv_ref, qgain_ref, kgain_ref,
                wpool_ref, pscale_ref, place_ref, swap_ref,
                q_ref, k_ref, v_ref, lat_ref, krope_ref, pool_ref, ulast_ref,
                ext_ref, *, nb, tm, hist_shared, truncated):
    t = pl.program_id(1)
    d_model = x_ref.shape[-1]
    rows = nb * tm
    x = x_ref[...].reshape(rows, d_model)
    h = _rms(x, gmix_ref[...]).astype(_BF16)
    z = _dot(h, win_ref[...])
    u = z[:, :POOL_WIDTH]
    q_lat = z[:, POOL_WIDTH:POOL_WIDTH + Q_RANK]
    kv_lat = z[:, POOL_WIDTH + Q_RANK:POOL_WIDTH + Q_RANK + KV_RANK]
    kpe = z[:, POOL_WIDTH + Q_RANK + KV_RANK:POOL_WIDTH + Q_RANK + KV_RANK + D_ROPE]

    for b in range(nb):
        @pl.when(t == 0)
        def _():
            ext_ref[b, 0:HIST_ROWS, :] = hist_ref[0 if hist_shared else b]

        @pl.when(t > 0)
        def _():
            ext_ref[b, 0:HIST_ROWS, :] = ext_ref[b, tm:tm + HIST_ROWS, :]

        ext_ref[b, HIST_ROWS:HIST_ROWS + tm, :] = u[b * tm:(b + 1) * tm]
    for b in range(nb):
        cols = []
        for g, w in enumerate(POOL_WINDOWS):
            sl = slice(g * POOL_GROUP, (g + 1) * POOL_GROUP)
            cur = ext_ref[b, HIST_ROWS:HIST_ROWS + tm, sl]
            acc = cur
            for j in range(1, w):
                acc = acc + ext_ref[b, HIST_ROWS - j:HIST_ROWS - j + tm, sl]
            if truncated:
                row = t * tm + lax.broadcasted_iota(jnp.int32, (tm, 1), 0)
                mean = acc / jnp.minimum(row + 1, w).astype(_F32)
            else:
                mean = acc / float(w)
            cols.append(_dot((mean - cur).astype(_BF16), wpool_ref[g]))
        pool = jnp.concatenate(cols, axis=1) * pscale_ref[...]
        pool_ref[b] = pool.astype(pool_ref.dtype)
        ulast_ref[b] = ext_ref[b, tm:tm + HIST_ROWS, :]

    qn = _rms(q_lat, gq_ref[...]).astype(_BF16)
    qraw = _dot(qn, wuq_ref[...])
    c_kv = _rms(kv_lat, gkv_ref[...])
    lat_ref[...] = c_kv.reshape(nb, tm, KV_RANK)
    ckv_b = c_kv.astype(_BF16)
    knope = _dot(ckv_b, wuk_ref[...])
    vfull = _dot(ckv_b, wuv_ref[...])

    def per_row(ref):
        tab = ref[...]
        return tab if nb == 1 else jnp.concatenate([tab] * nb, axis=0)

    cosp, sinp = per_row(cosp_ref), per_row(sinp_ref)
    krope = kpe * per_row(cos32_ref) + _perm(kpe, swap_ref[...]) * per_row(sin32_ref)
    krope_ref[...] = krope.reshape(nb, tm, D_ROPE)
    kpe_placed = _perm(krope, place_ref[...])

    lane = lax.broadcasted_iota(jnp.int32, (1, LANES), 1)
    first_half = lane < D_NOPE + D_ROPE // 2
    for hd in range(N_HEADS):
        sl = slice(hd * LANES, (hd + 1) * LANES)
        qh = qraw[:, sl]
        partner = jnp.where(first_half, pltpu.roll(qh, LANES - D_ROPE // 2, 1), pltpu.roll(qh, D_ROPE // 2, 1))
        qh = _head_norm(qh * cosp + partner * sinp, qgain_ref[...])
        q_ref[:, hd] = qh.astype(q_ref.dtype).reshape(nb, tm, LANES)
        kh = _head_norm(knope[:, sl] + kpe_placed, kgain_ref[...])
        k_ref[:, hd] = kh.astype(k_ref.dtype).reshape(nb, tm, LANES)
        v_ref[:, hd] = vfull[:, sl].astype(v_ref.dtype).reshape(nb, tm, LANES)


def _const_spec(a):
    nd = a.ndim
    return pl.BlockSpec(a.shape, lambda *_: (0,) * nd, pipeline_mode=pl.Buffered(1))


def _pre_call(x, hist, tabs, lw, consts, *, nb, hist_shared, truncated):
    bsz, t_len, d_model = x.shape
    tm = t_len if nb > 1 else min(ROW_TILE, t_len)
    assert bsz % nb == 0 and t_len % tm == 0 and tm % HIST_ROWS == 0
    grid = (bsz // nb, t_len // tm)
    weights = (lw["gmix"], lw["w_in"], lw["gq"], lw["w_uq"], lw["gkv"], lw["w_uk"], lw["w_uv"],
               lw["qgain"], lw["kgain"], lw["w_pool"], lw["pscale"], consts["place"], consts["swap"])
    hist_spec = (pl.BlockSpec((1, HIST_ROWS, POOL_WIDTH), lambda b, t: (0, 0, 0)) if hist_shared
                 else pl.BlockSpec((nb, HIST_ROWS, POOL_WIDTH), lambda b, t: (b, 0, 0)))
    in_specs = [pl.BlockSpec((nb, tm, d_model), lambda b, t: (b, t, 0)), hist_spec]
    in_specs += [pl.BlockSpec((tm, a.shape[1]), lambda b, t: (t, 0)) for a in tabs]
    in_specs += [_const_spec(w) for w in weights]
    head_spec = pl.BlockSpec((nb, N_HEADS, tm, LANES), lambda b, t: (b, 0, t, 0))
    head_shape = jax.ShapeDtypeStruct((bsz, N_HEADS, t_len, LANES), _BF16)

    def row_out(width, dtype):
        return (pl.BlockSpec((nb, tm, width), lambda b, t: (b, t, 0)),
                jax.ShapeDtypeStruct((bsz, t_len, width), dtype))

    lat_spec, lat_shape = row_out(KV_RANK, _F32)
    kr_spec, kr_shape = row_out(D_ROPE, _F32)
    pool_spec, pool_shape = row_out(POOL_WIDTH, _BF16)
    out_specs = [head_spec, head_spec, head_spec, lat_spec, kr_spec, pool_spec,
                 pl.BlockSpec((nb, HIST_ROWS, POOL_WIDTH), lambda b, t: (b, 0, 0))]
    out_shape = [head_shape, head_shape, head_shape, lat_shape, kr_shape, pool_shape,
                 jax.ShapeDtypeStruct((bsz, HIST_ROWS, POOL_WIDTH), _F32)]
    q, k, v, lat, krope, pool, ulast = pl.pallas_call(
        functools.partial(_pre_kernel, nb=nb, tm=tm, hist_shared=hist_shared, truncated=truncated),
        grid=grid, in_specs=in_specs, out_specs=out_specs, out_shape=out_shape,
        scratch_shapes=[pltpu.VMEM((nb, HIST_ROWS + tm, POOL_WIDTH), _F32)],
        compiler_params=pltpu.CompilerParams(dimension_semantics=("arbitrary", "arbitrary"),
                                             vmem_limit_bytes=VMEM_LIMIT_BYTES),
        name="pre_mixer",
    )(x, hist, *tabs, *weights)
    return dict(q=q, k=k, v=v, lat=lat, krope=krope, pool=pool, ulast=ulast)


def _post_kernel(x_ref, pool_ref, attn_ref, wop_ref, woa_ref, gffn_ref, wg_ref, wu_ref, wd_ref, o_ref):
    x1 = x_ref[...] + _dot(pool_ref[...], wop_ref[...]) + _dot(attn_ref[...], woa_ref[...])
    hn = _rms(x1, gffn_ref[...]).astype(_BF16)
    g = _dot(hn, wg_ref[...])
    up = _dot(hn, wu_ref[...])
    act = (g * jax.nn.sigmoid(g) * up).astype(_BF16)
    o_ref[...] = x1 + _dot(act, wd_ref[...])


def _post_call(x, pool, attn, lw):
    shape = x.shape
    d_model = shape[-1]
    x2 = x.reshape(-1, d_model)
    n = x2.shape[0]
    tm = min(ROW_TILE, n)
    assert n % tm == 0
    weights = (lw["w_o_pool"], lw["w_o_attn"], lw["gffn"], lw["w_gate"], lw["w_up"], lw["w_down"])

    def rows(width):
        return pl.BlockSpec((tm, width), lambda i: (i, 0))

    out = pl.pallas_call(
        _post_kernel, grid=(n // tm,),
        in_specs=[rows(d_model), rows(POOL_WIDTH), rows(N_HEADS * D_V)] + [_const_spec(w) for w in weights],
        out_specs=rows(d_model), out_shape=jax.ShapeDtypeStruct((n, d_model), _F32),
        compiler_params=pltpu.CompilerParams(dimension_semantics=("arbitrary",),
                                             vmem_limit_bytes=VMEM_LIMIT_BYTES),
        name="post_ffn",
    )(x2, pool.reshape(n, POOL_WIDTH), attn.reshape(n, N_HEADS * D_V), *weights)
    return out.reshape(shape)


def _attn_kernel(q_ref, k_ref, v_ref, km_ref, vm_ref, o_ref, m_ref, l_ref, acc_ref, *, tile):
    qi = pl.program_id(2)
    nrep = tile // LANES

    def step(hh, k_t, v_t, mask):
        s = _dot_nt(q_ref[0, hh], k_t)
        if mask is not None:
            s = jnp.where(mask, s, MASKED)
        m_prev = m_ref[hh]
        m_next = jnp.maximum(m_prev, jnp.max(s, axis=1, keepdims=True))
        alpha = jnp.exp2(m_prev - m_next)
        p = jnp.exp2(s - jnp.concatenate([m_next] * nrep, axis=1))
        l_ref[hh] = alpha * l_ref[hh] + jnp.sum(p, axis=1, keepdims=True)
        acc_ref[hh] = alpha * acc_ref[hh] + _dot(p.astype(_BF16), v_t)
        m_ref[hh] = m_next

    for hh in range(2):
        s = _dot_nt(q_ref[0, hh], km_ref[hh])
        m = jnp.max(s, axis=1, keepdims=True)
        p = jnp.exp2(s - m)
        m_ref[hh] = jnp.broadcast_to(m, (tile, LANES))
        l_ref[hh] = jnp.broadcast_to(jnp.sum(p, axis=1, keepdims=True), (tile, LANES))
        acc_ref[hh] = _dot(p.astype(_BF16), vm_ref[hh])

    def body(j, carry):
        off = pl.multiple_of(j * tile, tile)
        for hh in range(2):
            step(hh, k_ref[0, hh, pl.ds(off, tile), :], v_ref[0, hh, pl.ds(off, tile), :], None)
        return carry

    lax.fori_loop(0, qi, body, 0)

    off = pl.multiple_of(qi * tile, tile)
    row_chunk = lax.broadcasted_iota(jnp.int32, (tile, tile), 0) // CHUNK
    col_chunk = lax.broadcasted_iota(jnp.int32, (tile, tile), 1) // CHUNK
    mask = row_chunk >= col_chunk
    for hh in range(2):
        step(hh, k_ref[0, hh, pl.ds(off, tile), :], v_ref[0, hh, pl.ds(off, tile), :], mask)
    o_ref[0] = (acc_ref[0] / l_ref[0] + acc_ref[1] / l_ref[1]).astype(o_ref.dtype)


def _prompt_attn_call(q, k, v, km, vm):
    bsz, n_heads, s_len, _ = q.shape
    tile = min(ATTN_TILE, s_len)
    assert s_len % tile == 0 and tile % CHUNK == 0 and n_heads % 2 == 0
    grid = (bsz, n_heads // 2, s_len // tile)
    seq_spec = pl.BlockSpec((1, 2, s_len, LANES), lambda b, p, i: (b, p, 0, 0))
    meta_spec = pl.BlockSpec((2, N_META, LANES), lambda b, p, i: (p, 0, 0))
    return pl.pallas_call(
        functools.partial(_attn_kernel, tile=tile), grid=grid,
        in_specs=[pl.BlockSpec((1, 2, tile, LANES), lambda b, p, i: (b, p, i, 0)),
                  seq_spec, seq_spec, meta_spec, meta_spec],
        out_specs=pl.BlockSpec((1, tile, LANES), lambda b, p, i: (b, i, p)),
        out_shape=jax.ShapeDtypeStruct((bsz, s_len, n_heads * D_V), _BF16),
        scratch_shapes=[pltpu.VMEM((2, tile, LANES), _F32)] * 3,
        compiler_params=pltpu.CompilerParams(dimension_semantics=("arbitrary",) * 3,
                                             vmem_limit_bytes=VMEM_LIMIT_BYTES),
        name="prompt_attn",
    )(q, k, v, km, vm)


def _meta_attn_kernel(q_ref, k_ref, v_ref, o_ref):
    for pair in range(N_HEADS // 2):
        out = None
        for hd in (2 * pair, 2 * pair + 1):
            s = _dot_nt(q_ref[hd], k_ref[hd])
            p = jnp.exp2(s - jnp.max(s, axis=1, keepdims=True))
            o = _dot(p.astype(_BF16), v_ref[hd]) / jnp.sum(p, axis=1, keepdims=True)
            out = o if out is None else out + o
        o_ref[:, pair * LANES:(pair + 1) * LANES] = out.astype(o_ref.dtype)


def _meta_attn_call(q, k, v):
    n_rows = q.shape[1]
    return pl.pallas_call(
        _meta_attn_kernel, out_shape=jax.ShapeDtypeStruct((n_rows, N_HEADS * D_V), _BF16),
        name="meta_attn",
    )(q, k, v)


def _sample_attn_kernel(q_ref, kn_ref, vn_ref, lat_ref, kpe_ref, wuk_ref, wuv_ref, kgain_ref, place_ref,
                        o_ref, *, past):
    t_len = q_ref.shape[2]
    latb = lat_ref[0].astype(_BF16)
    kpe_placed = _perm(kpe_ref[0], place_ref[...])
    tq = lax.broadcasted_iota(jnp.int32, (t_len, t_len), 0)
    tk = lax.broadcasted_iota(jnp.int32, (t_len, t_len), 1)
    new_mask = (past + tq) // CHUNK >= (past + tk) // CHUNK
    for pair in range(N_HEADS // 2):
        out = None
        for hd in (2 * pair, 2 * pair + 1):
            sl = slice(hd * LANES, (hd + 1) * LANES)
            kc = _head_norm(_dot(latb, wuk_ref[:, sl]) + kpe_placed, kgain_ref[...]).astype(_BF16)
            vc = _dot(latb, wuv_ref[:, sl]).astype(_BF16)
            q = q_ref[0, hd]
            s_meta = _dot_nt(q, kc[:N_META])
            s_past = _dot_nt(q, kc[N_META:])
            s_new = jnp.where(new_mask, _dot_nt(q, kn_ref[0, hd]), MASKED)
            m = jnp.maximum(jnp.maximum(jnp.max(s_meta, axis=1, keepdims=True),
                                        jnp.max(s_past, axis=1, keepdims=True)),
                            jnp.max(s_new, axis=1, keepdims=True))
            p_meta, p_past, p_new = jnp.exp2(s_meta - m), jnp.exp2(s_past - m), jnp.exp2(s_new - m)
            den = (jnp.sum(p_meta, axis=1, keepdims=True) + jnp.sum(p_past, axis=1, keepdims=True)
                   + jnp.sum(p_new, axis=1, keepdims=True))
            num = (_dot(p_meta.astype(_BF16), vc[:N_META]) + _dot(p_past.astype(_BF16), vc[N_META:])
                   + _dot(p_new.astype(_BF16), vn_ref[0, hd]))
            o = num / den
            out = o if out is None else out + o
        o_ref[0, :, pair * LANES:(pair + 1) * LANES] = out.astype(o_ref.dtype)


def _sample_attn_call(q, k_new, v_new, cache_lat, cache_kpe, lw, consts):
    bsz, n_heads, t_len, _ = q.shape
    l_cache = cache_lat.shape[1]
    past = l_cache - N_META
    head_spec = pl.BlockSpec((1, n_heads, t_len, LANES), lambda b: (b, 0, 0, 0))
    weights = (lw["w_uk"], lw["w_uv"], lw["kgain"], consts["place"])
    return pl.pallas_call(
        functools.partial(_sample_attn_kernel, past=past), grid=(bsz,),
        in_specs=[head_spec, head_spec, head_spec,
                  pl.BlockSpec((1, l_cache, KV_RANK), lambda b: (b, 0, 0)),
                  pl.BlockSpec((1, l_cache, D_ROPE), lambda b: (b, 0, 0))] + [_const_spec(w) for w in weights],
        out_specs=pl.BlockSpec((1, t_len, n_heads * D_V), lambda b: (b, 0, 0)),
        out_shape=jax.ShapeDtypeStruct((bsz, t_len, n_heads * D_V), _BF16),
        compiler_params=pltpu.CompilerParams(dimension_semantics=("arbitrary",),
                                             vmem_limit_bytes=VMEM_LIMIT_BYTES),
        name="sample_attn",
    )(q, k_new, v_new, cache_lat, cache_kpe, *weights)


def _pack_heads(w, width, lane_of_head):
    kdim = w.shape[0]
    out = jnp.zeros((kdim, N_HEADS, LANES), w.dtype)
    w3 = w.reshape(kdim, N_HEADS, width)
    for hd in range(N_HEADS):
        out = out.at[:, hd, lane_of_head(hd):lane_of_head(hd) + width].set(w3[:, hd])
    return out.reshape(kdim, N_HEADS * LANES)


def _layer_weights(l, norm_mix, w_in, q_a_norm, w_uq, kv_a_norm, w_uk, w_uv, q_norm, k_norm,
                   w_pool, pool_scale, w_o, norm_ffn, w_gate, w_up, w_down):
    d_in = w_in.shape[2]
    d_in_pad = -(-d_in // (2 * LANES)) * (2 * LANES)
    pad96 = lambda g: jnp.pad(g.astype(_F32), (0, LANES - D_QK))[None, :]
    q_scale = D_QK ** -0.5 * math.log2(math.e)
    return dict(
        gmix=norm_mix[l][None, :].astype(_F32),
        w_in=jnp.pad(w_in[l], ((0, 0), (0, d_in_pad - d_in))).astype(_BF16),
        gq=q_a_norm[l][None, :].astype(_F32),
        w_uq=_pack_heads(w_uq[l], D_QK, lambda hd: 0).astype(_BF16),
        gkv=kv_a_norm[l][None, :].astype(_F32),
        w_uk=_pack_heads(w_uk[l], D_NOPE, lambda hd: 0).astype(_BF16),
        w_uv=_pack_heads(w_uv[l], D_V, lambda hd: (hd % 2) * D_V).astype(_BF16),
        qgain=pad96(q_norm[l]) * q_scale,
        kgain=pad96(k_norm[l]),
        w_pool=w_pool[l].astype(_BF16),
        pscale=pool_scale[l][None, :].astype(_F32),
        w_o_pool=w_o[l][:POOL_WIDTH].astype(_BF16),
        w_o_attn=w_o[l][POOL_WIDTH:].astype(_BF16),
        gffn=norm_ffn[l][None, :].astype(_F32),
        w_gate=w_gate[l].astype(_BF16),
        w_up=w_up[l].astype(_BF16),
        w_down=w_down[l].astype(_BF16),
    )


def _rope_tables(pos0, t_len):
    half = D_ROPE // 2
    inv = ROPE_THETA ** (-jnp.arange(half, dtype=_F32) / half)
    ang = (pos0 + jnp.arange(t_len, dtype=jnp.int32)).astype(_F32)[:, None] * inv[None, :]
    cos, sin = jnp.cos(ang), jnp.sin(ang)
    cos32 = jnp.concatenate([cos, cos], axis=1)
    sin32 = jnp.concatenate([-sin, sin], axis=1)
    ones = jnp.ones((t_len, D_NOPE), _F32)
    zeros = jnp.zeros((t_len, D_NOPE), _F32)
    tail1 = jnp.ones((t_len, LANES - D_QK), _F32)
    tail0 = jnp.zeros((t_len, LANES - D_QK), _F32)
    cosp = jnp.concatenate([ones, cos32, tail1], axis=1)
    sinp = jnp.concatenate([zeros, sin32, tail0], axis=1)
    return (cosp, sinp, cos32, sin32)


def _perm_consts():
    half = D_ROPE // 2
    idx = jnp.arange(D_ROPE)
    place = jnp.zeros((D_ROPE, LANES), _F32).at[idx, D_NOPE + idx].set(1.0)
    swap = jnp.zeros((D_ROPE, D_ROPE), _F32).at[idx, (idx + half) % D_ROPE].set(1.0)
    return dict(place=place.astype(_BF16), swap=swap.astype(_BF16))


def kernel(x_prompt, x_sample, cache_latent, cache_krope, state_pool, meta_tokens, norm_mix, w_in, q_a_norm, w_uq, kv_a_norm, w_uk, w_uv, q_norm, k_norm, w_pool, pool_scale, w_o, norm_ffn, w_gate, w_up, w_down):
    depth = norm_mix.shape[0]
    bp, s_len, d_model = x_prompt.shape
    bs, t_len, _ = x_sample.shape
    past = cache_latent.shape[2] - N_META
    consts = _perm_consts()
    tabs_meta = _rope_tables(0, N_META)
    tabs_prompt = _rope_tables(N_META, s_len)
    tabs_sample = _rope_tables(N_META + past, t_len)
    nb_sample = math.gcd(bs, 8)

    xm = meta_tokens.astype(_F32)[None]
    xp = x_prompt
    xs = x_sample
    zero_hist = jnp.zeros((1, HIST_ROWS, POOL_WIDTH), _F32)
    lat_p, kpe_p, pool_p, lat_s, kpe_s, pool_s = [], [], [], [], [], []
    for l in range(depth):
        lw = _layer_weights(l, norm_mix, w_in, q_a_norm, w_uq, kv_a_norm, w_uk, w_uv, q_norm, k_norm,
                            w_pool, pool_scale, w_o, norm_ffn, w_gate, w_up, w_down)
        pm = _pre_call(xm, zero_hist, tabs_meta, lw, consts, nb=1, hist_shared=True, truncated=True)
        pp = _pre_call(xp, pm["ulast"], tabs_prompt, lw, consts, nb=1, hist_shared=True, truncated=False)
        hist_s = jnp.pad(state_pool[l].astype(_F32), ((0, 0), (HIST_ROWS - POOL_STATE, 0), (0, 0)))
        ps = _pre_call(xs, hist_s, tabs_sample, lw, consts, nb=nb_sample, hist_shared=False, truncated=False)

        attn_p = _prompt_attn_call(pp["q"], pp["k"], pp["v"], pm["k"][0], pm["v"][0])
        attn_s = _sample_attn_call(ps["q"], ps["k"], ps["v"], cache_latent[l], cache_krope[l], lw, consts)
        xp = _post_call(xp, pp["pool"], attn_p, lw)
        xs = _post_call(xs, ps["pool"], attn_s, lw)
        if l + 1 < depth:
            attn_m = _meta_attn_call(pm["q"][0], pm["k"][0], pm["v"][0])
            xm = _post_call(xm, pm["pool"], attn_m[None], lw)

        rep = lambda a: jnp.broadcast_to(a, (bp,) + a.shape[1:])
        lat_p.append(jnp.concatenate([rep(pm["lat"]), pp["lat"]], axis=1))
        kpe_p.append(jnp.concatenate([rep(pm["krope"]), pp["krope"]], axis=1))
        pool_p.append(pp["ulast"][:, HIST_ROWS - POOL_STATE:])
        lat_s.append(ps["lat"])
        kpe_s.append(ps["krope"])
        pool_s.append(ps["ulast"][:, HIST_ROWS - POOL_STATE:])
    return (xp, xs, jnp.stack(lat_p), jnp.stack(kpe_p), jnp.stack(pool_p),
            jnp.stack(lat_s), jnp.stack(kpe_s), jnp.stack(pool_s))
```

```python
import functools
import math

import jax
import jax.numpy as jnp
from jax import lax
from jax.experimental import pallas as pl
from jax.experimental.pallas import tpu as pltpu

CHUNK = 64
N_META = 16
N_HEADS = 8
D_NOPE = 64
D_ROPE = 32
D_QK = D_NOPE + D_ROPE
D_V = 64
Q_RANK = 384
KV_RANK = 256
POOL_WINDOWS = (2, 4, 8, 16)
POOL_GROUP = 128
POOL_WIDTH = POOL_GROUP * len(POOL_WINDOWS)
POOL_STATE = max(POOL_WINDOWS) - 1
ROPE_THETA = 10000.0
EPS = 1e-6

LANES = 128
HIST_ROWS = 16
VMEM_LIMIT_BYTES = 56 * 1024 * 1024
ROW_TILE = 512
MASKED = -1e30
EXP2_SAFE_RANGE = 120.0

_F32 = jnp.float32
_BF16 = jnp.bfloat16
_NT = (((1,), (1,)), ((), ()))


def _dot(a, b):
    return jnp.dot(a, b, preferred_element_type=_F32)


def _dot_nt(a, b):
    return lax.dot_general(a, b, _NT, preferred_element_type=_F32)


def _rms(x, g):
    return x * lax.rsqrt(jnp.mean(x * x, axis=-1, keepdims=True) + EPS) * g


def _perm(x, p):
    hi = x.astype(_BF16)
    lo = (x - hi.astype(_F32)).astype(_BF16)
    return _dot(hi, p) + _dot(lo, p)


def _head_norm(x, gain):
    ss = jnp.sum(x * x, axis=-1, keepdims=True)
    return x * lax.rsqrt(ss / D_QK + EPS) * gain


def _pre_kernel(x_ref, hist_ref, cosp_ref, sinp_ref, cos32_ref, sin32_ref, cost_ref, sint_ref,
                gmix_ref, win_ref, gq_ref, wuq_ref, gkv_ref, wuk_ref, wuv_ref, qgain_ref, kgain_ref,
                wpool_ref, pscale_ref, place_ref, swap_ref,
                q_ref, k_ref, v_ref, lat_ref, krope_ref, pool_ref, ulast_ref,
                ext_ref, *, nb, tm, hist_shared, truncated, feature_major):
    t = pl.program_id(1)
    d_model = x_ref.shape[-1]
    rows = nb * tm
    x = x_ref[...].reshape(rows, d_model)
    h = _rms(x, gmix_ref[...]).astype(_BF16)
    z = _dot(h, win_ref[...])
    u = z[:, :POOL_WIDTH]
    q_lat = z[:, POOL_WIDTH:POOL_WIDTH + Q_RANK]
    kv_lat = z[:, POOL_WIDTH + Q_RANK:POOL_WIDTH + Q_RANK + KV_RANK]
    kpe = z[:, POOL_WIDTH + Q_RANK + KV_RANK:POOL_WIDTH + Q_RANK + KV_RANK + D_ROPE]

    for b in range(nb):
        @pl.when(t == 0)
        def _():
            ext_ref[b, 0:HIST_ROWS, :] = hist_ref[0 if hist_shared else b]

        @pl.when(t > 0)
        def _():
            ext_ref[b, 0:HIST_ROWS, :] = ext_ref[b, tm:tm + HIST_ROWS, :]

        ext_ref[b, HIST_ROWS:HIST_ROWS + tm, :] = u[b * tm:(b + 1) * tm]
    for b in range(nb):
        cols = []
        for g, w in enumerate(POOL_WINDOWS):
            sl = slice(g * POOL_GROUP, (g + 1) * POOL_GROUP)
            cur = ext_ref[b, HIST_ROWS:HIST_ROWS + tm, sl]
            acc = cur
            for j in range(1, w):
                acc = acc + ext_ref[b, HIST_ROWS - j:HIST_ROWS - j + tm, sl]
            if truncated:
                row = t * tm + lax.broadcasted_iota(jnp.int32, (tm, 1), 0)
                mean = acc / jnp.minimum(row + 1, w).astype(_F32)
            else:
                mean = acc / float(w)
            cols.append(_dot((mean - cur).astype(_BF16), wpool_ref[g]))
        pool = jnp.concatenate(cols, axis=1) * pscale_ref[...]
        pool_ref[b] = pool.astype(pool_ref.dtype)
        ulast_ref[b] = ext_ref[b, tm:tm + HIST_ROWS, :]

    qn = _rms(q_lat, gq_ref[...]).astype(_BF16)
    c_kv = _rms(kv_lat, gkv_ref[...])
    lat_ref[...] = c_kv.reshape(nb, tm, KV_RANK)
    ckv_b = c_kv.astype(_BF16)
    knope = _dot(ckv_b, wuk_ref[...])

    def per_row(ref):
        tab = ref[...]
        return tab if nb == 1 else jnp.concatenate([tab] * nb, axis=0)

    cosp, sinp = per_row(cosp_ref), per_row(sinp_ref)
    krope = kpe * per_row(cos32_ref) + _perm(kpe, swap_ref[...]) * per_row(sin32_ref)
    krope_ref[...] = krope.reshape(nb, tm, D_ROPE)
    kpe_placed = _perm(krope, place_ref[...])

    for hd in range(N_HEADS):
        kh = _head_norm(knope[:, hd * LANES:(hd + 1) * LANES] + kpe_placed, kgain_ref[...])
        k_ref[:, hd] = kh.astype(k_ref.dtype).reshape(nb, tm, LANES)

    if feature_major:
        half = D_ROPE // 2
        qraw_t = _dot_nt(wuq_ref[...], qn)
        v_t = _dot_nt(wuv_ref[...], ckv_b)
        for pair in range(N_HEADS // 2):
            v_ref[0, pair, 0] = v_t[pair * LANES:(pair + 1) * LANES].astype(v_ref.dtype)
        cos_t, sin_t, gain = cost_ref[...], sint_ref[...], qgain_ref[...]
        for hd in range(N_HEADS):
            base = hd * LANES
            nope = qraw_t[base:base + D_NOPE]
            x1 = qraw_t[base + D_NOPE:base + D_NOPE + half]
            x2 = qraw_t[base + D_NOPE + half:base + D_QK]
            r1 = x1 * cos_t - x2 * sin_t
            r2 = x1 * sin_t + x2 * cos_t
            ss = (jnp.sum(nope * nope, axis=0, keepdims=True) + jnp.sum(r1 * r1, axis=0, keepdims=True)
                  + jnp.sum(r2 * r2, axis=0, keepdims=True))
            scale = lax.rsqrt(ss / D_QK + EPS)
            qh = jnp.concatenate([nope * scale * gain[:D_NOPE], r1 * scale * gain[D_NOPE:D_NOPE + half],
                                  r2 * scale * gain[D_NOPE + half:D_QK],
                                  jnp.zeros((LANES - D_QK, rows), _F32)], axis=0)
            q_ref[0, hd] = qh.astype(q_ref.dtype)
    else:
        qraw = _dot(qn, wuq_ref[...])
        vfull = _dot(ckv_b, wuv_ref[...])
        lane = lax.broadcasted_iota(jnp.int32, (1, LANES), 1)
        first_half = lane < D_NOPE + D_ROPE // 2
        for hd in range(N_HEADS):
            sl = slice(hd * LANES, (hd + 1) * LANES)
            qh = qraw[:, sl]
            partner = jnp.where(first_half, pltpu.roll(qh, LANES - D_ROPE // 2, 1), pltpu.roll(qh, D_ROPE // 2, 1))
            qh = _head_norm(qh * cosp + partner * sinp, qgain_ref[...])
            q_ref[:, hd] = qh.astype(q_ref.dtype).reshape(nb, tm, LANES)
            v_ref[:, hd] = vfull[:, sl].astype(v_ref.dtype).reshape(nb, tm, LANES)


def _const_spec(a):
    nd = a.ndim
    return pl.BlockSpec(a.shape, lambda *_: (0,) * nd, pipeline_mode=pl.Buffered(1))


def _pre_call(x, hist, tabs, lw, consts, *, nb, hist_shared, truncated, feature_major=False):
    bsz, t_len, d_model = x.shape
    tm = t_len if nb > 1 else min(ROW_TILE, t_len)
    assert bsz % nb == 0 and t_len % tm == 0 and tm % HIST_ROWS == 0 and not (feature_major and nb > 1)
    grid = (bsz // nb, t_len // tm)
    fm = "_t" if feature_major else ""
    weights = (lw["gmix"], lw["w_in"], lw["gq"], lw["w_uq" + fm], lw["gkv"], lw["w_uk"], lw["w_uv" + fm],
               lw["qgain" + fm], lw["kgain"], lw["w_pool"], lw["pscale"], consts["place"], consts["swap"])
    hist_spec = (pl.BlockSpec((1, HIST_ROWS, POOL_WIDTH), lambda b, t: (0, 0, 0)) if hist_shared
                 else pl.BlockSpec((nb, HIST_ROWS, POOL_WIDTH), lambda b, t: (b, 0, 0)))
    in_specs = [pl.BlockSpec((nb, tm, d_model), lambda b, t: (b, t, 0)), hist_spec]
    in_specs += [pl.BlockSpec((tm, a.shape[1]), lambda b, t: (t, 0)) for a in tabs[:4]]
    in_specs += [pl.BlockSpec((a.shape[0], tm), lambda b, t: (0, t)) for a in tabs[4:]]
    in_specs += [_const_spec(w) for w in weights]
    head_spec = pl.BlockSpec((nb, N_HEADS, tm, LANES), lambda b, t: (b, 0, t, 0))
    head_shape = jax.ShapeDtypeStruct((bsz, N_HEADS, t_len, LANES), _BF16)
    if feature_major:
        q_spec = pl.BlockSpec((1, N_HEADS, LANES, tm), lambda b, t: (b, 0, 0, t))
        q_shape = jax.ShapeDtypeStruct((bsz, N_HEADS, LANES, t_len), _BF16)
        v_spec = pl.BlockSpec((1, N_HEADS // 2, 1, LANES, tm), lambda b, t: (b, 0, t, 0, 0))
        v_shape = jax.ShapeDtypeStruct((bsz, N_HEADS // 2, t_len // tm, LANES, tm), _BF16)
    else:
        q_spec, q_shape, v_spec, v_shape = head_spec, head_shape, head_spec, head_shape

    def row_out(width, dtype):
        return (pl.BlockSpec((nb, tm, width), lambda b, t: (b, t, 0)),
                jax.ShapeDtypeStruct((bsz, t_len, width), dtype))

    lat_spec, lat_shape = row_out(KV_RANK, _F32)
    kr_spec, kr_shape = row_out(D_ROPE, _F32)
    pool_spec, pool_shape = row_out(POOL_WIDTH, _BF16)
    out_specs = [q_spec, head_spec, v_spec, lat_spec, kr_spec, pool_spec,
                 pl.BlockSpec((nb, HIST_ROWS, POOL_WIDTH), lambda b, t: (b, 0, 0))]
    out_shape = [q_shape, head_shape, v_shape, lat_shape, kr_shape, pool_shape,
                 jax.ShapeDtypeStruct((bsz, HIST_ROWS, POOL_WIDTH), _F32)]
    q, k, v, lat, krope, pool, ulast = pl.pallas_call(
        functools.partial(_pre_kernel, nb=nb, tm=tm, hist_shared=hist_shared, truncated=truncated,
                          feature_major=feature_major),
        grid=grid, in_specs=in_specs, out_specs=out_specs, out_shape=out_shape,
        scratch_shapes=[pltpu.VMEM((nb, HIST_ROWS + tm, POOL_WIDTH), _F32)],
        compiler_params=pltpu.CompilerParams(dimension_semantics=("arbitrary", "arbitrary"),
                                             vmem_limit_bytes=VMEM_LIMIT_BYTES),
        name="pre_mixer",
    )(x, hist, *tabs, *weights)
    return dict(q=q, k=k, v=v, lat=lat, krope=krope, pool=pool, ulast=ulast)


def _post_kernel(x_ref, pool_ref, attn_ref, wop_ref, woa_ref, gffn_ref, wg_ref, wu_ref, wd_ref, o_ref):
    x1 = x_ref[...] + _dot(pool_ref[...], wop_ref[...]) + _dot(attn_ref[...], woa_ref[...])
    hn = _rms(x1, gffn_ref[...]).astype(_BF16)
    g = _dot(hn, wg_ref[...])
    up = _dot(hn, wu_ref[...])
    act = (g * jax.nn.sigmoid(g) * up).astype(_BF16)
    o_ref[...] = x1 + _dot(act, wd_ref[...])


def _post_call(x, pool, attn, lw):
    shape = x.shape
    d_model = shape[-1]
    x2 = x.reshape(-1, d_model)
    n = x2.shape[0]
    tm = min(ROW_TILE, n)
    assert n % tm == 0
    weights = (lw["w_o_pool"], lw["w_o_attn"], lw["gffn"], lw["w_gate"], lw["w_up"], lw["w_down"])

    def rows(width):
        return pl.BlockSpec((tm, width), lambda i: (i, 0))

    out = pl.pallas_call(
        _post_kernel, grid=(n // tm,),
        in_specs=[rows(d_model), rows(POOL_WIDTH), rows(N_HEADS * D_V)] + [_const_spec(w) for w in weights],
        out_specs=rows(d_model), out_shape=jax.ShapeDtypeStruct((n, d_model), _F32),
        compiler_params=pltpu.CompilerParams(dimension_semantics=("arbitrary",),
                                             vmem_limit_bytes=VMEM_LIMIT_BYTES),
        name="post_ffn",
    )(x2, pool.reshape(n, POOL_WIDTH), attn.reshape(n, N_HEADS * D_V), *weights)
    return out.reshape(shape)


def _attn_kernel(kbound_ref, q_ref, k_ref, v_ref, km_ref, vm_ref, o_ref,
                 acc_ref, m_ref, l_ref, s_even_ref, s_odd_ref, *, tile, online):
    qi = pl.program_id(2)

    def scores(hh, k_rows):
        return _dot(k_rows, q_ref[0, hh])

    def consume(hh, s_t, v_cols, mask):
        if mask is not None:
            s_t = jnp.where(mask, s_t, MASKED)
        m_prev = m_ref[hh]
        m_next = jnp.maximum(m_prev, jnp.max(s_t, axis=0, keepdims=True)) if online else m_prev
        p_t = jnp.exp2(s_t - m_next)
        pv = _dot(v_cols, p_t.astype(_BF16))
        l_new = jnp.sum(p_t, axis=0, keepdims=True)
        if online:
            alpha = jnp.exp2(m_prev - m_next)
            acc_ref[hh] = alpha * acc_ref[hh] + pv
            l_ref[hh] = alpha * l_ref[hh] + l_new
            m_ref[hh] = m_next
        else:
            acc_ref[hh] += pv
            l_ref[hh] += l_new

    def key_tile(hh, j):
        return k_ref[0, hh, pl.ds(pl.multiple_of(j * tile, tile), tile), :]

    def value_tile(hh, j):
        return v_ref[0, 0, j, hh * D_V:(hh + 1) * D_V, :]

    for hh in range(2):
        acc_ref[hh] = jnp.zeros(acc_ref.shape[1:], _F32)
        l_ref[hh] = jnp.zeros((1, tile), _F32)
        if online:
            m_ref[hh] = jnp.full((1, tile), MASKED, _F32)
        else:
            qf = q_ref[0, hh].astype(_F32)
            m_ref[hh] = jnp.sqrt(jnp.sum(qf * qf, axis=0, keepdims=True)) * kbound_ref[...]
        consume(hh, scores(hh, km_ref[hh]), vm_ref[hh], None)
        s_even_ref[hh] = scores(hh, key_tile(hh, 0))

    def step(j, s_cur_ref, s_next_ref):
        for hh in range(2):
            s_next_ref[hh] = scores(hh, key_tile(hh, j + 1))
            consume(hh, s_cur_ref[hh], value_tile(hh, j), None)

    def body(j, carry):
        @pl.when(j % 2 == 0)
        def _():
            step(j, s_even_ref, s_odd_ref)

        @pl.when(j % 2 == 1)
        def _():
            step(j, s_odd_ref, s_even_ref)

        return carry

    lax.fori_loop(0, qi, body, 0)
    key_chunk = lax.broadcasted_iota(jnp.int32, (tile, tile), 0) // CHUNK
    query_chunk = lax.broadcasted_iota(jnp.int32, (tile, tile), 1) // CHUNK
    mask = query_chunk >= key_chunk
    for parity, s_fin_ref in enumerate((s_even_ref, s_odd_ref)):
        @pl.when(qi % 2 == parity)
        def _():
            for hh in range(2):
                consume(hh, s_fin_ref[hh], value_tile(hh, qi), mask)

    out_t = jnp.concatenate([acc_ref[0] / l_ref[0], acc_ref[1] / l_ref[1]], axis=0)
    o_ref[0] = out_t.T.astype(o_ref.dtype)


def _prompt_attn_call(kbound, q, k, v, km, vm, *, online):
    bsz, n_heads, _, s_len = q.shape
    tile = v.shape[-1]
    n_tiles = s_len // tile
    assert s_len % tile == 0 and tile % CHUNK == 0 and n_heads % 2 == 0 and 2 * D_V == LANES
    grid = (bsz, n_heads // 2, n_tiles)
    return pl.pallas_call(
        functools.partial(_attn_kernel, tile=tile, online=online), grid=grid,
        in_specs=[pl.BlockSpec((1, 1), lambda b, p, i: (0, 0)),
                  pl.BlockSpec((1, 2, LANES, tile), lambda b, p, i: (b, p, 0, i)),
                  pl.BlockSpec((1, 2, s_len, LANES), lambda b, p, i: (b, p, 0, 0)),
                  pl.BlockSpec((1, 1, n_tiles, LANES, tile), lambda b, p, i: (b, p, 0, 0, 0)),
                  pl.BlockSpec((2, N_META, LANES), lambda b, p, i: (p, 0, 0)),
                  pl.BlockSpec((2, D_V, N_META), lambda b, p, i: (p, 0, 0))],
        out_specs=pl.BlockSpec((1, tile, LANES), lambda b, p, i: (b, i, p)),
        out_shape=jax.ShapeDtypeStruct((bsz, s_len, n_heads * D_V), _BF16),
        scratch_shapes=[pltpu.VMEM((2, D_V, tile), _F32), pltpu.VMEM((2, 1, tile), _F32),
                        pltpu.VMEM((2, 1, tile), _F32), pltpu.VMEM((2, tile, tile), _F32),
                        pltpu.VMEM((2, tile, tile), _F32)],
        compiler_params=pltpu.CompilerParams(dimension_semantics=("arbitrary",) * 3,
                                             vmem_limit_bytes=VMEM_LIMIT_BYTES),
        name="prompt_attn",
    )(kbound, q, k, v, km, vm)


def _meta_attn_kernel(q_ref, k_ref, v_ref, o_ref):
    for pair in range(N_HEADS // 2):
        out = None
        for hd in (2 * pair, 2 * pair + 1):
            s = _dot_nt(q_ref[hd], k_ref[hd])
            p = jnp.exp2(s - jnp.max(s, axis=1, keepdims=True))
            o = _dot(p.astype(_BF16), v_ref[hd]) / jnp.sum(p, axis=1, keepdims=True)
            out = o if out is None else out + o
        o_ref[:, pair * LANES:(pair + 1) * LANES] = out.astype(o_ref.dtype)


def _meta_attn_call(q, k, v):
    n_rows = q.shape[1]
    return pl.pallas_call(
        _meta_attn_kernel, out_shape=jax.ShapeDtypeStruct((n_rows, N_HEADS * D_V), _BF16),
        name="meta_attn",
    )(q, k, v)


def _sample_attn_kernel(q_ref, kn_ref, vn_ref, lat_ref, kpe_ref, wuk_ref, wuv_ref, kgain_ref, place_ref,
                        o_ref, *, past):
    t_len = q_ref.shape[2]
    latb = lat_ref[0].astype(_BF16)
    kpe_placed = _perm(kpe_ref[0], place_ref[...])
    tq = lax.broadcasted_iota(jnp.int32, (t_len, t_len), 0)
    tk = lax.broadcasted_iota(jnp.int32, (t_len, t_len), 1)
    new_mask = (past + tq) // CHUNK >= (past + tk) // CHUNK
    for pair in range(N_HEADS // 2):
        out = None
        for hd in (2 * pair, 2 * pair + 1):
            sl = slice(hd * LANES, (hd + 1) * LANES)
            kc = _head_norm(_dot(latb, wuk_ref[:, sl]) + kpe_placed, kgain_ref[...]).astype(_BF16)
            vc = _dot(latb, wuv_ref[:, sl]).astype(_BF16)
            q = q_ref[0, hd]
            s_meta = _dot_nt(q, kc[:N_META])
            s_past = _dot_nt(q, kc[N_META:])
            s_new = jnp.where(new_mask, _dot_nt(q, kn_ref[0, hd]), MASKED)
            m = jnp.maximum(jnp.maximum(jnp.max(s_meta, axis=1, keepdims=True),
                                        jnp.max(s_past, axis=1, keepdims=True)),
                            jnp.max(s_new, axis=1, keepdims=True))
            p_meta, p_past, p_new = jnp.exp2(s_meta - m), jnp.exp2(s_past - m), jnp.exp2(s_new - m)
            den = (jnp.sum(p_meta, axis=1, keepdims=True) + jnp.sum(p_past, axis=1, keepdims=True)
                   + jnp.sum(p_new, axis=1, keepdims=True))
            num = (_dot(p_meta.astype(_BF16), vc[:N_META]) + _dot(p_past.astype(_BF16), vc[N_META:])
                   + _dot(p_new.astype(_BF16), vn_ref[0, hd]))
            o = num / den
            out = o if out is None else out + o
        o_ref[0, :, pair * LANES:(pair + 1) * LANES] = out.astype(o_ref.dtype)


def _sample_attn_call(q, k_new, v_new, cache_lat, cache_kpe, lw, consts):
    bsz, n_heads, t_len, _ = q.shape
    l_cache = cache_lat.shape[1]
    past = l_cache - N_META
    head_spec = pl.BlockSpec((1, n_heads, t_len, LANES), lambda b: (b, 0, 0, 0))
    weights = (lw["w_uk"], lw["w_uv"], lw["kgain"], consts["place"])
    return pl.pallas_call(
        functools.partial(_sample_attn_kernel, past=past), grid=(bsz,),
        in_specs=[head_spec, head_spec, head_spec,
                  pl.BlockSpec((1, l_cache, KV_RANK), lambda b: (b, 0, 0)),
                  pl.BlockSpec((1, l_cache, D_ROPE), lambda b: (b, 0, 0))] + [_const_spec(w) for w in weights],
        out_specs=pl.BlockSpec((1, t_len, n_heads * D_V), lambda b: (b, 0, 0)),
        out_shape=jax.ShapeDtypeStruct((bsz, t_len, n_heads * D_V), _BF16),
        compiler_params=pltpu.CompilerParams(dimension_semantics=("arbitrary",),
                                             vmem_limit_bytes=VMEM_LIMIT_BYTES),
        name="sample_attn",
    )(q, k_new, v_new, cache_lat, cache_kpe, *weights)


def _pack_heads(w, width, lane_of_head):
    kdim = w.shape[0]
    out = jnp.zeros((kdim, N_HEADS, LANES), w.dtype)
    w3 = w.reshape(kdim, N_HEADS, width)
    for hd in range(N_HEADS):
        out = out.at[:, hd, lane_of_head(hd):lane_of_head(hd) + width].set(w3[:, hd])
    return out.reshape(kdim, N_HEADS * LANES)


def _layer_weights(l, norm_mix, w_in, q_a_norm, w_uq, kv_a_norm, w_uk, w_uv, q_norm, k_norm,
                   w_pool, pool_scale, w_o, norm_ffn, w_gate, w_up, w_down):
    d_in = w_in.shape[2]
    d_in_pad = -(-d_in // (2 * LANES)) * (2 * LANES)
    pad96 = lambda g: jnp.pad(g.astype(_F32), (0, LANES - D_QK))[None, :]
    q_scale = D_QK ** -0.5 * math.log2(math.e)
    return dict(
        gmix=norm_mix[l][None, :].astype(_F32),
        w_in=jnp.pad(w_in[l], ((0, 0), (0, d_in_pad - d_in))).astype(_BF16),
        gq=q_a_norm[l][None, :].astype(_F32),
        w_uq=_pack_heads(w_uq[l], D_QK, lambda hd: 0).astype(_BF16),
        w_uq_t=_pack_heads(w_uq[l], D_QK, lambda hd: 0).astype(_BF16).T,
        gkv=kv_a_norm[l][None, :].astype(_F32),
        w_uk=_pack_heads(w_uk[l], D_NOPE, lambda hd: 0).astype(_BF16),
        w_uv=_pack_heads(w_uv[l], D_V, lambda hd: (hd % 2) * D_V).astype(_BF16),
        w_uv_t=w_uv[l].astype(_BF16).T,
        qgain=pad96(q_norm[l]) * q_scale,
        qgain_t=(pad96(q_norm[l]) * q_scale).T,
        kgain=pad96(k_norm[l]),
        w_pool=w_pool[l].astype(_BF16),
        pscale=pool_scale[l][None, :].astype(_F32),
        w_o_pool=w_o[l][:POOL_WIDTH].astype(_BF16),
        w_o_attn=w_o[l][POOL_WIDTH:].astype(_BF16),
        gffn=norm_ffn[l][None, :].astype(_F32),
        w_gate=w_gate[l].astype(_BF16),
        w_up=w_up[l].astype(_BF16),
        w_down=w_down[l].astype(_BF16),
    )


def _rope_tables(pos0, t_len):
    half = D_ROPE // 2
    inv = ROPE_THETA ** (-jnp.arange(half, dtype=_F32) / half)
    ang = (pos0 + jnp.arange(t_len, dtype=jnp.int32)).astype(_F32)[:, None] * inv[None, :]
    cos, sin = jnp.cos(ang), jnp.sin(ang)
    cos32 = jnp.concatenate([cos, cos], axis=1)
    sin32 = jnp.concatenate([-sin, sin], axis=1)
    ones = jnp.ones((t_len, D_NOPE), _F32)
    zeros = jnp.zeros((t_len, D_NOPE), _F32)
    tail1 = jnp.ones((t_len, LANES - D_QK), _F32)
    tail0 = jnp.zeros((t_len, LANES - D_QK), _F32)
    cosp = jnp.concatenate([ones, cos32, tail1], axis=1)
    sinp = jnp.concatenate([zeros, sin32, tail0], axis=1)
    return (cosp, sinp, cos32, sin32, cos.T, sin.T)


def _perm_consts():
    half = D_ROPE // 2
    idx = jnp.arange(D_ROPE)
    place = jnp.zeros((D_ROPE, LANES), _F32).at[idx, D_NOPE + idx].set(1.0)
    swap = jnp.zeros((D_ROPE, D_ROPE), _F32).at[idx, (idx + half) % D_ROPE].set(1.0)
    return dict(place=place.astype(_BF16), swap=swap.astype(_BF16))


def kernel(x_prompt, x_sample, cache_latent, cache_krope, state_pool, meta_tokens, norm_mix, w_in, q_a_norm, w_uq, kv_a_norm, w_uk, w_uv, q_norm, k_norm, w_pool, pool_scale, w_o, norm_ffn, w_gate, w_up, w_down):
    depth = norm_mix.shape[0]
    bp, s_len, d_model = x_prompt.shape
    bs, t_len, _ = x_sample.shape
    past = cache_latent.shape[2] - N_META
    consts = _perm_consts()
    tabs_meta = _rope_tables(0, N_META)
    tabs_prompt = _rope_tables(N_META, s_len)
    tabs_sample = _rope_tables(N_META + past, t_len)
    nb_sample = math.gcd(bs, 8)

    xm = meta_tokens.astype(_F32)[None]
    xp = x_prompt
    xs = x_sample
    zero_hist = jnp.zeros((1, HIST_ROWS, POOL_WIDTH), _F32)
    lat_p, kpe_p, pool_p, lat_s, kpe_s, pool_s = [], [], [], [], [], []
    for l in range(depth):
        lw = _layer_weights(l, norm_mix, w_in, q_a_norm, w_uq, kv_a_norm, w_uk, w_uv, q_norm, k_norm,
                            w_pool, pool_scale, w_o, norm_ffn, w_gate, w_up, w_down)
        pm = _pre_call(xm, zero_hist, tabs_meta, lw, consts, nb=1, hist_shared=True, truncated=True)
        pp = _pre_call(xp, pm["ulast"], tabs_prompt, lw, consts, nb=1, hist_shared=True, truncated=False,
                       feature_major=True)
        hist_s = jnp.pad(state_pool[l].astype(_F32), ((0, 0), (HIST_ROWS - POOL_STATE, 0), (0, 0)))
        ps = _pre_call(xs, hist_s, tabs_sample, lw, consts, nb=nb_sample, hist_shared=False, truncated=False)

        vm_t = jnp.stack([pm["v"][0, hd, :, (hd % 2) * D_V:(hd % 2 + 1) * D_V].T for hd in range(N_HEADS)])
        kbound = (D_QK ** 0.5 * jnp.max(jnp.abs(lw["kgain"]))).reshape(1, 1)
        qbound = D_QK ** 0.5 * jnp.max(jnp.abs(lw["qgain"]))
        attn_args = (kbound, pp["q"], pp["k"], pp["v"], pm["k"][0], vm_t)
        attn_p = lax.cond(2.0 * qbound * kbound[0, 0] <= EXP2_SAFE_RANGE,
                          functools.partial(_prompt_attn_call, online=False),
                          functools.partial(_prompt_attn_call, online=True), *attn_args)
        attn_s = _sample_attn_call(ps["q"], ps["k"], ps["v"], cache_latent[l], cache_krope[l], lw, consts)
        xp = _post_call(xp, pp["pool"], attn_p, lw)
        xs = _post_call(xs, ps["pool"], attn_s, lw)
        if l + 1 < depth:
            attn_m = _meta_attn_call(pm["q"][0], pm["k"][0], pm["v"][0])
            xm = _post_call(xm, pm["pool"], attn_m[None], lw)

        rep = lambda a: jnp.broadcast_to(a, (bp,) + a.shape[1:])
        lat_p.append(jnp.concatenate([rep(pm["lat"]), pp["lat"]], axis=1))
        kpe_p.append(jnp.concatenate([rep(pm["krope"]), pp["krope"]], axis=1))
        pool_p.append(pp["ulast"][:, HIST_ROWS - POOL_STATE:])
        lat_s.append(ps["lat"])
        kpe_s.append(ps["krope"])
        pool_s.append(ps["ulast"][:, HIST_ROWS - POOL_STATE:])
    return (xp, xs, jnp.stack(lat_p), jnp.stack(kpe_p), jnp.stack(pool_p),
            jnp.stack(lat_s), jnp.stack(kpe_s), jnp.stack(pool_s))
```

```python
import functools
import math

import jax
import jax.numpy as jnp
from jax import lax
from jax.experimental import pallas as pl
from jax.experimental.pallas import tpu as pltpu

CHUNK = 64
N_META = 16
N_HEADS = 8
D_NOPE = 64
D_ROPE = 32
D_QK = D_NOPE + D_ROPE
D_V = 64
Q_RANK = 384
KV_RANK = 256
POOL_WINDOWS = (2, 4, 8, 16)
POOL_GROUP = 128
POOL_WIDTH = POOL_GROUP * len(POOL_WINDOWS)
POOL_STATE = max(POOL_WINDOWS) - 1
ROPE_THETA = 10000.0
EPS = 1e-6

LANES = 128
HIST_ROWS = 16
VMEM_LIMIT_BYTES = 56 * 1024 * 1024
ROW_TILE = 512
MASKED = -1e30
EXP2_SAFE_RANGE = 120.0

_F32 = jnp.float32
_BF16 = jnp.bfloat16
_NT = (((1,), (1,)), ((), ()))


def _dot(a, b):
    return jnp.dot(a, b, preferred_element_type=_F32)


def _dot_nt(a, b):
    return lax.dot_general(a, b, _NT, preferred_element_type=_F32)


def _rms(x, g):
    return x * lax.rsqrt(jnp.mean(x * x, axis=-1, keepdims=True) + EPS) * g


def _perm(x, p):
    hi = x.astype(_BF16)
    lo = (x - hi.astype(_F32)).astype(_BF16)
    return _dot(hi, p) + _dot(lo, p)


def _head_norm(x, gain):
    ss = jnp.sum(x * x, axis=-1, keepdims=True)
    return x * lax.rsqrt(ss / D_QK + EPS) * gain


def _pre_kernel(x_ref, hist_ref, cosp_ref, sinp_ref, cos32_ref, sin32_ref, cost_ref, sint_ref,
                gmix_ref, win_ref, gq_ref, wuq_ref, gkv_ref, wuk_ref, wuv_ref, qgain_ref, kgain_ref,
                wpool_ref, pscale_ref, place_ref, swap_ref,
                q_ref, k_ref, v_ref, lat_ref, krope_ref, pool_ref, ulast_ref,
                ext_ref, *, nb, tm, hist_shared, truncated, feature_major):
    t = pl.program_id(1)
    d_model = x_ref.shape[-1]
    rows = nb * tm
    x = x_ref[...].reshape(rows, d_model)
    h = _rms(x, gmix_ref[...]).astype(_BF16)
    z = _dot(h, win_ref[...])
    u = z[:, :POOL_WIDTH]
    q_lat = z[:, POOL_WIDTH:POOL_WIDTH + Q_RANK]
    kv_lat = z[:, POOL_WIDTH + Q_RANK:POOL_WIDTH + Q_RANK + KV_RANK]
    kpe = z[:, POOL_WIDTH + Q_RANK + KV_RANK:POOL_WIDTH + Q_RANK + KV_RANK + D_ROPE]

    for b in range(nb):
        @pl.when(t == 0)
        def _():
            ext_ref[b, 0:HIST_ROWS, :] = hist_ref[0 if hist_shared else b]

        @pl.when(t > 0)
        def _():
            ext_ref[b, 0:HIST_ROWS, :] = ext_ref[b, tm:tm + HIST_ROWS, :]

        ext_ref[b, HIST_ROWS:HIST_ROWS + tm, :] = u[b * tm:(b + 1) * tm]
    for b in range(nb):
        cols = []
        for g, w in enumerate(POOL_WINDOWS):
            sl = slice(g * POOL_GROUP, (g + 1) * POOL_GROUP)
            cur = ext_ref[b, HIST_ROWS:HIST_ROWS + tm, sl]
            acc = cur
            for j in range(1, w):
                acc = acc + ext_ref[b, HIST_ROWS - j:HIST_ROWS - j + tm, sl]
            if truncated:
                row = t * tm + lax.broadcasted_iota(jnp.int32, (tm, 1), 0)
                mean = acc / jnp.minimum(row + 1, w).astype(_F32)
            else:
                mean = acc / float(w)
            cols.append(_dot((mean - cur).astype(_BF16), wpool_ref[g]))
        pool = jnp.concatenate(cols, axis=1) * pscale_ref[...]
        pool_ref[b] = pool.astype(pool_ref.dtype)
        ulast_ref[b] = ext_ref[b, tm:tm + HIST_ROWS, :]

    qn = _rms(q_lat, gq_ref[...]).astype(_BF16)
    c_kv = _rms(kv_lat, gkv_ref[...])
    lat_ref[...] = c_kv.reshape(lat_ref.shape)
    ckv_b = c_kv.astype(_BF16)
    knope = _dot(ckv_b, wuk_ref[...])

    def per_row(ref):
        tab = ref[...]
        return tab if nb == 1 else jnp.concatenate([tab] * nb, axis=0)

    cosp, sinp = per_row(cosp_ref), per_row(sinp_ref)
    krope = kpe * per_row(cos32_ref) + _perm(kpe, swap_ref[...]) * per_row(sin32_ref)
    krope_ref[...] = krope.reshape(krope_ref.shape)
    kpe_placed = _perm(krope, place_ref[...])

    for hd in range(N_HEADS):
        kh = _head_norm(knope[:, hd * LANES:(hd + 1) * LANES] + kpe_placed, kgain_ref[...])
        k_ref[:, hd] = kh.astype(k_ref.dtype).reshape(nb, tm, LANES)

    if feature_major:
        half = D_ROPE // 2
        qraw_t = _dot_nt(wuq_ref[...], qn)
        v_t = _dot_nt(wuv_ref[...], ckv_b)
        for pair in range(N_HEADS // 2):
            v_ref[0, pair, 0] = v_t[pair * LANES:(pair + 1) * LANES].astype(v_ref.dtype)
        cos_t, sin_t, gain = cost_ref[...], sint_ref[...], qgain_ref[...]
        for hd in range(N_HEADS):
            base = hd * LANES
            nope = qraw_t[base:base + D_NOPE]
            x1 = qraw_t[base + D_NOPE:base + D_NOPE + half]
            x2 = qraw_t[base + D_NOPE + half:base + D_QK]
            r1 = x1 * cos_t - x2 * sin_t
            r2 = x1 * sin_t + x2 * cos_t
            ss = (jnp.sum(nope * nope, axis=0, keepdims=True) + jnp.sum(r1 * r1, axis=0, keepdims=True)
                  + jnp.sum(r2 * r2, axis=0, keepdims=True))
            scale = lax.rsqrt(ss / D_QK + EPS)
            qh = jnp.concatenate([nope * scale * gain[:D_NOPE], r1 * scale * gain[D_NOPE:D_NOPE + half],
                                  r2 * scale * gain[D_NOPE + half:D_QK],
                                  jnp.zeros((LANES - D_QK, rows), _F32)], axis=0)
            q_ref[0, hd] = qh.astype(q_ref.dtype)
    else:
        qraw = _dot(qn, wuq_ref[...])
        vfull = _dot(ckv_b, wuv_ref[...])
        lane = lax.broadcasted_iota(jnp.int32, (1, LANES), 1)
        first_half = lane < D_NOPE + D_ROPE // 2
        for hd in range(N_HEADS):
            sl = slice(hd * LANES, (hd + 1) * LANES)
            qh = qraw[:, sl]
            partner = jnp.where(first_half, pltpu.roll(qh, LANES - D_ROPE // 2, 1), pltpu.roll(qh, D_ROPE // 2, 1))
            qh = _head_norm(qh * cosp + partner * sinp, qgain_ref[...])
            q_ref[:, hd] = qh.astype(q_ref.dtype).reshape(nb, tm, LANES)
            v_ref[:, hd] = vfull[:, sl].astype(v_ref.dtype).reshape(nb, tm, LANES)


def _const_spec(a):
    nd = a.ndim
    return pl.BlockSpec(a.shape, lambda *_: (0,) * nd, pipeline_mode=pl.Buffered(1))


def _skip_refs(fn, start, count):
    def wrapped(*refs, **kw):
        return fn(*refs[:start], *refs[start + count:], **kw)
    return wrapped


def _pre_call(x, hist, tabs, lw, consts, *, nb, hist_shared, truncated, feature_major=False, into=None):
    bsz, t_len, d_model = x.shape
    tm = t_len if nb > 1 else min(ROW_TILE, t_len)
    assert bsz % nb == 0 and t_len % tm == 0 and tm % HIST_ROWS == 0 and not (feature_major and nb > 1)
    grid = (bsz // nb, t_len // tm)
    fm = "_t" if feature_major else ""
    weights = (lw["gmix"], lw["w_in"], lw["gq"], lw["w_uq" + fm], lw["gkv"], lw["w_uk"], lw["w_uv" + fm],
               lw["qgain" + fm], lw["kgain"], lw["w_pool"], lw["pscale"], consts["place"], consts["swap"])
    hist_spec = (pl.BlockSpec((1, HIST_ROWS, POOL_WIDTH), lambda b, t: (0, 0, 0)) if hist_shared
                 else pl.BlockSpec((nb, HIST_ROWS, POOL_WIDTH), lambda b, t: (b, 0, 0)))
    in_specs = [pl.BlockSpec((nb, tm, d_model), lambda b, t: (b, t, 0)), hist_spec]
    in_specs += [pl.BlockSpec((tm, a.shape[1]), lambda b, t: (t, 0)) for a in tabs[:4]]
    in_specs += [pl.BlockSpec((a.shape[0], tm), lambda b, t: (0, t)) for a in tabs[4:]]
    in_specs += [_const_spec(w) for w in weights]
    head_spec = pl.BlockSpec((nb, N_HEADS, tm, LANES), lambda b, t: (b, 0, t, 0))
    head_shape = jax.ShapeDtypeStruct((bsz, N_HEADS, t_len, LANES), _BF16)
    if feature_major:
        q_spec = pl.BlockSpec((1, N_HEADS, LANES, tm), lambda b, t: (b, 0, 0, t))
        q_shape = jax.ShapeDtypeStruct((bsz, N_HEADS, LANES, t_len), _BF16)
        v_spec = pl.BlockSpec((1, N_HEADS // 2, 1, LANES, tm), lambda b, t: (b, 0, t, 0, 0))
        v_shape = jax.ShapeDtypeStruct((bsz, N_HEADS // 2, t_len // tm, LANES, tm), _BF16)
    else:
        q_spec, q_shape, v_spec, v_shape = head_spec, head_shape, head_spec, head_shape

    def row_out(width, dtype):
        return (pl.BlockSpec((nb, tm, width), lambda b, t: (b, t, 0)),
                jax.ShapeDtypeStruct((bsz, t_len, width), dtype))

    body = functools.partial(_pre_kernel, nb=nb, tm=tm, hist_shared=hist_shared, truncated=truncated,
                             feature_major=feature_major)
    operands = [x, hist, *tabs, *weights]
    aliases = {}
    if into is None:
        lat_spec, lat_shape = row_out(KV_RANK, _F32)
        kr_spec, kr_shape = row_out(D_ROPE, _F32)
    else:
        layer, depth, bufs = into
        assert nb == 1

        def seq_out(width):
            block = tuple(pl.Element(n) for n in (1, 1, tm, width))
            return (pl.BlockSpec(block, lambda b, t: (layer, b, (1 + t * (tm // N_META)) * N_META, 0)),
                    jax.ShapeDtypeStruct((depth, bsz, N_META + t_len, width), _F32))

        lat_spec, lat_shape = seq_out(KV_RANK)
        kr_spec, kr_shape = seq_out(D_ROPE)
        if bufs is not None:
            body = _skip_refs(body, len(operands), len(bufs))
            aliases = {len(operands): 3, len(operands) + 1: 4}
            in_specs += [pl.BlockSpec(memory_space=pl.ANY)] * len(bufs)
            operands += list(bufs)
    pool_spec, pool_shape = row_out(POOL_WIDTH, _BF16)
    out_specs = [q_spec, head_spec, v_spec, lat_spec, kr_spec, pool_spec,
                 pl.BlockSpec((nb, HIST_ROWS, POOL_WIDTH), lambda b, t: (b, 0, 0))]
    out_shape = [q_shape, head_shape, v_shape, lat_shape, kr_shape, pool_shape,
                 jax.ShapeDtypeStruct((bsz, HIST_ROWS, POOL_WIDTH), _F32)]
    q, k, v, lat, krope, pool, ulast = pl.pallas_call(
        body, grid=grid, in_specs=in_specs, out_specs=out_specs, out_shape=out_shape,
        input_output_aliases=aliases,
        scratch_shapes=[pltpu.VMEM((nb, HIST_ROWS + tm, POOL_WIDTH), _F32)],
        compiler_params=pltpu.CompilerParams(dimension_semantics=("arbitrary", "arbitrary"),
                                             vmem_limit_bytes=VMEM_LIMIT_BYTES),
        name="pre_mixer",
    )(*operands)
    return dict(q=q, k=k, v=v, lat=lat, krope=krope, pool=pool, ulast=ulast)


def _meta_rows_kernel(lat_any, kr_any, mlat_ref, mkr_ref, lat_ref, kr_ref):
    del lat_any, kr_any
    lat_ref[0, 0] = mlat_ref[0]
    kr_ref[0, 0] = mkr_ref[0]


def _meta_rows_call(lat_all, kr_all, meta_lat, meta_kr):
    depth, bsz = lat_all.shape[:2]

    def src(width):
        return pl.BlockSpec((1, N_META, width), lambda l, b: (l, 0, 0))

    def dst(width):
        return pl.BlockSpec((1, 1, N_META, width), lambda l, b: (l, b, 0, 0))

    return pl.pallas_call(
        _meta_rows_kernel, grid=(depth, bsz),
        in_specs=[pl.BlockSpec(memory_space=pl.ANY)] * 2 + [src(KV_RANK), src(D_ROPE)],
        out_specs=[dst(KV_RANK), dst(D_ROPE)],
        out_shape=[jax.ShapeDtypeStruct(lat_all.shape, lat_all.dtype), jax.ShapeDtypeStruct(kr_all.shape, kr_all.dtype)],
        input_output_aliases={0: 0, 1: 1},
        compiler_params=pltpu.CompilerParams(dimension_semantics=("arbitrary", "arbitrary")),
        name="meta_rows",
    )(lat_all, kr_all, meta_lat, meta_kr)


def _post_kernel(x_ref, pool_ref, attn_ref, wop_ref, woa_ref, gffn_ref, wg_ref, wu_ref, wd_ref, o_ref):
    x1 = x_ref[...] + _dot(pool_ref[...], wop_ref[...]) + _dot(attn_ref[...], woa_ref[...])
    hn = _rms(x1, gffn_ref[...]).astype(_BF16)
    g = _dot(hn, wg_ref[...])
    up = _dot(hn, wu_ref[...])
    act = (g * jax.nn.sigmoid(g) * up).astype(_BF16)
    o_ref[...] = x1 + _dot(act, wd_ref[...])


def _post_call(x, pool, attn, lw):
    shape = x.shape
    d_model = shape[-1]
    x2 = x.reshape(-1, d_model)
    n = x2.shape[0]
    tm = min(ROW_TILE, n)
    assert n % tm == 0
    weights = (lw["w_o_pool"], lw["w_o_attn"], lw["gffn"], lw["w_gate"], lw["w_up"], lw["w_down"])

    def rows(width):
        return pl.BlockSpec((tm, width), lambda i: (i, 0))

    out = pl.pallas_call(
        _post_kernel, grid=(n // tm,),
        in_specs=[rows(d_model), rows(POOL_WIDTH), rows(N_HEADS * D_V)] + [_const_spec(w) for w in weights],
        out_specs=rows(d_model), out_shape=jax.ShapeDtypeStruct((n, d_model), _F32),
        compiler_params=pltpu.CompilerParams(dimension_semantics=("arbitrary",),
                                             vmem_limit_bytes=VMEM_LIMIT_BYTES),
        name="post_ffn",
    )(x2, pool.reshape(n, POOL_WIDTH), attn.reshape(n, N_HEADS * D_V), *weights)
    return out.reshape(shape)


def _attn_kernel(kbound_ref, q_ref, k_ref, v_ref, km_ref, vm_ref, o_ref,
                 acc_ref, m_ref, l_ref, s_even_ref, s_odd_ref, *, tile, online):
    qi = pl.program_id(2)

    def scores(hh, k_rows):
        return _dot(k_rows, q_ref[0, hh])

    def consume(hh, s_t, v_cols, mask):
        if mask is not None:
            s_t = jnp.where(mask, s_t, MASKED)
        m_prev = m_ref[hh]
        m_next = jnp.maximum(m_prev, jnp.max(s_t, axis=0, keepdims=True)) if online else m_prev
        p_t = jnp.exp2(s_t - m_next)
        pv = _dot(v_cols, p_t.astype(_BF16))
        l_new = jnp.sum(p_t, axis=0, keepdims=True)
        if online:
            alpha = jnp.exp2(m_prev - m_next)
            acc_ref[hh] = alpha * acc_ref[hh] + pv
            l_ref[hh] = alpha * l_ref[hh] + l_new
            m_ref[hh] = m_next
        else:
            acc_ref[hh] += pv
            l_ref[hh] += l_new

    def key_tile(hh, j):
        return k_ref[0, hh, pl.ds(pl.multiple_of(j * tile, tile), tile), :]

    def value_tile(hh, j):
        return v_ref[0, 0, j, hh * D_V:(hh + 1) * D_V, :]

    for hh in range(2):
        acc_ref[hh] = jnp.zeros(acc_ref.shape[1:], _F32)
        l_ref[hh] = jnp.zeros((1, tile), _F32)
        if online:
            m_ref[hh] = jnp.full((1, tile), MASKED, _F32)
        else:
            qf = q_ref[0, hh].astype(_F32)
            m_ref[hh] = jnp.sqrt(jnp.sum(qf * qf, axis=0, keepdims=True)) * kbound_ref[...]
        consume(hh, scores(hh, km_ref[hh]), vm_ref[hh], None)
        s_even_ref[hh] = scores(hh, key_tile(hh, 0))

    def step(j, s_cur_ref, s_next_ref):
        for hh in range(2):
            s_next_ref[hh] = scores(hh, key_tile(hh, j + 1))
            consume(hh, s_cur_ref[hh], value_tile(hh, j), None)

    def two_steps(jj, carry):
        step(2 * jj, s_even_ref, s_odd_ref)
        step(2 * jj + 1, s_odd_ref, s_even_ref)
        return carry

    lax.fori_loop(0, qi // 2, two_steps, 0)

    @pl.when(qi % 2 == 1)
    def _():
        step(qi - 1, s_even_ref, s_odd_ref)

    key_chunk = lax.broadcasted_iota(jnp.int32, (tile, tile), 0) // CHUNK
    query_chunk = lax.broadcasted_iota(jnp.int32, (tile, tile), 1) // CHUNK
    mask = query_chunk >= key_chunk
    for parity, s_fin_ref in enumerate((s_even_ref, s_odd_ref)):
        @pl.when(qi % 2 == parity)
        def _():
            for hh in range(2):
                consume(hh, s_fin_ref[hh], value_tile(hh, qi), mask)

    out_t = jnp.concatenate([acc_ref[0] / l_ref[0], acc_ref[1] / l_ref[1]], axis=0)
    o_ref[0] = out_t.T.astype(o_ref.dtype)


def _prompt_attn_call(kbound, q, k, v, km, vm, *, online):
    bsz, n_heads, _, s_len = q.shape
    tile = v.shape[-1]
    n_tiles = s_len // tile
    assert s_len % tile == 0 and tile % CHUNK == 0 and n_heads % 2 == 0 and 2 * D_V == LANES
    grid = (bsz, n_heads // 2, n_tiles)
    return pl.pallas_call(
        functools.partial(_attn_kernel, tile=tile, online=online), grid=grid,
        in_specs=[pl.BlockSpec((1, 1), lambda b, p, i: (0, 0)),
                  pl.BlockSpec((1, 2, LANES, tile), lambda b, p, i: (b, p, 0, i)),
                  pl.BlockSpec((1, 2, s_len, LANES), lambda b, p, i: (b, p, 0, 0)),
                  pl.BlockSpec((1, 1, n_tiles, LANES, tile), lambda b, p, i: (b, p, 0, 0, 0)),
                  pl.BlockSpec((2, N_META, LANES), lambda b, p, i: (p, 0, 0)),
                  pl.BlockSpec((2, D_V, N_META), lambda b, p, i: (p, 0, 0))],
        out_specs=pl.BlockSpec((1, tile, LANES), lambda b, p, i: (b, i, p)),
        out_shape=jax.ShapeDtypeStruct((bsz, s_len, n_heads * D_V), _BF16),
        scratch_shapes=[pltpu.VMEM((2, D_V, tile), _F32), pltpu.VMEM((2, 1, tile), _F32),
                        pltpu.VMEM((2, 1, tile), _F32), pltpu.VMEM((2, tile, tile), _F32),
                        pltpu.VMEM((2, tile, tile), _F32)],
        compiler_params=pltpu.CompilerParams(dimension_semantics=("arbitrary",) * 3,
                                             vmem_limit_bytes=VMEM_LIMIT_BYTES),
        name="prompt_attn",
    )(kbound, q, k, v, km, vm)


def _meta_attn_kernel(q_ref, k_ref, v_ref, o_ref):
    for pair in range(N_HEADS // 2):
        out = None
        for hd in (2 * pair, 2 * pair + 1):
            s = _dot_nt(q_ref[hd], k_ref[hd])
            p = jnp.exp2(s - jnp.max(s, axis=1, keepdims=True))
            o = _dot(p.astype(_BF16), v_ref[hd]) / jnp.sum(p, axis=1, keepdims=True)
            out = o if out is None else out + o
        o_ref[:, pair * LANES:(pair + 1) * LANES] = out.astype(o_ref.dtype)


def _meta_attn_call(q, k, v):
    n_rows = q.shape[1]
    return pl.pallas_call(
        _meta_attn_kernel, out_shape=jax.ShapeDtypeStruct((n_rows, N_HEADS * D_V), _BF16),
        name="meta_attn",
    )(q, k, v)


def _row_sums_t(x):
    ones = jnp.ones((8, x.shape[1]), _BF16)
    hi = x.astype(_BF16)
    lo = (x - hi.astype(_F32)).astype(_BF16)
    return (_dot_nt(ones, hi) + _dot_nt(ones, lo))[0:1]


def _sample_attn_kernel(q_ref, kn_ref, vn_ref, lat_ref, kpe_ref, wukt_ref, wukq_ref, wuv_ref, kgain_ref, place_ref,
                        o_ref, *, past):
    t_len = q_ref.shape[2]
    kgain = kgain_ref[...]
    qg = [(q_ref[0, hd].astype(_F32) * kgain).astype(_BF16) for hd in range(N_HEADS)]
    q_all = jnp.concatenate(qg, axis=0)
    q_lat = jnp.concatenate([_dot(qg[hd], wukq_ref[hd]) for hd in range(N_HEADS)], axis=0).astype(_BF16)

    def cached_scores(lo, hi):
        lat = lat_ref[0, 0, lo:hi].astype(_BF16)
        kpe = kpe_ref[0, 0, lo:hi]
        s = _dot_nt(q_lat, lat) + _dot_nt(q_all, _perm(kpe, place_ref[...]).astype(_BF16))
        ss_rope = _row_sums_t(kpe * kpe)
        rows = []
        for pair in range(N_HEADS // 2):
            k_t = _dot_nt(wukt_ref[pair * LANES:(pair + 1) * LANES], lat)
            for hh in range(2):
                hd = 2 * pair + hh
                k_h = k_t[hh * D_NOPE:(hh + 1) * D_NOPE]
                r = lax.rsqrt((jnp.sum(k_h * k_h, axis=0, keepdims=True) + ss_rope) / D_QK + EPS)
                rows.append(s[hd * t_len:(hd + 1) * t_len] * r)
        return jnp.concatenate(rows, axis=0), lat

    s_meta, lat_meta = cached_scores(0, N_META)
    s_past, lat_past = cached_scores(N_META, N_META + past)
    tq = lax.broadcasted_iota(jnp.int32, (t_len, t_len), 0)
    tk = lax.broadcasted_iota(jnp.int32, (t_len, t_len), 1)
    new_mask = (past + tq) // CHUNK >= (past + tk) // CHUNK
    s_new = jnp.concatenate([jnp.where(new_mask, _dot_nt(q_ref[0, hd], kn_ref[0, hd]), MASKED)
                             for hd in range(N_HEADS)], axis=0)
    m = jnp.maximum(jnp.maximum(jnp.max(s_meta, axis=1, keepdims=True), jnp.max(s_past, axis=1, keepdims=True)),
                    jnp.max(s_new, axis=1, keepdims=True))
    p_meta, p_past, p_new = jnp.exp2(s_meta - m), jnp.exp2(s_past - m), jnp.exp2(s_new - m)
    den = (jnp.sum(p_meta, axis=1, keepdims=True) + jnp.sum(p_past, axis=1, keepdims=True)
           + jnp.sum(p_new, axis=1, keepdims=True))
    p_lat = (_dot(p_meta.astype(_BF16), lat_meta) + _dot(p_past.astype(_BF16), lat_past)).astype(_BF16)
    p_new = p_new.astype(_BF16)
    for pair in range(N_HEADS // 2):
        out = None
        for hd in (2 * pair, 2 * pair + 1):
            rows = slice(hd * t_len, (hd + 1) * t_len)
            o = (_dot(p_lat[rows], wuv_ref[:, hd * LANES:(hd + 1) * LANES]) + _dot(p_new[rows], vn_ref[0, hd])) / den[rows]
            out = o if out is None else out + o
        o_ref[0, :, pair * LANES:(pair + 1) * LANES] = out.astype(o_ref.dtype)


def _sample_attn_call(layer, q, k_new, v_new, cache_lat, cache_kpe, lw, consts):
    bsz, n_heads, t_len, _ = q.shape
    l_cache = cache_lat.shape[2]
    past = l_cache - N_META
    assert past % LANES == 0 and t_len % HIST_ROWS == 0
    head_spec = pl.BlockSpec((1, n_heads, t_len, LANES), lambda b: (b, 0, 0, 0))
    weights = (lw["w_uk_t"], lw["w_uk_q"], lw["w_uv"], lw["kgain"], consts["place"])
    return pl.pallas_call(
        functools.partial(_sample_attn_kernel, past=past), grid=(bsz,),
        in_specs=[head_spec, head_spec, head_spec,
                  pl.BlockSpec((1, 1, l_cache, KV_RANK), lambda b: (layer, b, 0, 0)),
                  pl.BlockSpec((1, 1, l_cache, D_ROPE), lambda b: (layer, b, 0, 0))]
                 + [_const_spec(w) for w in weights],
        out_specs=pl.BlockSpec((1, t_len, n_heads * D_V), lambda b: (b, 0, 0)),
        out_shape=jax.ShapeDtypeStruct((bsz, t_len, n_heads * D_V), _BF16),
        compiler_params=pltpu.CompilerParams(dimension_semantics=("arbitrary",),
                                             vmem_limit_bytes=VMEM_LIMIT_BYTES),
        name="sample_attn",
    )(q, k_new, v_new, cache_lat, cache_kpe, *weights)


def _pack_heads(w, width, alternate=False):
    kdim = w.shape[0]
    w3 = w.reshape(kdim, N_HEADS, width)
    low = jnp.pad(w3, ((0, 0), (0, 0), (0, LANES - width)))
    if alternate:
        high = jnp.pad(w3, ((0, 0), (0, 0), (LANES - width, 0)))
        odd = (jnp.arange(N_HEADS) % 2 == 1)[None, :, None]
        low = jnp.where(odd, high, low)
    return low.reshape(kdim, N_HEADS * LANES)


def _layer_weights(l, norm_mix, w_in, q_a_norm, w_uq, kv_a_norm, w_uk, w_uv, q_norm, k_norm,
                   w_pool, pool_scale, w_o, norm_ffn, w_gate, w_up, w_down):
    d_in = w_in.shape[2]
    d_in_pad = -(-d_in // (2 * LANES)) * (2 * LANES)
    pad96 = lambda g: jnp.pad(g.astype(_F32), (0, LANES - D_QK))[None, :]
    q_scale = D_QK ** -0.5 * math.log2(math.e)
    return dict(
        gmix=norm_mix[l][None, :].astype(_F32),
        w_in=jnp.pad(w_in[l], ((0, 0), (0, d_in_pad - d_in))).astype(_BF16),
        gq=q_a_norm[l][None, :].astype(_F32),
        w_uq=_pack_heads(w_uq[l], D_QK).astype(_BF16),
        w_uq_t=_pack_heads(w_uq[l], D_QK).astype(_BF16).T,
        gkv=kv_a_norm[l][None, :].astype(_F32),
        w_uk=_pack_heads(w_uk[l], D_NOPE).astype(_BF16),
        w_uk_t=w_uk[l].astype(_BF16).T,
        w_uk_q=jnp.pad(w_uk[l].astype(_BF16).T.reshape(N_HEADS, D_NOPE, KV_RANK),
                       ((0, 0), (0, LANES - D_NOPE), (0, 0))),
        w_uv=_pack_heads(w_uv[l], D_V, alternate=True).astype(_BF16),
        w_uv_t=w_uv[l].astype(_BF16).T,
        qgain=pad96(q_norm[l]) * q_scale,
        qgain_t=(pad96(q_norm[l]) * q_scale).T,
        kgain=pad96(k_norm[l]),
        w_pool=w_pool[l].astype(_BF16),
        pscale=pool_scale[l][None, :].astype(_F32),
        w_o_pool=w_o[l][:POOL_WIDTH].astype(_BF16),
        w_o_attn=w_o[l][POOL_WIDTH:].astype(_BF16),
        gffn=norm_ffn[l][None, :].astype(_F32),
        w_gate=w_gate[l].astype(_BF16),
        w_up=w_up[l].astype(_BF16),
        w_down=w_down[l].astype(_BF16),
    )


def _rope_tables(pos0, t_len):
    half = D_ROPE // 2
    inv = ROPE_THETA ** (-jnp.arange(half, dtype=_F32) / half)
    ang = (pos0 + jnp.arange(t_len, dtype=jnp.int32)).astype(_F32)[:, None] * inv[None, :]
    cos, sin = jnp.cos(ang), jnp.sin(ang)
    cos32 = jnp.concatenate([cos, cos], axis=1)
    sin32 = jnp.concatenate([-sin, sin], axis=1)
    ones = jnp.ones((t_len, D_NOPE), _F32)
    zeros = jnp.zeros((t_len, D_NOPE), _F32)
    tail1 = jnp.ones((t_len, LANES - D_QK), _F32)
    tail0 = jnp.zeros((t_len, LANES - D_QK), _F32)
    cosp = jnp.concatenate([ones, cos32, tail1], axis=1)
    sinp = jnp.concatenate([zeros, sin32, tail0], axis=1)
    return (cosp, sinp, cos32, sin32, cos.T, sin.T)


def _perm_consts():
    half = D_ROPE // 2
    idx = jnp.arange(D_ROPE)
    place = jnp.zeros((D_ROPE, LANES), _F32).at[idx, D_NOPE + idx].set(1.0)
    swap = jnp.zeros((D_ROPE, D_ROPE), _F32).at[idx, (idx + half) % D_ROPE].set(1.0)
    return dict(place=place.astype(_BF16), swap=swap.astype(_BF16))


def kernel(x_prompt, x_sample, cache_latent, cache_krope, state_pool, meta_tokens, norm_mix, w_in, q_a_norm, w_uq, kv_a_norm, w_uk, w_uv, q_norm, k_norm, w_pool, pool_scale, w_o, norm_ffn, w_gate, w_up, w_down):
    depth = norm_mix.shape[0]
    bp, s_len, d_model = x_prompt.shape
    bs, t_len, _ = x_sample.shape
    past = cache_latent.shape[2] - N_META
    consts = _perm_consts()
    tabs_meta = _rope_tables(0, N_META)
    tabs_prompt = _rope_tables(N_META, s_len)
    tabs_sample = _rope_tables(N_META + past, t_len)
    nb_sample = math.gcd(bs, 8)

    xm = meta_tokens.astype(_F32)[None]
    xp = x_prompt
    xs = x_sample
    zero_hist = jnp.zeros((1, HIST_ROWS, POOL_WIDTH), _F32)
    seq_bufs = None
    meta_lat, meta_kr, pool_p, lat_s, kpe_s, pool_s = [], [], [], [], [], []
    for l in range(depth):
        lw = _layer_weights(l, norm_mix, w_in, q_a_norm, w_uq, kv_a_norm, w_uk, w_uv, q_norm, k_norm,
                            w_pool, pool_scale, w_o, norm_ffn, w_gate, w_up, w_down)
        pm = _pre_call(xm, zero_hist, tabs_meta, lw, consts, nb=1, hist_shared=True, truncated=True)
        pp = _pre_call(xp, pm["ulast"], tabs_prompt, lw, consts, nb=1, hist_shared=True, truncated=False,
                       feature_major=True, into=(l, depth, seq_bufs))
        seq_bufs = (pp["lat"], pp["krope"])
        hist_s = jnp.pad(state_pool[l].astype(_F32), ((0, 0), (HIST_ROWS - POOL_STATE, 0), (0, 0)))
        ps = _pre_call(xs, hist_s, tabs_sample, lw, consts, nb=nb_sample, hist_shared=False, truncated=False)

        vm_t = jnp.stack([pm["v"][0, hd, :, (hd % 2) * D_V:(hd % 2 + 1) * D_V].T for hd in range(N_HEADS)])
        kbound = (D_QK ** 0.5 * jnp.max(jnp.abs(lw["kgain"]))).reshape(1, 1)
        qbound = D_QK ** 0.5 * jnp.max(jnp.abs(lw["qgain"]))
        attn_args = (kbound, pp["q"], pp["k"], pp["v"], pm["k"][0], vm_t)
        attn_p = lax.cond(2.0 * qbound * kbound[0, 0] <= EXP2_SAFE_RANGE,
                          functools.partial(_prompt_attn_call, online=False),
                          functools.partial(_prompt_attn_call, online=True), *attn_args)
        attn_s = _sample_attn_call(l, ps["q"], ps["k"], ps["v"], cache_latent, cache_krope, lw, consts)
        xp = _post_call(xp, pp["pool"], attn_p, lw)
        xs = _post_call(xs, ps["pool"], attn_s, lw)
        if l + 1 < depth:
            attn_m = _meta_attn_call(pm["q"][0], pm["k"][0], pm["v"][0])
            xm = _post_call(xm, pm["pool"], attn_m[None], lw)

        meta_lat.append(pm["lat"][0])
        meta_kr.append(pm["krope"][0])
        pool_p.append(pp["ulast"][:, HIST_ROWS - POOL_STATE:])
        lat_s.append(ps["lat"])
        kpe_s.append(ps["krope"])
        pool_s.append(ps["ulast"][:, HIST_ROWS - POOL_STATE:])
    lat_p, kpe_p = _meta_rows_call(*seq_bufs, jnp.stack(meta_lat), jnp.stack(meta_kr))
    return (xp, xs, lat_p, kpe_p, jnp.stack(pool_p), jnp.stack(lat_s), jnp.stack(kpe_s), jnp.stack(pool_s))
```

```python
import functools
import math

import jax
import jax.numpy as jnp
from jax import lax
from jax.experimental import pallas as pl
from jax.experimental.pallas import tpu as pltpu

CHUNK = 64
N_META = 16
N_HEADS = 8
D_NOPE = 64
D_ROPE = 32
D_QK = D_NOPE + D_ROPE
D_V = 64
Q_RANK = 384
KV_RANK = 256
POOL_WINDOWS = (2, 4, 8, 16)
POOL_GROUP = 128
POOL_WIDTH = POOL_GROUP * len(POOL_WINDOWS)
POOL_STATE = max(POOL_WINDOWS) - 1
ROPE_THETA = 10000.0
EPS = 1e-6

LANES = 128
HIST_ROWS = 16
VMEM_LIMIT_BYTES = 56 * 1024 * 1024
ROW_TILE = 512
ATTN_HEADS_PER_STEP = 4
MASKED = -1e30
EXP2_SAFE_RANGE = 120.0

_F32 = jnp.float32
_BF16 = jnp.bfloat16
_NT = (((1,), (1,)), ((), ()))


def _dot(a, b):
    return jnp.dot(a, b, preferred_element_type=_F32)


def _dot_nt(a, b):
    return lax.dot_general(a, b, _NT, preferred_element_type=_F32)


def _rms(x, g):
    return x * lax.rsqrt(jnp.mean(x * x, axis=-1, keepdims=True) + EPS) * g


def _perm(x, p):
    hi = x.astype(_BF16)
    lo = (x - hi.astype(_F32)).astype(_BF16)
    return _dot(hi, p) + _dot(lo, p)


def _head_norm(x, gain):
    ss = jnp.sum(x * x, axis=-1, keepdims=True)
    return x * lax.rsqrt(ss / D_QK + EPS) * gain


def _pre_kernel(x_ref, hist_ref, cosp_ref, sinp_ref, cos32_ref, sin32_ref, cost_ref, sint_ref,
                gmix_ref, win_ref, gq_ref, wuq_ref, gkv_ref, wuk_ref, wuv_ref, qgain_ref, kgain_ref,
                wpool_ref, pscale_ref, place_ref, swap_ref,
                q_ref, k_ref, v_ref, lat_ref, krope_ref, pool_ref, ulast_ref,
                ext_ref, *, nb, tm, hist_shared, truncated, feature_major):
    t = pl.program_id(1)
    d_model = x_ref.shape[-1]
    rows = nb * tm
    x = x_ref[...].reshape(rows, d_model)
    h = _rms(x, gmix_ref[...]).astype(_BF16)
    z = _dot(h, win_ref[...])
    u = z[:, :POOL_WIDTH]
    q_lat = z[:, POOL_WIDTH:POOL_WIDTH + Q_RANK]
    kv_lat = z[:, POOL_WIDTH + Q_RANK:POOL_WIDTH + Q_RANK + KV_RANK]
    kpe = z[:, POOL_WIDTH + Q_RANK + KV_RANK:POOL_WIDTH + Q_RANK + KV_RANK + D_ROPE]

    for b in range(nb):
        @pl.when(t == 0)
        def _():
            ext_ref[b, 0:HIST_ROWS, :] = hist_ref[0 if hist_shared else b]

        @pl.when(t > 0)
        def _():
            ext_ref[b, 0:HIST_ROWS, :] = ext_ref[b, tm:tm + HIST_ROWS, :]

        ext_ref[b, HIST_ROWS:HIST_ROWS + tm, :] = u[b * tm:(b + 1) * tm]
    for b in range(nb):
        cols = []
        for g, w in enumerate(POOL_WINDOWS):
            sl = slice(g * POOL_GROUP, (g + 1) * POOL_GROUP)
            cur = ext_ref[b, HIST_ROWS:HIST_ROWS + tm, sl]
            acc = cur
            for j in range(1, w):
                acc = acc + ext_ref[b, HIST_ROWS - j:HIST_ROWS - j + tm, sl]
            if truncated:
                row = t * tm + lax.broadcasted_iota(jnp.int32, (tm, 1), 0)
                mean = acc / jnp.minimum(row + 1, w).astype(_F32)
            else:
                mean = acc / float(w)
            cols.append(_dot((mean - cur).astype(_BF16), wpool_ref[g]))
        pool = jnp.concatenate(cols, axis=1) * pscale_ref[...]
        pool_ref[b] = pool.astype(pool_ref.dtype)
        ulast_ref[b] = ext_ref[b, tm:tm + HIST_ROWS, :]

    qn = _rms(q_lat, gq_ref[...]).astype(_BF16)
    c_kv = _rms(kv_lat, gkv_ref[...])
    lat_ref[...] = c_kv.reshape(lat_ref.shape)
    ckv_b = c_kv.astype(_BF16)
    knope = _dot(ckv_b, wuk_ref[...])

    def per_row(ref):
        tab = ref[...]
        return tab if nb == 1 else jnp.concatenate([tab] * nb, axis=0)

    cosp, sinp = per_row(cosp_ref), per_row(sinp_ref)
    krope = kpe * per_row(cos32_ref) + _perm(kpe, swap_ref[...]) * per_row(sin32_ref)
    krope_ref[...] = krope.reshape(krope_ref.shape)
    kpe_placed = _perm(krope, place_ref[...])

    for hd in range(N_HEADS):
        kh = _head_norm(knope[:, hd * LANES:(hd + 1) * LANES] + kpe_placed, kgain_ref[...])
        k_ref[:, hd] = kh.astype(k_ref.dtype).reshape(nb, tm, LANES)

    if feature_major:
        half = D_ROPE // 2
        qraw_t = _dot_nt(wuq_ref[...], qn)
        v_t = _dot_nt(wuv_ref[...], ckv_b)
        for pair in range(N_HEADS // 2):
            v_ref[0, pair, 0] = v_t[pair * LANES:(pair + 1) * LANES].astype(v_ref.dtype)
        cos_t, sin_t, gain = cost_ref[...], sint_ref[...], qgain_ref[...]
        for hd in range(N_HEADS):
            base = hd * LANES
            nope = qraw_t[base:base + D_NOPE]
            x1 = qraw_t[base + D_NOPE:base + D_NOPE + half]
            x2 = qraw_t[base + D_NOPE + half:base + D_QK]
            r1 = x1 * cos_t - x2 * sin_t
            r2 = x1 * sin_t + x2 * cos_t
            ss = (jnp.sum(nope * nope, axis=0, keepdims=True) + jnp.sum(r1 * r1, axis=0, keepdims=True)
                  + jnp.sum(r2 * r2, axis=0, keepdims=True))
            scale = lax.rsqrt(ss / D_QK + EPS)
            qh = jnp.concatenate([nope * scale * gain[:D_NOPE], r1 * scale * gain[D_NOPE:D_NOPE + half],
                                  r2 * scale * gain[D_NOPE + half:D_QK],
                                  jnp.zeros((LANES - D_QK, rows), _F32)], axis=0)
            q_ref[0, hd] = qh.astype(q_ref.dtype)
    else:
        qraw = _dot(qn, wuq_ref[...])
        vfull = _dot(ckv_b, wuv_ref[...])
        lane = lax.broadcasted_iota(jnp.int32, (1, LANES), 1)
        first_half = lane < D_NOPE + D_ROPE // 2
        for hd in range(N_HEADS):
            sl = slice(hd * LANES, (hd + 1) * LANES)
            qh = qraw[:, sl]
            partner = jnp.where(first_half, pltpu.roll(qh, LANES - D_ROPE // 2, 1), pltpu.roll(qh, D_ROPE // 2, 1))
            qh = _head_norm(qh * cosp + partner * sinp, qgain_ref[...])
            q_ref[:, hd] = qh.astype(q_ref.dtype).reshape(nb, tm, LANES)
            v_ref[:, hd] = vfull[:, sl].astype(v_ref.dtype).reshape(nb, tm, LANES)


def _const_spec(a):
    nd = a.ndim
    return pl.BlockSpec(a.shape, lambda *_: (0,) * nd, pipeline_mode=pl.Buffered(1))


def _skip_refs(fn, start, count):
    def wrapped(*refs, **kw):
        return fn(*refs[:start], *refs[start + count:], **kw)
    return wrapped


def _pre_call(x, hist, tabs, lw, consts, *, nb, hist_shared, truncated, feature_major=False, into=None):
    bsz, t_len, d_model = x.shape
    tm = t_len if nb > 1 else min(ROW_TILE, t_len)
    assert bsz % nb == 0 and t_len % tm == 0 and tm % HIST_ROWS == 0 and not (feature_major and nb > 1)
    grid = (bsz // nb, t_len // tm)
    fm = "_t" if feature_major else ""
    weights = (lw["gmix"], lw["w_in"], lw["gq"], lw["w_uq" + fm], lw["gkv"], lw["w_uk"], lw["w_uv" + fm],
               lw["qgain" + fm], lw["kgain"], lw["w_pool"], lw["pscale"], consts["place"], consts["swap"])
    hist_spec = (pl.BlockSpec((1, HIST_ROWS, POOL_WIDTH), lambda b, t: (0, 0, 0)) if hist_shared
                 else pl.BlockSpec((nb, HIST_ROWS, POOL_WIDTH), lambda b, t: (b, 0, 0)))
    in_specs = [pl.BlockSpec((nb, tm, d_model), lambda b, t: (b, t, 0)), hist_spec]
    in_specs += [pl.BlockSpec((tm, a.shape[1]), lambda b, t: (t, 0)) for a in tabs[:4]]
    in_specs += [pl.BlockSpec((a.shape[0], tm), lambda b, t: (0, t)) for a in tabs[4:]]
    in_specs += [_const_spec(w) for w in weights]
    head_spec = pl.BlockSpec((nb, N_HEADS, tm, LANES), lambda b, t: (b, 0, t, 0))
    head_shape = jax.ShapeDtypeStruct((bsz, N_HEADS, t_len, LANES), _BF16)
    if feature_major:
        q_spec = pl.BlockSpec((1, N_HEADS, LANES, tm), lambda b, t: (b, 0, 0, t))
        q_shape = jax.ShapeDtypeStruct((bsz, N_HEADS, LANES, t_len), _BF16)
        v_spec = pl.BlockSpec((1, N_HEADS // 2, 1, LANES, tm), lambda b, t: (b, 0, t, 0, 0))
        v_shape = jax.ShapeDtypeStruct((bsz, N_HEADS // 2, t_len // tm, LANES, tm), _BF16)
    else:
        q_spec, q_shape, v_spec, v_shape = head_spec, head_shape, head_spec, head_shape

    def row_out(width, dtype):
        return (pl.BlockSpec((nb, tm, width), lambda b, t: (b, t, 0)),
                jax.ShapeDtypeStruct((bsz, t_len, width), dtype))

    body = functools.partial(_pre_kernel, nb=nb, tm=tm, hist_shared=hist_shared, truncated=truncated,
                             feature_major=feature_major)
    operands = [x, hist, *tabs, *weights]
    aliases = {}
    if into is None:
        lat_spec, lat_shape = row_out(KV_RANK, _F32)
        kr_spec, kr_shape = row_out(D_ROPE, _F32)
    else:
        layer, depth, bufs = into
        assert nb == 1

        def seq_out(width):
            block = tuple(pl.Element(n) for n in (1, 1, tm, width))
            return (pl.BlockSpec(block, lambda b, t: (layer, b, (1 + t * (tm // N_META)) * N_META, 0)),
                    jax.ShapeDtypeStruct((depth, bsz, N_META + t_len, width), _F32))

        lat_spec, lat_shape = seq_out(KV_RANK)
        kr_spec, kr_shape = seq_out(D_ROPE)
        if bufs is not None:
            body = _skip_refs(body, len(operands), len(bufs))
            aliases = {len(operands): 3, len(operands) + 1: 4}
            in_specs += [pl.BlockSpec(memory_space=pl.ANY)] * len(bufs)
            operands += list(bufs)
    pool_spec, pool_shape = row_out(POOL_WIDTH, _BF16)
    out_specs = [q_spec, head_spec, v_spec, lat_spec, kr_spec, pool_spec,
                 pl.BlockSpec((nb, HIST_ROWS, POOL_WIDTH), lambda b, t: (b, 0, 0))]
    out_shape = [q_shape, head_shape, v_shape, lat_shape, kr_shape, pool_shape,
                 jax.ShapeDtypeStruct((bsz, HIST_ROWS, POOL_WIDTH), _F32)]
    q, k, v, lat, krope, pool, ulast = pl.pallas_call(
        body, grid=grid, in_specs=in_specs, out_specs=out_specs, out_shape=out_shape,
        input_output_aliases=aliases,
        scratch_shapes=[pltpu.VMEM((nb, HIST_ROWS + tm, POOL_WIDTH), _F32)],
        compiler_params=pltpu.CompilerParams(dimension_semantics=("arbitrary", "arbitrary"),
                                             vmem_limit_bytes=VMEM_LIMIT_BYTES),
        name="pre_mixer",
    )(*operands)
    return dict(q=q, k=k, v=v, lat=lat, krope=krope, pool=pool, ulast=ulast)


def _meta_rows_kernel(lat_any, kr_any, mlat_ref, mkr_ref, lat_ref, kr_ref):
    del lat_any, kr_any
    lat_ref[0, 0] = mlat_ref[0]
    kr_ref[0, 0] = mkr_ref[0]


def _meta_rows_call(lat_all, kr_all, meta_lat, meta_kr):
    depth, bsz = lat_all.shape[:2]

    def src(width):
        return pl.BlockSpec((1, N_META, width), lambda l, b: (l, 0, 0))

    def dst(width):
        return pl.BlockSpec((1, 1, N_META, width), lambda l, b: (l, b, 0, 0))

    return pl.pallas_call(
        _meta_rows_kernel, grid=(depth, bsz),
        in_specs=[pl.BlockSpec(memory_space=pl.ANY)] * 2 + [src(KV_RANK), src(D_ROPE)],
        out_specs=[dst(KV_RANK), dst(D_ROPE)],
        out_shape=[jax.ShapeDtypeStruct(lat_all.shape, lat_all.dtype), jax.ShapeDtypeStruct(kr_all.shape, kr_all.dtype)],
        input_output_aliases={0: 0, 1: 1},
        compiler_params=pltpu.CompilerParams(dimension_semantics=("arbitrary", "arbitrary")),
        name="meta_rows",
    )(lat_all, kr_all, meta_lat, meta_kr)


def _post_kernel(x_ref, pool_ref, attn_ref, wop_ref, woa_ref, gffn_ref, wg_ref, wu_ref, wd_ref, o_ref):
    x1 = x_ref[...] + _dot(pool_ref[...], wop_ref[...]) + _dot(attn_ref[...], woa_ref[...])
    hn = _rms(x1, gffn_ref[...]).astype(_BF16)
    g = _dot(hn, wg_ref[...])
    up = _dot(hn, wu_ref[...])
    act = (g * jax.nn.sigmoid(g) * up).astype(_BF16)
    o_ref[...] = x1 + _dot(act, wd_ref[...])


def _post_call(x, pool, attn, lw):
    shape = x.shape
    d_model = shape[-1]
    x2 = x.reshape(-1, d_model)
    n = x2.shape[0]
    tm = min(ROW_TILE, n)
    assert n % tm == 0
    weights = (lw["w_o_pool"], lw["w_o_attn"], lw["gffn"], lw["w_gate"], lw["w_up"], lw["w_down"])

    def rows(width):
        return pl.BlockSpec((tm, width), lambda i: (i, 0))

    out = pl.pallas_call(
        _post_kernel, grid=(n // tm,),
        in_specs=[rows(d_model), rows(POOL_WIDTH), rows(N_HEADS * D_V)] + [_const_spec(w) for w in weights],
        out_specs=rows(d_model), out_shape=jax.ShapeDtypeStruct((n, d_model), _F32),
        compiler_params=pltpu.CompilerParams(dimension_semantics=("arbitrary",),
                                             vmem_limit_bytes=VMEM_LIMIT_BYTES),
        name="post_ffn",
    )(x2, pool.reshape(n, POOL_WIDTH), attn.reshape(n, N_HEADS * D_V), *weights)
    return out.reshape(shape)


def _attn_kernel(kbound_ref, q_ref, k_ref, v_ref, km_ref, vm_ref, o_ref,
                 acc_ref, m_ref, l_ref, s_even_ref, s_odd_ref, *, tile, online):
    qi = pl.program_id(2)

    def scores(hh, k_rows):
        return _dot(k_rows, q_ref[0, hh])

    def consume(hh, s_t, v_cols, mask):
        if mask is not None:
            s_t = jnp.where(mask, s_t, MASKED)
        m_prev = m_ref[hh]
        m_next = jnp.maximum(m_prev, jnp.max(s_t, axis=0, keepdims=True)) if online else m_prev
        p_t = jnp.exp2(s_t - m_next)
        pv = _dot(v_cols, p_t.astype(_BF16))
        l_new = jnp.sum(p_t, axis=0, keepdims=True)
        if online:
            alpha = jnp.exp2(m_prev - m_next)
            acc_ref[hh] = alpha * acc_ref[hh] + pv
            l_ref[hh] = alpha * l_ref[hh] + l_new
            m_ref[hh] = m_next
        else:
            acc_ref[hh] += pv
            l_ref[hh] += l_new

    def key_tile(hh, j):
        return k_ref[0, hh, pl.ds(pl.multiple_of(j * tile, tile), tile), :]

    def value_tile(hh, j):
        return v_ref[0, hh // 2, j, (hh % 2) * D_V:(hh % 2 + 1) * D_V, :]

    heads = range(q_ref.shape[1])
    for hh in heads:
        acc_ref[hh] = jnp.zeros(acc_ref.shape[1:], _F32)
        l_ref[hh] = jnp.zeros((1, tile), _F32)
        if online:
            m_ref[hh] = jnp.full((1, tile), MASKED, _F32)
        else:
            qf = q_ref[0, hh].astype(_F32)
            m_ref[hh] = jnp.sqrt(jnp.sum(qf * qf, axis=0, keepdims=True)) * kbound_ref[...]
        consume(hh, scores(hh, km_ref[hh]), vm_ref[hh], None)
        s_even_ref[hh] = scores(hh, key_tile(hh, 0))

    def step(j, s_cur_ref, s_next_ref):
        for hh in heads:
            s_next_ref[hh] = scores(hh, key_tile(hh, j + 1))
            consume(hh, s_cur_ref[hh], value_tile(hh, j), None)

    def two_steps(jj, carry):
        step(2 * jj, s_even_ref, s_odd_ref)
        step(2 * jj + 1, s_odd_ref, s_even_ref)
        return carry

    lax.fori_loop(0, qi // 2, two_steps, 0)

    @pl.when(qi % 2 == 1)
    def _():
        step(qi - 1, s_even_ref, s_odd_ref)

    key_chunk = lax.broadcasted_iota(jnp.int32, (tile, tile), 0) // CHUNK
    query_chunk = lax.broadcasted_iota(jnp.int32, (tile, tile), 1) // CHUNK
    mask = query_chunk >= key_chunk
    for parity, s_fin_ref in enumerate((s_even_ref, s_odd_ref)):
        @pl.when(qi % 2 == parity)
        def _():
            for hh in heads:
                consume(hh, s_fin_ref[hh], value_tile(hh, qi), mask)

    out_t = jnp.concatenate([acc_ref[hh] / l_ref[hh] for hh in heads], axis=0)
    o_ref[0] = out_t.T.astype(o_ref.dtype)


def _prompt_attn_call(kbound, q, k, v, km, vm, *, online):
    bsz, n_heads, _, s_len = q.shape
    tile = v.shape[-1]
    n_tiles = s_len // tile
    hps = ATTN_HEADS_PER_STEP
    assert s_len % tile == 0 and tile % CHUNK == 0 and n_heads % hps == 0 and hps % 2 == 0 and 2 * D_V == LANES
    grid = (bsz, n_heads // hps, n_tiles)
    return pl.pallas_call(
        functools.partial(_attn_kernel, tile=tile, online=online), grid=grid,
        in_specs=[pl.BlockSpec((1, 1), lambda b, p, i: (0, 0)),
                  pl.BlockSpec((1, hps, LANES, tile), lambda b, p, i: (b, p, 0, i)),
                  pl.BlockSpec((1, hps, s_len, LANES), lambda b, p, i: (b, p, 0, 0)),
                  pl.BlockSpec((1, hps // 2, n_tiles, LANES, tile), lambda b, p, i: (b, p, 0, 0, 0)),
                  pl.BlockSpec((hps, N_META, LANES), lambda b, p, i: (p, 0, 0)),
                  pl.BlockSpec((hps, D_V, N_META), lambda b, p, i: (p, 0, 0))],
        out_specs=pl.BlockSpec((1, tile, hps * D_V), lambda b, p, i: (b, i, p)),
        out_shape=jax.ShapeDtypeStruct((bsz, s_len, n_heads * D_V), _BF16),
        scratch_shapes=[pltpu.VMEM((hps, D_V, tile), _F32), pltpu.VMEM((hps, 1, tile), _F32),
                        pltpu.VMEM((hps, 1, tile), _F32), pltpu.VMEM((hps, tile, tile), _F32),
                        pltpu.VMEM((hps, tile, tile), _F32)],
        compiler_params=pltpu.CompilerParams(dimension_semantics=("arbitrary",) * 3,
                                             vmem_limit_bytes=VMEM_LIMIT_BYTES),
        name="prompt_attn",
    )(kbound, q, k, v, km, vm)


def _meta_attn_kernel(q_ref, k_ref, v_ref, o_ref):
    for pair in range(N_HEADS // 2):
        out = None
        for hd in (2 * pair, 2 * pair + 1):
            s = _dot_nt(q_ref[hd], k_ref[hd])
            p = jnp.exp2(s - jnp.max(s, axis=1, keepdims=True))
            o = _dot(p.astype(_BF16), v_ref[hd]) / jnp.sum(p, axis=1, keepdims=True)
            out = o if out is None else out + o
        o_ref[:, pair * LANES:(pair + 1) * LANES] = out.astype(o_ref.dtype)


def _meta_attn_call(q, k, v):
    n_rows = q.shape[1]
    return pl.pallas_call(
        _meta_attn_kernel, out_shape=jax.ShapeDtypeStruct((n_rows, N_HEADS * D_V), _BF16),
        name="meta_attn",
    )(q, k, v)


def _row_sums_t(x):
    ones = jnp.ones((8, x.shape[1]), _BF16)
    hi = x.astype(_BF16)
    lo = (x - hi.astype(_F32)).astype(_BF16)
    return (_dot_nt(ones, hi) + _dot_nt(ones, lo))[0:1]


def _sample_attn_kernel(q_ref, kn_ref, vn_ref, lat_ref, kpe_ref, wukt_ref, wukq_ref, wuv_ref, kgain_ref, place_ref,
                        o_ref, *, past):
    t_len = q_ref.shape[2]
    kgain = kgain_ref[...]
    qg = [(q_ref[0, hd].astype(_F32) * kgain).astype(_BF16) for hd in range(N_HEADS)]
    q_all = jnp.concatenate(qg, axis=0)
    q_lat = jnp.concatenate([_dot(qg[hd], wukq_ref[hd]) for hd in range(N_HEADS)], axis=0).astype(_BF16)

    def cached_scores(lo, hi):
        lat = lat_ref[0, 0, lo:hi].astype(_BF16)
        kpe = kpe_ref[0, lo:hi]
        s = _dot_nt(q_lat, lat) + _dot_nt(q_all, _perm(kpe, place_ref[...]).astype(_BF16))
        ss_rope = _row_sums_t(kpe * kpe)
        rows = []
        for pair in range(N_HEADS // 2):
            k_t = _dot_nt(wukt_ref[pair * LANES:(pair + 1) * LANES], lat)
            for hh in range(2):
                hd = 2 * pair + hh
                k_h = k_t[hh * D_NOPE:(hh + 1) * D_NOPE]
                r = lax.rsqrt((jnp.sum(k_h * k_h, axis=0, keepdims=True) + ss_rope) / D_QK + EPS)
                rows.append(s[hd * t_len:(hd + 1) * t_len] * r)
        return jnp.concatenate(rows, axis=0), lat

    s_meta, lat_meta = cached_scores(0, N_META)
    s_past, lat_past = cached_scores(N_META, N_META + past)
    tq = lax.broadcasted_iota(jnp.int32, (t_len, t_len), 0)
    tk = lax.broadcasted_iota(jnp.int32, (t_len, t_len), 1)
    new_mask = (past + tq) // CHUNK >= (past + tk) // CHUNK
    s_new = jnp.concatenate([jnp.where(new_mask, _dot_nt(q_ref[0, hd], kn_ref[0, hd]), MASKED)
                             for hd in range(N_HEADS)], axis=0)
    m = jnp.maximum(jnp.maximum(jnp.max(s_meta, axis=1, keepdims=True), jnp.max(s_past, axis=1, keepdims=True)),
                    jnp.max(s_new, axis=1, keepdims=True))
    p_meta, p_past, p_new = jnp.exp2(s_meta - m), jnp.exp2(s_past - m), jnp.exp2(s_new - m)
    den = (jnp.sum(p_meta, axis=1, keepdims=True) + jnp.sum(p_past, axis=1, keepdims=True)
           + jnp.sum(p_new, axis=1, keepdims=True))
    p_lat = (_dot(p_meta.astype(_BF16), lat_meta) + _dot(p_past.astype(_BF16), lat_past)).astype(_BF16)
    p_new = p_new.astype(_BF16)
    for pair in range(N_HEADS // 2):
        out = None
        for hd in (2 * pair, 2 * pair + 1):
            rows = slice(hd * t_len, (hd + 1) * t_len)
            o = (_dot(p_lat[rows], wuv_ref[:, hd * LANES:(hd + 1) * LANES]) + _dot(p_new[rows], vn_ref[0, hd])) / den[rows]
            out = o if out is None else out + o
        o_ref[0, :, pair * LANES:(pair + 1) * LANES] = out.astype(o_ref.dtype)


def _sample_attn_call(layer, q, k_new, v_new, cache_lat, cache_kpe, lw, consts):
    bsz, n_heads, t_len, _ = q.shape
    l_cache = cache_lat.shape[2]
    past = l_cache - N_META
    assert past % LANES == 0 and t_len % HIST_ROWS == 0
    head_spec = pl.BlockSpec((1, n_heads, t_len, LANES), lambda b: (b, 0, 0, 0))
    weights = (lw["w_uk_t"], lw["w_uk_q"], lw["w_uv"], lw["kgain"], consts["place"])
    return pl.pallas_call(
        functools.partial(_sample_attn_kernel, past=past), grid=(bsz,),
        in_specs=[head_spec, head_spec, head_spec,
                  pl.BlockSpec((1, 1, l_cache, KV_RANK), lambda b: (layer, b, 0, 0)),
                  pl.BlockSpec((1, l_cache, D_ROPE), lambda b: (b, 0, 0))]
                 + [_const_spec(w) for w in weights],
        out_specs=pl.BlockSpec((1, t_len, n_heads * D_V), lambda b: (b, 0, 0)),
        out_shape=jax.ShapeDtypeStruct((bsz, t_len, n_heads * D_V), _BF16),
        compiler_params=pltpu.CompilerParams(dimension_semantics=("arbitrary",),
                                             vmem_limit_bytes=VMEM_LIMIT_BYTES),
        name="sample_attn",
    )(q, k_new, v_new, cache_lat, cache_kpe, *weights)


def _pack_heads(w, width, alternate=False):
    kdim = w.shape[0]
    w3 = w.reshape(kdim, N_HEADS, width)
    low = jnp.pad(w3, ((0, 0), (0, 0), (0, LANES - width)))
    if alternate:
        high = jnp.pad(w3, ((0, 0), (0, 0), (LANES - width, 0)))
        odd = (jnp.arange(N_HEADS) % 2 == 1)[None, :, None]
        low = jnp.where(odd, high, low)
    return low.reshape(kdim, N_HEADS * LANES)


def _layer_weights(l, norm_mix, w_in, q_a_norm, w_uq, kv_a_norm, w_uk, w_uv, q_norm, k_norm,
                   w_pool, pool_scale, w_o, norm_ffn, w_gate, w_up, w_down):
    d_in = w_in.shape[2]
    d_in_pad = -(-d_in // (2 * LANES)) * (2 * LANES)
    pad96 = lambda g: jnp.pad(g.astype(_F32), (0, LANES - D_QK))[None, :]
    q_scale = D_QK ** -0.5 * math.log2(math.e)
    return dict(
        gmix=norm_mix[l][None, :].astype(_F32),
        w_in=jnp.pad(w_in[l], ((0, 0), (0, d_in_pad - d_in))).astype(_BF16),
        gq=q_a_norm[l][None, :].astype(_F32),
        w_uq=_pack_heads(w_uq[l], D_QK).astype(_BF16),
        w_uq_t=_pack_heads(w_uq[l], D_QK).astype(_BF16).T,
        gkv=kv_a_norm[l][None, :].astype(_F32),
        w_uk=_pack_heads(w_uk[l], D_NOPE).astype(_BF16),
        w_uk_t=w_uk[l].astype(_BF16).T,
        w_uk_q=jnp.pad(w_uk[l].astype(_BF16).T.reshape(N_HEADS, D_NOPE, KV_RANK),
                       ((0, 0), (0, LANES - D_NOPE), (0, 0))),
        w_uv=_pack_heads(w_uv[l], D_V, alternate=True).astype(_BF16),
        w_uv_t=w_uv[l].astype(_BF16).T,
        qgain=pad96(q_norm[l]) * q_scale,
        qgain_t=(pad96(q_norm[l]) * q_scale).T,
        kgain=pad96(k_norm[l]),
        w_pool=w_pool[l].astype(_BF16),
        pscale=pool_scale[l][None, :].astype(_F32),
        w_o_pool=w_o[l][:POOL_WIDTH].astype(_BF16),
        w_o_attn=w_o[l][POOL_WIDTH:].astype(_BF16),
        gffn=norm_ffn[l][None, :].astype(_F32),
        w_gate=w_gate[l].astype(_BF16),
        w_up=w_up[l].astype(_BF16),
        w_down=w_down[l].astype(_BF16),
    )


def _rope_tables(pos0, t_len):
    half = D_ROPE // 2
    inv = ROPE_THETA ** (-jnp.arange(half, dtype=_F32) / half)
    ang = (pos0 + jnp.arange(t_len, dtype=jnp.int32)).astype(_F32)[:, None] * inv[None, :]
    cos, sin = jnp.cos(ang), jnp.sin(ang)
    cos32 = jnp.concatenate([cos, cos], axis=1)
    sin32 = jnp.concatenate([-sin, sin], axis=1)
    ones = jnp.ones((t_len, D_NOPE), _F32)
    zeros = jnp.zeros((t_len, D_NOPE), _F32)
    tail1 = jnp.ones((t_len, LANES - D_QK), _F32)
    tail0 = jnp.zeros((t_len, LANES - D_QK), _F32)
    cosp = jnp.concatenate([ones, cos32, tail1], axis=1)
    sinp = jnp.concatenate([zeros, sin32, tail0], axis=1)
    return (cosp, sinp, cos32, sin32, cos.T, sin.T)


def _perm_consts():
    half = D_ROPE // 2
    idx = jnp.arange(D_ROPE)
    place = jnp.zeros((D_ROPE, LANES), _F32).at[idx, D_NOPE + idx].set(1.0)
    swap = jnp.zeros((D_ROPE, D_ROPE), _F32).at[idx, (idx + half) % D_ROPE].set(1.0)
    return dict(place=place.astype(_BF16), swap=swap.astype(_BF16))


def kernel(x_prompt, x_sample, cache_latent, cache_krope, state_pool, meta_tokens, norm_mix, w_in, q_a_norm, w_uq, kv_a_norm, w_uk, w_uv, q_norm, k_norm, w_pool, pool_scale, w_o, norm_ffn, w_gate, w_up, w_down):
    depth = norm_mix.shape[0]
    bp, s_len, d_model = x_prompt.shape
    bs, t_len, _ = x_sample.shape
    past = cache_latent.shape[2] - N_META
    consts = _perm_consts()
    tabs_meta = _rope_tables(0, N_META)
    tabs_prompt = _rope_tables(N_META, s_len)
    tabs_sample = _rope_tables(N_META + past, t_len)
    nb_sample = math.gcd(bs, 8)

    xm = meta_tokens.astype(_F32)[None]
    xp = x_prompt
    xs = x_sample
    zero_hist = jnp.zeros((1, HIST_ROWS, POOL_WIDTH), _F32)
    seq_bufs = None
    meta_lat, meta_kr, pool_p, lat_s, kpe_s, pool_s = [], [], [], [], [], []
    for l in range(depth):
        lw = _layer_weights(l, norm_mix, w_in, q_a_norm, w_uq, kv_a_norm, w_uk, w_uv, q_norm, k_norm,
                            w_pool, pool_scale, w_o, norm_ffn, w_gate, w_up, w_down)
        pm = _pre_call(xm, zero_hist, tabs_meta, lw, consts, nb=1, hist_shared=True, truncated=True)
        pp = _pre_call(xp, pm["ulast"], tabs_prompt, lw, consts, nb=1, hist_shared=True, truncated=False,
                       feature_major=True, into=(l, depth, seq_bufs))
        seq_bufs = (pp["lat"], pp["krope"])
        hist_s = jnp.pad(state_pool[l].astype(_F32), ((0, 0), (HIST_ROWS - POOL_STATE, 0), (0, 0)))
        ps = _pre_call(xs, hist_s, tabs_sample, lw, consts, nb=nb_sample, hist_shared=False, truncated=False)

        vm_t = jnp.stack([pm["v"][0, hd, :, (hd % 2) * D_V:(hd % 2 + 1) * D_V].T for hd in range(N_HEADS)])
        kbound = (D_QK ** 0.5 * jnp.max(jnp.abs(lw["kgain"]))).reshape(1, 1)
        qbound = D_QK ** 0.5 * jnp.max(jnp.abs(lw["qgain"]))
        attn_args = (kbound, pp["q"], pp["k"], pp["v"], pm["k"][0], vm_t)
        attn_p = lax.cond(2.0 * qbound * kbound[0, 0] <= EXP2_SAFE_RANGE,
                          functools.partial(_prompt_attn_call, online=False),
                          functools.partial(_prompt_attn_call, online=True), *attn_args)
        attn_s = _sample_attn_call(l, ps["q"], ps["k"], ps["v"], cache_latent, cache_krope[l], lw, consts)
        xp = _post_call(xp, pp["pool"], attn_p, lw)
        xs = _post_call(xs, ps["pool"], attn_s, lw)
        if l + 1 < depth:
            attn_m = _meta_attn_call(pm["q"][0], pm["k"][0], pm["v"][0])
            xm = _post_call(xm, pm["pool"], attn_m[None], lw)

        meta_lat.append(pm["lat"][0])
        meta_kr.append(pm["krope"][0])
        pool_p.append(pp["ulast"][:, HIST_ROWS - POOL_STATE:])
        lat_s.append(ps["lat"])
        kpe_s.append(ps["krope"])
        pool_s.append(ps["ulast"][:, HIST_ROWS - POOL_STATE:])
    lat_p, kpe_p = _meta_rows_call(*seq_bufs, jnp.stack(meta_lat), jnp.stack(meta_kr))
    return (xp, xs, lat_p, kpe_p, jnp.stack(pool_p), jnp.stack(lat_s), jnp.stack(kpe_s), jnp.stack(pool_s))
```

```python
import functools
import math

import jax
import jax.numpy as jnp
from jax import lax
from jax.experimental import pallas as pl
from jax.experimental.pallas import tpu as pltpu

CHUNK = 64
N_META = 16
N_HEADS = 8
D_NOPE = 64
D_ROPE = 32
D_QK = D_NOPE + D_ROPE
D_V = 64
Q_RANK = 384
KV_RANK = 256
POOL_WINDOWS = (2, 4, 8, 16)
POOL_GROUP = 128
POOL_WIDTH = POOL_GROUP * len(POOL_WINDOWS)
POOL_STATE = max(POOL_WINDOWS) - 1
ROPE_THETA = 10000.0
EPS = 1e-6

LANES = 128
HIST_ROWS = 16
VMEM_LIMIT_BYTES = 56 * 1024 * 1024
ROW_TILE = 512
ATTN_HEADS_PER_STEP = 4
MASKED = -1e30
EXP2_SAFE_RANGE = 120.0

_F32 = jnp.float32
_BF16 = jnp.bfloat16
_NT = (((1,), (1,)), ((), ()))


def _dot(a, b):
    return jnp.dot(a, b, preferred_element_type=_F32)


def _dot_nt(a, b):
    return lax.dot_general(a, b, _NT, preferred_element_type=_F32)


def _rms(x, g):
    return x * lax.rsqrt(jnp.mean(x * x, axis=-1, keepdims=True) + EPS) * g


def _perm(x, p):
    hi = x.astype(_BF16)
    lo = (x - hi.astype(_F32)).astype(_BF16)
    return _dot(hi, p) + _dot(lo, p)


def _head_norm(x, gain):
    ss = jnp.sum(x * x, axis=-1, keepdims=True)
    return x * lax.rsqrt(ss / D_QK + EPS) * gain


def _pre_kernel(x_ref, hist_ref, cosp_ref, sinp_ref, cos32_ref, sin32_ref, cost_ref, sint_ref,
                gmix_ref, win_ref, gq_ref, wuq_ref, gkv_ref, wuk_ref, wuv_ref, qgain_ref, kgain_ref,
                wpool_ref, pscale_ref, place_ref, swap_ref,
                q_ref, k_ref, v_ref, lat_ref, krope_ref, pool_ref, ulast_ref,
                ext_ref, *, nb, tm, hist_shared, truncated, feature_major):
    t = pl.program_id(1)
    d_model = x_ref.shape[-1]
    rows = nb * tm
    x = x_ref[...].reshape(rows, d_model)
    h = _rms(x, gmix_ref[...]).astype(_BF16)
    z = _dot(h, win_ref[...])
    u = z[:, :POOL_WIDTH]
    q_lat = z[:, POOL_WIDTH:POOL_WIDTH + Q_RANK]
    kv_lat = z[:, POOL_WIDTH + Q_RANK:POOL_WIDTH + Q_RANK + KV_RANK]
    kpe = z[:, POOL_WIDTH + Q_RANK + KV_RANK:POOL_WIDTH + Q_RANK + KV_RANK + D_ROPE]

    for b in range(nb):
        @pl.when(t == 0)
        def _():
            ext_ref[b, 0:HIST_ROWS, :] = hist_ref[0 if hist_shared else b]

        @pl.when(t > 0)
        def _():
            ext_ref[b, 0:HIST_ROWS, :] = ext_ref[b, tm:tm + HIST_ROWS, :]

        ext_ref[b, HIST_ROWS:HIST_ROWS + tm, :] = u[b * tm:(b + 1) * tm]
    for b in range(nb):
        cols = []
        for g, w in enumerate(POOL_WINDOWS):
            sl = slice(g * POOL_GROUP, (g + 1) * POOL_GROUP)
            cur = ext_ref[b, HIST_ROWS:HIST_ROWS + tm, sl]
            acc = cur
            for j in range(1, w):
                acc = acc + ext_ref[b, HIST_ROWS - j:HIST_ROWS - j + tm, sl]
            if truncated:
                row = t * tm + lax.broadcasted_iota(jnp.int32, (tm, 1), 0)
                mean = acc / jnp.minimum(row + 1, w).astype(_F32)
            else:
                mean = acc / float(w)
            cols.append(_dot((mean - cur).astype(_BF16), wpool_ref[g]))
        pool = jnp.concatenate(cols, axis=1) * pscale_ref[...]
        pool_ref[b] = pool.astype(pool_ref.dtype)
        ulast_ref[b] = ext_ref[b, tm:tm + HIST_ROWS, :]

    qn = _rms(q_lat, gq_ref[...]).astype(_BF16)
    c_kv = _rms(kv_lat, gkv_ref[...])
    lat_ref[...] = c_kv.reshape(lat_ref.shape)
    ckv_b = c_kv.astype(_BF16)
    knope = _dot(ckv_b, wuk_ref[...])

    def per_row(ref):
        tab = ref[...]
        return tab if nb == 1 else jnp.concatenate([tab] * nb, axis=0)

    cosp, sinp = per_row(cosp_ref), per_row(sinp_ref)
    krope = kpe * per_row(cos32_ref) + _perm(kpe, swap_ref[...]) * per_row(sin32_ref)
    krope_ref[...] = krope.reshape(krope_ref.shape)
    kpe_placed = _perm(krope, place_ref[...])

    for hd in range(N_HEADS):
        kh = _head_norm(knope[:, hd * LANES:(hd + 1) * LANES] + kpe_placed, kgain_ref[...])
        k_ref[:, hd] = kh.astype(k_ref.dtype).reshape(nb, tm, LANES)

    if feature_major:
        half = D_ROPE // 2
        qraw_t = _dot_nt(wuq_ref[...], qn)
        v_t = _dot_nt(wuv_ref[...], ckv_b)
        for pair in range(N_HEADS // 2):
            v_ref[0, pair, 0] = v_t[pair * LANES:(pair + 1) * LANES].astype(v_ref.dtype)
        cos_t, sin_t, gain = cost_ref[...], sint_ref[...], qgain_ref[...]
        for hd in range(N_HEADS):
            base = hd * LANES
            nope = qraw_t[base:base + D_NOPE]
            x1 = qraw_t[base + D_NOPE:base + D_NOPE + half]
            x2 = qraw_t[base + D_NOPE + half:base + D_QK]
            r1 = x1 * cos_t - x2 * sin_t
            r2 = x1 * sin_t + x2 * cos_t
            ss = (jnp.sum(nope * nope, axis=0, keepdims=True) + jnp.sum(r1 * r1, axis=0, keepdims=True)
                  + jnp.sum(r2 * r2, axis=0, keepdims=True))
            scale = lax.rsqrt(ss / D_QK + EPS)
            qh = jnp.concatenate([nope * scale * gain[:D_NOPE], r1 * scale * gain[D_NOPE:D_NOPE + half],
                                  r2 * scale * gain[D_NOPE + half:D_QK],
                                  jnp.zeros((LANES - D_QK, rows), _F32)], axis=0)
            q_ref[0, hd] = qh.astype(q_ref.dtype)
    else:
        qraw = _dot(qn, wuq_ref[...])
        vfull = _dot(ckv_b, wuv_ref[...])
        lane = lax.broadcasted_iota(jnp.int32, (1, LANES), 1)
        first_half = lane < D_NOPE + D_ROPE // 2
        for hd in range(N_HEADS):
            sl = slice(hd * LANES, (hd + 1) * LANES)
            qh = qraw[:, sl]
            partner = jnp.where(first_half, pltpu.roll(qh, LANES - D_ROPE // 2, 1), pltpu.roll(qh, D_ROPE // 2, 1))
            qh = _head_norm(qh * cosp + partner * sinp, qgain_ref[...])
            q_ref[:, hd] = qh.astype(q_ref.dtype).reshape(nb, tm, LANES)
            v_ref[:, hd] = vfull[:, sl].astype(v_ref.dtype).reshape(nb, tm, LANES)


def _const_spec(a):
    nd = a.ndim
    return pl.BlockSpec(a.shape, lambda *_: (0,) * nd, pipeline_mode=pl.Buffered(1))


def _skip_refs(fn, start, count):
    def wrapped(*refs, **kw):
        return fn(*refs[:start], *refs[start + count:], **kw)
    return wrapped


def _pre_call(x, hist, tabs, lw, consts, *, nb, hist_shared, truncated, feature_major=False, into=None):
    bsz, t_len, d_model = x.shape
    tm = t_len if nb > 1 else min(ROW_TILE, t_len)
    assert bsz % nb == 0 and t_len % tm == 0 and tm % HIST_ROWS == 0 and not (feature_major and nb > 1)
    grid = (bsz // nb, t_len // tm)
    fm = "_t" if feature_major else ""
    weights = (lw["gmix"], lw["w_in"], lw["gq"], lw["w_uq" + fm], lw["gkv"], lw["w_uk"], lw["w_uv" + fm],
               lw["qgain" + fm], lw["kgain"], lw["w_pool"], lw["pscale"], consts["place"], consts["swap"])
    hist_spec = (pl.BlockSpec((1, HIST_ROWS, POOL_WIDTH), lambda b, t: (0, 0, 0)) if hist_shared
                 else pl.BlockSpec((nb, HIST_ROWS, POOL_WIDTH), lambda b, t: (b, 0, 0)))
    in_specs = [pl.BlockSpec((nb, tm, d_model), lambda b, t: (b, t, 0)), hist_spec]
    in_specs += [pl.BlockSpec((tm, a.shape[1]), lambda b, t: (t, 0)) for a in tabs[:4]]
    in_specs += [pl.BlockSpec((a.shape[0], tm), lambda b, t: (0, t)) for a in tabs[4:]]
    in_specs += [_const_spec(w) for w in weights]
    head_spec = pl.BlockSpec((nb, N_HEADS, tm, LANES), lambda b, t: (b, 0, t, 0))
    head_shape = jax.ShapeDtypeStruct((bsz, N_HEADS, t_len, LANES), _BF16)
    if feature_major:
        q_spec = pl.BlockSpec((1, N_HEADS, LANES, tm), lambda b, t: (b, 0, 0, t))
        q_shape = jax.ShapeDtypeStruct((bsz, N_HEADS, LANES, t_len), _BF16)
        v_spec = pl.BlockSpec((1, N_HEADS // 2, 1, LANES, tm), lambda b, t: (b, 0, t, 0, 0))
        v_shape = jax.ShapeDtypeStruct((bsz, N_HEADS // 2, t_len // tm, LANES, tm), _BF16)
    else:
        q_spec, q_shape, v_spec, v_shape = head_spec, head_shape, head_spec, head_shape

    def row_out(width, dtype):
        return (pl.BlockSpec((nb, tm, width), lambda b, t: (b, t, 0)),
                jax.ShapeDtypeStruct((bsz, t_len, width), dtype))

    body = functools.partial(_pre_kernel, nb=nb, tm=tm, hist_shared=hist_shared, truncated=truncated,
                             feature_major=feature_major)
    operands = [x, hist, *tabs, *weights]
    aliases = {}
    if into is None:
        lat_spec, lat_shape = row_out(KV_RANK, _F32)
        kr_spec, kr_shape = row_out(D_ROPE, _F32)
    else:
        layer, depth, bufs = into
        assert nb == 1

        def seq_out(width):
            block = tuple(pl.Element(n) for n in (1, 1, tm, width))
            return (pl.BlockSpec(block, lambda b, t: (layer, b, (1 + t * (tm // N_META)) * N_META, 0)),
                    jax.ShapeDtypeStruct((depth, bsz, N_META + t_len, width), _F32))

        lat_spec, lat_shape = seq_out(KV_RANK)
        kr_spec, kr_shape = seq_out(D_ROPE)
        if bufs is not None:
            body = _skip_refs(body, len(operands), len(bufs))
            aliases = {len(operands): 3, len(operands) + 1: 4}
            in_specs += [pl.BlockSpec(memory_space=pl.ANY)] * len(bufs)
            operands += list(bufs)
    pool_spec, pool_shape = row_out(POOL_WIDTH, _BF16)
    out_specs = [q_spec, head_spec, v_spec, lat_spec, kr_spec, pool_spec,
                 pl.BlockSpec((nb, HIST_ROWS, POOL_WIDTH), lambda b, t: (b, 0, 0))]
    out_shape = [q_shape, head_shape, v_shape, lat_shape, kr_shape, pool_shape,
                 jax.ShapeDtypeStruct((bsz, HIST_ROWS, POOL_WIDTH), _F32)]
    q, k, v, lat, krope, pool, ulast = pl.pallas_call(
        body, grid=grid, in_specs=in_specs, out_specs=out_specs, out_shape=out_shape,
        input_output_aliases=aliases,
        scratch_shapes=[pltpu.VMEM((nb, HIST_ROWS + tm, POOL_WIDTH), _F32)],
        compiler_params=pltpu.CompilerParams(dimension_semantics=("arbitrary", "arbitrary"),
                                             vmem_limit_bytes=VMEM_LIMIT_BYTES),
        name="pre_mixer",
    )(*operands)
    return dict(q=q, k=k, v=v, lat=lat, krope=krope, pool=pool, ulast=ulast)


def _meta_rows_kernel(lat_any, kr_any, mlat_ref, mkr_ref, lat_ref, kr_ref):
    del lat_any, kr_any
    lat_ref[0, 0] = mlat_ref[0]
    kr_ref[0, 0] = mkr_ref[0]


def _meta_rows_call(lat_all, kr_all, meta_lat, meta_kr):
    depth, bsz = lat_all.shape[:2]

    def src(width):
        return pl.BlockSpec((1, N_META, width), lambda l, b: (l, 0, 0))

    def dst(width):
        return pl.BlockSpec((1, 1, N_META, width), lambda l, b: (l, b, 0, 0))

    return pl.pallas_call(
        _meta_rows_kernel, grid=(depth, bsz),
        in_specs=[pl.BlockSpec(memory_space=pl.ANY)] * 2 + [src(KV_RANK), src(D_ROPE)],
        out_specs=[dst(KV_RANK), dst(D_ROPE)],
        out_shape=[jax.ShapeDtypeStruct(lat_all.shape, lat_all.dtype), jax.ShapeDtypeStruct(kr_all.shape, kr_all.dtype)],
        input_output_aliases={0: 0, 1: 1},
        compiler_params=pltpu.CompilerParams(dimension_semantics=("arbitrary", "arbitrary")),
        name="meta_rows",
    )(lat_all, kr_all, meta_lat, meta_kr)


def _post_kernel(x_ref, pool_ref, attn_ref, wop_ref, woa_ref, gffn_ref, wg_ref, wu_ref, wd_ref, o_ref):
    x1 = x_ref[...] + _dot(pool_ref[...], wop_ref[...]) + _dot(attn_ref[...], woa_ref[...])
    hn = _rms(x1, gffn_ref[...]).astype(_BF16)
    g = _dot(hn, wg_ref[...])
    up = _dot(hn, wu_ref[...])
    act = (g * jax.nn.sigmoid(g) * up).astype(_BF16)
    o_ref[...] = x1 + _dot(act, wd_ref[...])


def _post_call(x, pool, attn, lw):
    shape = x.shape
    d_model = shape[-1]
    x2 = x.reshape(-1, d_model)
    n = x2.shape[0]
    tm = min(ROW_TILE, n)
    assert n % tm == 0
    weights = (lw["w_o_pool"], lw["w_o_attn"], lw["gffn"], lw["w_gate"], lw["w_up"], lw["w_down"])

    def rows(width):
        return pl.BlockSpec((tm, width), lambda i: (i, 0))

    out = pl.pallas_call(
        _post_kernel, grid=(n // tm,),
        in_specs=[rows(d_model), rows(POOL_WIDTH), rows(N_HEADS * D_V)] + [_const_spec(w) for w in weights],
        out_specs=rows(d_model), out_shape=jax.ShapeDtypeStruct((n, d_model), _F32),
        compiler_params=pltpu.CompilerParams(dimension_semantics=("arbitrary",),
                                             vmem_limit_bytes=VMEM_LIMIT_BYTES),
        name="post_ffn",
    )(x2, pool.reshape(n, POOL_WIDTH), attn.reshape(n, N_HEADS * D_V), *weights)
    return out.reshape(shape)


def _attn_kernel(kbound_ref, q_ref, q_next_ref, k_ref, v_ref, km_ref, vm_ref, o_ref,
                 acc_ref, m_ref, l_ref, s_meta_ref, s_first_ref, s_even_ref, s_odd_ref, *, tile, online):
    qi = pl.program_id(2)

    def scores(hh, k_rows, queries_ref=q_ref):
        return _dot(k_rows, queries_ref[0, hh])

    def consume(hh, s_t, v_cols, mask):
        if mask is not None:
            s_t = jnp.where(mask, s_t, MASKED)
        m_prev = m_ref[hh]
        m_next = jnp.maximum(m_prev, jnp.max(s_t, axis=0, keepdims=True)) if online else m_prev
        p_t = jnp.exp2(s_t - m_next)
        pv = _dot(v_cols, p_t.astype(_BF16))
        l_new = jnp.sum(p_t, axis=0, keepdims=True)
        if online:
            alpha = jnp.exp2(m_prev - m_next)
            acc_ref[hh] = alpha * acc_ref[hh] + pv
            l_ref[hh] = alpha * l_ref[hh] + l_new
            m_ref[hh] = m_next
        else:
            acc_ref[hh] += pv
            l_ref[hh] += l_new

    def key_tile(hh, j):
        return k_ref[0, hh, pl.ds(pl.multiple_of(j * tile, tile), tile), :]

    def value_tile(hh, j):
        return v_ref[0, hh // 2, j, (hh % 2) * D_V:(hh % 2 + 1) * D_V, :]

    heads = range(q_ref.shape[1])
    for hh in heads:
        acc_ref[hh] = jnp.zeros(acc_ref.shape[1:], _F32)
        l_ref[hh] = jnp.zeros((1, tile), _F32)
        if online:
            m_ref[hh] = jnp.full((1, tile), MASKED, _F32)
        else:
            qf = q_ref[0, hh].astype(_F32)
            m_ref[hh] = jnp.sqrt(jnp.sum(qf * qf, axis=0, keepdims=True)) * kbound_ref[...]
        s_meta_ref[hh] = scores(hh, km_ref[hh])

    @pl.when(qi == 0)
    def _():
        for hh in heads:
            s_even_ref[hh] = scores(hh, key_tile(hh, 0))

    def step(j, s_cur_ref, s_next_ref):
        for hh in heads:
            s_next_ref[hh] = scores(hh, key_tile(hh, j + 1))
            consume(hh, s_cur_ref[hh], value_tile(hh, j), None)

    @pl.when(qi >= 1)
    def _():
        step(0, s_first_ref, s_odd_ref)

    def two_steps(jj, carry):
        step(2 * jj + 1, s_odd_ref, s_even_ref)
        step(2 * jj + 2, s_even_ref, s_odd_ref)
        return carry

    lax.fori_loop(0, jnp.maximum(qi - 1, 0) // 2, two_steps, 0)

    @pl.when(jnp.logical_and(qi >= 2, qi % 2 == 0))
    def _():
        step(qi - 1, s_odd_ref, s_even_ref)

    key_chunk = lax.broadcasted_iota(jnp.int32, (tile, tile), 0) // CHUNK
    query_chunk = lax.broadcasted_iota(jnp.int32, (tile, tile), 1) // CHUNK
    meta_mask = lax.broadcasted_iota(jnp.int32, (km_ref.shape[1], tile), 0) < N_META
    mask = jnp.concatenate([query_chunk >= key_chunk, meta_mask], axis=0)
    for parity, s_fin_ref in enumerate((s_even_ref, s_odd_ref)):
        @pl.when(qi % 2 == parity)
        def _():
            for hh in heads:
                s_first_ref[hh] = scores(hh, key_tile(hh, 0), q_next_ref)
                s_t = jnp.concatenate([s_fin_ref[hh], s_meta_ref[hh]], axis=0)
                v_cols = jnp.concatenate([value_tile(hh, qi), vm_ref[hh]], axis=1)
                consume(hh, s_t, v_cols, mask)

    out_t = jnp.concatenate([acc_ref[hh] / l_ref[hh] for hh in heads], axis=0)
    o_ref[0] = out_t.T.astype(o_ref.dtype)


def _prompt_attn_call(kbound, q, k, v, km, vm, *, online):
    bsz, n_heads, _, s_len = q.shape
    tile = v.shape[-1]
    n_tiles = s_len // tile
    hps = ATTN_HEADS_PER_STEP
    assert s_len % tile == 0 and tile % CHUNK == 0 and n_heads % hps == 0 and hps % 2 == 0 and 2 * D_V == LANES
    grid = (bsz, n_heads // hps, n_tiles)
    return pl.pallas_call(
        functools.partial(_attn_kernel, tile=tile, online=online), grid=grid,
        in_specs=[pl.BlockSpec((1, 1), lambda b, p, i: (0, 0)),
                  pl.BlockSpec((1, hps, LANES, tile), lambda b, p, i: (b, p, 0, i)),
                  pl.BlockSpec((1, hps, LANES, tile), lambda b, p, i: (b, p, 0, jnp.minimum(i + 1, n_tiles - 1))),
                  pl.BlockSpec((1, hps, s_len, LANES), lambda b, p, i: (b, p, 0, 0)),
                  pl.BlockSpec((1, hps // 2, n_tiles, LANES, tile), lambda b, p, i: (b, p, 0, 0, 0)),
                  pl.BlockSpec((hps, LANES, LANES), lambda b, p, i: (p, 0, 0)),
                  pl.BlockSpec((hps, D_V, LANES), lambda b, p, i: (p, 0, 0))],
        out_specs=pl.BlockSpec((1, tile, hps * D_V), lambda b, p, i: (b, i, p)),
        out_shape=jax.ShapeDtypeStruct((bsz, s_len, n_heads * D_V), _BF16),
        scratch_shapes=[pltpu.VMEM((hps, D_V, tile), _F32), pltpu.VMEM((hps, 1, tile), _F32),
                        pltpu.VMEM((hps, 1, tile), _F32), pltpu.VMEM((hps, LANES, tile), _F32)]
                       + [pltpu.VMEM((hps, tile, tile), _F32)] * 3,
        compiler_params=pltpu.CompilerParams(dimension_semantics=("arbitrary",) * 3,
                                             vmem_limit_bytes=VMEM_LIMIT_BYTES),
        name="prompt_attn",
    )(kbound, q, q, k, v, km, vm)


def _meta_attn_kernel(q_ref, k_ref, v_ref, o_ref):
    for pair in range(N_HEADS // 2):
        out = None
        for hd in (2 * pair, 2 * pair + 1):
            s = _dot_nt(q_ref[hd], k_ref[hd])
            p = jnp.exp2(s - jnp.max(s, axis=1, keepdims=True))
            o = _dot(p.astype(_BF16), v_ref[hd]) / jnp.sum(p, axis=1, keepdims=True)
            out = o if out is None else out + o
        o_ref[:, pair * LANES:(pair + 1) * LANES] = out.astype(o_ref.dtype)


def _meta_attn_call(q, k, v):
    n_rows = q.shape[1]
    return pl.pallas_call(
        _meta_attn_kernel, out_shape=jax.ShapeDtypeStruct((n_rows, N_HEADS * D_V), _BF16),
        name="meta_attn",
    )(q, k, v)


def _row_sums_t(x):
    ones = jnp.ones((8, x.shape[1]), _BF16)
    hi = x.astype(_BF16)
    lo = (x - hi.astype(_F32)).astype(_BF16)
    return (_dot_nt(ones, hi) + _dot_nt(ones, lo))[0:1]


def _sample_attn_kernel(q_ref, kn_ref, vn_ref, lat_ref, kpe_ref, wukt_ref, wukq_ref, wuv_ref, kgain_ref, place_ref,
                        o_ref, *, past):
    t_len = q_ref.shape[2]
    kgain = kgain_ref[...]
    qg = [(q_ref[0, hd].astype(_F32) * kgain).astype(_BF16) for hd in range(N_HEADS)]
    q_all = jnp.concatenate(qg, axis=0)
    q_lat = jnp.concatenate([_dot(qg[hd], wukq_ref[hd]) for hd in range(N_HEADS)], axis=0).astype(_BF16)

    def cached_scores(lo, hi):
        lat = lat_ref[0, 0, lo:hi].astype(_BF16)
        kpe = kpe_ref[0, lo:hi]
        s = _dot_nt(q_lat, lat) + _dot_nt(q_all, _perm(kpe, place_ref[...]).astype(_BF16))
        ss_rope = _row_sums_t(kpe * kpe)
        rows = []
        for pair in range(N_HEADS // 2):
            k_t = _dot_nt(wukt_ref[pair * LANES:(pair + 1) * LANES], lat)
            for hh in range(2):
                hd = 2 * pair + hh
                k_h = k_t[hh * D_NOPE:(hh + 1) * D_NOPE]
                r = lax.rsqrt((jnp.sum(k_h * k_h, axis=0, keepdims=True) + ss_rope) / D_QK + EPS)
                rows.append(s[hd * t_len:(hd + 1) * t_len] * r)
        return jnp.concatenate(rows, axis=0), lat

    s_meta, lat_meta = cached_scores(0, N_META)
    s_past, lat_past = cached_scores(N_META, N_META + past)
    tq = lax.broadcasted_iota(jnp.int32, (t_len, t_len), 0)
    tk = lax.broadcasted_iota(jnp.int32, (t_len, t_len), 1)
    new_mask = (past + tq) // CHUNK >= (past + tk) // CHUNK
    s_new = jnp.concatenate([jnp.where(new_mask, _dot_nt(q_ref[0, hd], kn_ref[0, hd]), MASKED)
                             for hd in range(N_HEADS)], axis=0)
    m = jnp.maximum(jnp.maximum(jnp.max(s_meta, axis=1, keepdims=True), jnp.max(s_past, axis=1, keepdims=True)),
                    jnp.max(s_new, axis=1, keepdims=True))
    p_meta, p_past, p_new = jnp.exp2(s_meta - m), jnp.exp2(s_past - m), jnp.exp2(s_new - m)
    den = (jnp.sum(p_meta, axis=1, keepdims=True) + jnp.sum(p_past, axis=1, keepdims=True)
           + jnp.sum(p_new, axis=1, keepdims=True))
    p_lat = (_dot(p_meta.astype(_BF16), lat_meta) + _dot(p_past.astype(_BF16), lat_past)).astype(_BF16)
    p_new = p_new.astype(_BF16)
    for pair in range(N_HEADS // 2):
        out = None
        for hd in (2 * pair, 2 * pair + 1):
            rows = slice(hd * t_len, (hd + 1) * t_len)
            o = (_dot(p_lat[rows], wuv_ref[:, hd * LANES:(hd + 1) * LANES]) + _dot(p_new[rows], vn_ref[0, hd])) / den[rows]
            out = o if out is None else out + o
        o_ref[0, :, pair * LANES:(pair + 1) * LANES] = out.astype(o_ref.dtype)


def _sample_attn_call(layer, q, k_new, v_new, cache_lat, cache_kpe, lw, consts):
    bsz, n_heads, t_len, _ = q.shape
    l_cache = cache_lat.shape[2]
    past = l_cache - N_META
    assert past % LANES == 0 and t_len % HIST_ROWS == 0
    head_spec = pl.BlockSpec((1, n_heads, t_len, LANES), lambda b: (b, 0, 0, 0))
    weights = (lw["w_uk_t"], lw["w_uk_q"], lw["w_uv"], lw["kgain"], consts["place"])
    return pl.pallas_call(
        functools.partial(_sample_attn_kernel, past=past), grid=(bsz,),
        in_specs=[head_spec, head_spec, head_spec,
                  pl.BlockSpec((1, 1, l_cache, KV_RANK), lambda b: (layer, b, 0, 0)),
                  pl.BlockSpec((1, l_cache, D_ROPE), lambda b: (b, 0, 0))]
                 + [_const_spec(w) for w in weights],
        out_specs=pl.BlockSpec((1, t_len, n_heads * D_V), lambda b: (b, 0, 0)),
        out_shape=jax.ShapeDtypeStruct((bsz, t_len, n_heads * D_V), _BF16),
        compiler_params=pltpu.CompilerParams(dimension_semantics=("arbitrary",),
                                             vmem_limit_bytes=VMEM_LIMIT_BYTES),
        name="sample_attn",
    )(q, k_new, v_new, cache_lat, cache_kpe, *weights)


def _pack_heads(w, width, alternate=False):
    kdim = w.shape[0]
    w3 = w.reshape(kdim, N_HEADS, width)
    low = jnp.pad(w3, ((0, 0), (0, 0), (0, LANES - width)))
    if alternate:
        high = jnp.pad(w3, ((0, 0), (0, 0), (LANES - width, 0)))
        odd = (jnp.arange(N_HEADS) % 2 == 1)[None, :, None]
        low = jnp.where(odd, high, low)
    return low.reshape(kdim, N_HEADS * LANES)


def _layer_weights(l, norm_mix, w_in, q_a_norm, w_uq, kv_a_norm, w_uk, w_uv, q_norm, k_norm,
                   w_pool, pool_scale, w_o, norm_ffn, w_gate, w_up, w_down):
    d_in = w_in.shape[2]
    d_in_pad = -(-d_in // (2 * LANES)) * (2 * LANES)
    pad96 = lambda g: jnp.pad(g.astype(_F32), (0, LANES - D_QK))[None, :]
    q_scale = D_QK ** -0.5 * math.log2(math.e)
    return dict(
        gmix=norm_mix[l][None, :].astype(_F32),
        w_in=jnp.pad(w_in[l], ((0, 0), (0, d_in_pad - d_in))).astype(_BF16),
        gq=q_a_norm[l][None, :].astype(_F32),
        w_uq=_pack_heads(w_uq[l], D_QK).astype(_BF16),
        w_uq_t=_pack_heads(w_uq[l], D_QK).astype(_BF16).T,
        gkv=kv_a_norm[l][None, :].astype(_F32),
        w_uk=_pack_heads(w_uk[l], D_NOPE).astype(_BF16),
        w_uk_t=w_uk[l].astype(_BF16).T,
        w_uk_q=jnp.pad(w_uk[l].astype(_BF16).T.reshape(N_HEADS, D_NOPE, KV_RANK),
                       ((0, 0), (0, LANES - D_NOPE), (0, 0))),
        w_uv=_pack_heads(w_uv[l], D_V, alternate=True).astype(_BF16),
        w_uv_t=w_uv[l].astype(_BF16).T,
        qgain=pad96(q_norm[l]) * q_scale,
        qgain_t=(pad96(q_norm[l]) * q_scale).T,
        kgain=pad96(k_norm[l]),
        w_pool=w_pool[l].astype(_BF16),
        pscale=pool_scale[l][None, :].astype(_F32),
        w_o_pool=w_o[l][:POOL_WIDTH].astype(_BF16),
        w_o_attn=w_o[l][POOL_WIDTH:].astype(_BF16),
        gffn=norm_ffn[l][None, :].astype(_F32),
        w_gate=w_gate[l].astype(_BF16),
        w_up=w_up[l].astype(_BF16),
        w_down=w_down[l].astype(_BF16),
    )


def _rope_tables(pos0, t_len):
    half = D_ROPE // 2
    inv = ROPE_THETA ** (-jnp.arange(half, dtype=_F32) / half)
    ang = (pos0 + jnp.arange(t_len, dtype=jnp.int32)).astype(_F32)[:, None] * inv[None, :]
    cos, sin = jnp.cos(ang), jnp.sin(ang)
    cos32 = jnp.concatenate([cos, cos], axis=1)
    sin32 = jnp.concatenate([-sin, sin], axis=1)
    ones = jnp.ones((t_len, D_NOPE), _F32)
    zeros = jnp.zeros((t_len, D_NOPE), _F32)
    tail1 = jnp.ones((t_len, LANES - D_QK), _F32)
    tail0 = jnp.zeros((t_len, LANES - D_QK), _F32)
    cosp = jnp.concatenate([ones, cos32, tail1], axis=1)
    sinp = jnp.concatenate([zeros, sin32, tail0], axis=1)
    return (cosp, sinp, cos32, sin32, cos.T, sin.T)


def _perm_consts():
    half = D_ROPE // 2
    idx = jnp.arange(D_ROPE)
    place = jnp.zeros((D_ROPE, LANES), _F32).at[idx, D_NOPE + idx].set(1.0)
    swap = jnp.zeros((D_ROPE, D_ROPE), _F32).at[idx, (idx + half) % D_ROPE].set(1.0)
    return dict(place=place.astype(_BF16), swap=swap.astype(_BF16))


def kernel(x_prompt, x_sample, cache_latent, cache_krope, state_pool, meta_tokens, norm_mix, w_in, q_a_norm, w_uq, kv_a_norm, w_uk, w_uv, q_norm, k_norm, w_pool, pool_scale, w_o, norm_ffn, w_gate, w_up, w_down):
    depth = norm_mix.shape[0]
    bp, s_len, d_model = x_prompt.shape
    bs, t_len, _ = x_sample.shape
    past = cache_latent.shape[2] - N_META
    consts = _perm_consts()
    tabs_meta = _rope_tables(0, N_META)
    tabs_prompt = _rope_tables(N_META, s_len)
    tabs_sample = _rope_tables(N_META + past, t_len)
    nb_sample = math.gcd(bs, 8)

    xm = meta_tokens.astype(_F32)[None]
    xp = x_prompt
    xs = x_sample
    zero_hist = jnp.zeros((1, HIST_ROWS, POOL_WIDTH), _F32)
    seq_bufs = None
    meta_lat, meta_kr, pool_p, lat_s, kpe_s, pool_s = [], [], [], [], [], []
    for l in range(depth):
        lw = _layer_weights(l, norm_mix, w_in, q_a_norm, w_uq, kv_a_norm, w_uk, w_uv, q_norm, k_norm,
                            w_pool, pool_scale, w_o, norm_ffn, w_gate, w_up, w_down)
        pm = _pre_call(xm, zero_hist, tabs_meta, lw, consts, nb=1, hist_shared=True, truncated=True)
        pp = _pre_call(xp, pm["ulast"], tabs_prompt, lw, consts, nb=1, hist_shared=True, truncated=False,
                       feature_major=True, into=(l, depth, seq_bufs))
        seq_bufs = (pp["lat"], pp["krope"])
        hist_s = jnp.pad(state_pool[l].astype(_F32), ((0, 0), (HIST_ROWS - POOL_STATE, 0), (0, 0)))
        ps = _pre_call(xs, hist_s, tabs_sample, lw, consts, nb=nb_sample, hist_shared=False, truncated=False)

        km = jnp.pad(pm["k"][0], ((0, 0), (0, LANES - N_META), (0, 0)))
        vm_t = jnp.stack([pm["v"][0, hd, :, (hd % 2) * D_V:(hd % 2 + 1) * D_V].T for hd in range(N_HEADS)])
        vm_t = jnp.pad(vm_t, ((0, 0), (0, 0), (0, LANES - N_META)))
        kbound = (D_QK ** 0.5 * jnp.max(jnp.abs(lw["kgain"]))).reshape(1, 1)
        qbound = D_QK ** 0.5 * jnp.max(jnp.abs(lw["qgain"]))
        attn_args = (kbound, pp["q"], pp["k"], pp["v"], km, vm_t)
        attn_p = lax.cond(2.0 * qbound * kbound[0, 0] <= EXP2_SAFE_RANGE,
                          functools.partial(_prompt_attn_call, online=False),
                          functools.partial(_prompt_attn_call, online=True), *attn_args)
        attn_s = _sample_attn_call(l, ps["q"], ps["k"], ps["v"], cache_latent, cache_krope[l], lw, consts)
        xp = _post_call(xp, pp["pool"], attn_p, lw)
        xs = _post_call(xs, ps["pool"], attn_s, lw)
        if l + 1 < depth:
            attn_m = _meta_attn_call(pm["q"][0], pm["k"][0], pm["v"][0])
            xm = _post_call(xm, pm["pool"], attn_m[None], lw)

        meta_lat.append(pm["lat"][0])
        meta_kr.append(pm["krope"][0])
        pool_p.append(pp["ulast"][:, HIST_ROWS - POOL_STATE:])
        lat_s.append(ps["lat"])
        kpe_s.append(ps["krope"])
        pool_s.append(ps["ulast"][:, HIST_ROWS - POOL_STATE:])
    lat_p, kpe_p = _meta_rows_call(*seq_bufs, jnp.stack(meta_lat), jnp.stack(meta_kr))
    return (xp, xs, lat_p, kpe_p, jnp.stack(pool_p), jnp.stack(lat_s), jnp.stack(kpe_s), jnp.stack(pool_s))
```

```python
import functools
import math

import jax
import jax.numpy as jnp
from jax import lax
from jax.experimental import pallas as pl
from jax.experimental.pallas import tpu as pltpu

CHUNK = 64
N_META = 16
N_HEADS = 8
D_NOPE = 64
D_ROPE = 32
D_QK = D_NOPE + D_ROPE
D_V = 64
Q_RANK = 384
KV_RANK = 256
POOL_WINDOWS = (2, 4, 8, 16)
POOL_GROUP = 128
POOL_WIDTH = POOL_GROUP * len(POOL_WINDOWS)
POOL_STATE = max(POOL_WINDOWS) - 1
ROPE_THETA = 10000.0
EPS = 1e-6

LANES = 128
HIST_ROWS = 16
VMEM_LIMIT_BYTES = 56 * 1024 * 1024
ROW_TILE = 512
ATTN_HEADS_PER_STEP = 4
V_ROWS = D_V + 16
MASKED = -1e30
EXP2_SAFE_RANGE = 120.0

_F32 = jnp.float32
_BF16 = jnp.bfloat16
_NT = (((1,), (1,)), ((), ()))


def _dot(a, b):
    return jnp.dot(a, b, preferred_element_type=_F32)


def _dot_nt(a, b):
    return lax.dot_general(a, b, _NT, preferred_element_type=_F32)


def _rms(x, g):
    return x * lax.rsqrt(jnp.mean(x * x, axis=-1, keepdims=True) + EPS) * g


def _perm(x, p):
    hi = x.astype(_BF16)
    lo = (x - hi.astype(_F32)).astype(_BF16)
    return _dot(hi, p) + _dot(lo, p)


def _head_norm(x, gain):
    ss = jnp.sum(x * x, axis=-1, keepdims=True)
    return x * lax.rsqrt(ss / D_QK + EPS) * gain


def _pre_kernel(x_ref, hist_ref, cosp_ref, sinp_ref, cos32_ref, sin32_ref, cost_ref, sint_ref,
                gmix_ref, win_ref, gq_ref, wuq_ref, gkv_ref, wuk_ref, wuv_ref, qgain_ref, kgain_ref,
                wpool_ref, pscale_ref, place_ref, swap_ref,
                q_ref, k_ref, v_ref, lat_ref, krope_ref, pool_ref, ulast_ref,
                ext_ref, *, nb, tm, hist_shared, truncated, feature_major):
    t = pl.program_id(1)
    d_model = x_ref.shape[-1]
    rows = nb * tm
    x = x_ref[...].reshape(rows, d_model)
    h = _rms(x, gmix_ref[...]).astype(_BF16)
    z = _dot(h, win_ref[...])
    u = z[:, :POOL_WIDTH]
    q_lat = z[:, POOL_WIDTH:POOL_WIDTH + Q_RANK]
    kv_lat = z[:, POOL_WIDTH + Q_RANK:POOL_WIDTH + Q_RANK + KV_RANK]
    kpe = z[:, POOL_WIDTH + Q_RANK + KV_RANK:POOL_WIDTH + Q_RANK + KV_RANK + D_ROPE]

    for b in range(nb):
        @pl.when(t == 0)
        def _():
            ext_ref[b, 0:HIST_ROWS, :] = hist_ref[0 if hist_shared else b]

        @pl.when(t > 0)
        def _():
            ext_ref[b, 0:HIST_ROWS, :] = ext_ref[b, tm:tm + HIST_ROWS, :]

        ext_ref[b, HIST_ROWS:HIST_ROWS + tm, :] = u[b * tm:(b + 1) * tm]
    for b in range(nb):
        cols = []
        for g, w in enumerate(POOL_WINDOWS):
            sl = slice(g * POOL_GROUP, (g + 1) * POOL_GROUP)
            cur = ext_ref[b, HIST_ROWS:HIST_ROWS + tm, sl]
            acc = cur
            for j in range(1, w):
                acc = acc + ext_ref[b, HIST_ROWS - j:HIST_ROWS - j + tm, sl]
            if truncated:
                row = t * tm + lax.broadcasted_iota(jnp.int32, (tm, 1), 0)
                mean = acc / jnp.minimum(row + 1, w).astype(_F32)
            else:
                mean = acc / float(w)
            cols.append(_dot((mean - cur).astype(_BF16), wpool_ref[g]))
        pool = jnp.concatenate(cols, axis=1) * pscale_ref[...]
        pool_ref[b] = pool.astype(pool_ref.dtype)
        ulast_ref[b] = ext_ref[b, tm:tm + HIST_ROWS, :]

    qn = _rms(q_lat, gq_ref[...]).astype(_BF16)
    c_kv = _rms(kv_lat, gkv_ref[...])
    lat_ref[...] = c_kv.reshape(lat_ref.shape)
    ckv_b = c_kv.astype(_BF16)
    knope = _dot(ckv_b, wuk_ref[...])

    def per_row(ref):
        tab = ref[...]
        return tab if nb == 1 else jnp.concatenate([tab] * nb, axis=0)

    cosp, sinp = per_row(cosp_ref), per_row(sinp_ref)
    krope = kpe * per_row(cos32_ref) + _perm(kpe, swap_ref[...]) * per_row(sin32_ref)
    krope_ref[...] = krope.reshape(krope_ref.shape)
    kpe_placed = _perm(krope, place_ref[...])

    for hd in range(N_HEADS):
        kh = _head_norm(knope[:, hd * LANES:(hd + 1) * LANES] + kpe_placed, kgain_ref[...])
        k_ref[:, hd] = kh.astype(k_ref.dtype).reshape(nb, tm, LANES)

    if feature_major:
        half = D_ROPE // 2
        qraw_t = _dot_nt(wuq_ref[...], qn)
        v_t = _dot_nt(wuv_ref[...], ckv_b)
        ones = jnp.ones((V_ROWS - D_V, rows), _F32)
        for hd in range(N_HEADS):
            v_ref[0, hd, 0] = jnp.concatenate([v_t[hd * D_V:(hd + 1) * D_V], ones], axis=0).astype(v_ref.dtype)
        cos_t, sin_t, gain = cost_ref[...], sint_ref[...], qgain_ref[...]
        for hd in range(N_HEADS):
            base = hd * LANES
            nope = qraw_t[base:base + D_NOPE]
            x1 = qraw_t[base + D_NOPE:base + D_NOPE + half]
            x2 = qraw_t[base + D_NOPE + half:base + D_QK]
            r1 = x1 * cos_t - x2 * sin_t
            r2 = x1 * sin_t + x2 * cos_t
            ss = (jnp.sum(nope * nope, axis=0, keepdims=True) + jnp.sum(r1 * r1, axis=0, keepdims=True)
                  + jnp.sum(r2 * r2, axis=0, keepdims=True))
            scale = lax.rsqrt(ss / D_QK + EPS)
            qh = jnp.concatenate([nope * scale * gain[:D_NOPE], r1 * scale * gain[D_NOPE:D_NOPE + half],
                                  r2 * scale * gain[D_NOPE + half:D_QK],
                                  jnp.zeros((LANES - D_QK, rows), _F32)], axis=0)
            q_ref[0, hd] = qh.astype(q_ref.dtype)
    else:
        qraw = _dot(qn, wuq_ref[...])
        vfull = _dot(ckv_b, wuv_ref[...])
        lane = lax.broadcasted_iota(jnp.int32, (1, LANES), 1)
        first_half = lane < D_NOPE + D_ROPE // 2
        for hd in range(N_HEADS):
            sl = slice(hd * LANES, (hd + 1) * LANES)
            qh = qraw[:, sl]
            partner = jnp.where(first_half, pltpu.roll(qh, LANES - D_ROPE // 2, 1), pltpu.roll(qh, D_ROPE // 2, 1))
            qh = _head_norm(qh * cosp + partner * sinp, qgain_ref[...])
            q_ref[:, hd] = qh.astype(q_ref.dtype).reshape(nb, tm, LANES)
            v_ref[:, hd] = vfull[:, sl].astype(v_ref.dtype).reshape(nb, tm, LANES)


def _const_spec(a):
    nd = a.ndim
    return pl.BlockSpec(a.shape, lambda *_: (0,) * nd, pipeline_mode=pl.Buffered(1))


def _skip_refs(fn, start, count):
    def wrapped(*refs, **kw):
        return fn(*refs[:start], *refs[start + count:], **kw)
    return wrapped


def _pre_call(x, hist, tabs, lw, consts, *, nb, hist_shared, truncated, feature_major=False, into=None):
    bsz, t_len, d_model = x.shape
    tm = t_len if nb > 1 else min(ROW_TILE, t_len)
    assert bsz % nb == 0 and t_len % tm == 0 and tm % HIST_ROWS == 0 and not (feature_major and nb > 1)
    grid = (bsz // nb, t_len // tm)
    fm = "_t" if feature_major else ""
    weights = (lw["gmix"], lw["w_in"], lw["gq"], lw["w_uq" + fm], lw["gkv"], lw["w_uk"], lw["w_uv" + fm],
               lw["qgain" + fm], lw["kgain"], lw["w_pool"], lw["pscale"], consts["place"], consts["swap"])
    hist_spec = (pl.BlockSpec((1, HIST_ROWS, POOL_WIDTH), lambda b, t: (0, 0, 0)) if hist_shared
                 else pl.BlockSpec((nb, HIST_ROWS, POOL_WIDTH), lambda b, t: (b, 0, 0)))
    in_specs = [pl.BlockSpec((nb, tm, d_model), lambda b, t: (b, t, 0)), hist_spec]
    in_specs += [pl.BlockSpec((tm, a.shape[1]), lambda b, t: (t, 0)) for a in tabs[:4]]
    in_specs += [pl.BlockSpec((a.shape[0], tm), lambda b, t: (0, t)) for a in tabs[4:]]
    in_specs += [_const_spec(w) for w in weights]
    head_spec = pl.BlockSpec((nb, N_HEADS, tm, LANES), lambda b, t: (b, 0, t, 0))
    head_shape = jax.ShapeDtypeStruct((bsz, N_HEADS, t_len, LANES), _BF16)
    if feature_major:
        q_spec = pl.BlockSpec((1, N_HEADS, LANES, tm), lambda b, t: (b, 0, 0, t))
        q_shape = jax.ShapeDtypeStruct((bsz, N_HEADS, LANES, t_len), _BF16)
        v_spec = pl.BlockSpec((1, N_HEADS, 1, V_ROWS, tm), lambda b, t: (b, 0, t, 0, 0))
        v_shape = jax.ShapeDtypeStruct((bsz, N_HEADS, t_len // tm, V_ROWS, tm), _BF16)
    else:
        q_spec, q_shape, v_spec, v_shape = head_spec, head_shape, head_spec, head_shape

    def row_out(width, dtype):
        return (pl.BlockSpec((nb, tm, width), lambda b, t: (b, t, 0)),
                jax.ShapeDtypeStruct((bsz, t_len, width), dtype))

    body = functools.partial(_pre_kernel, nb=nb, tm=tm, hist_shared=hist_shared, truncated=truncated,
                             feature_major=feature_major)
    operands = [x, hist, *tabs, *weights]
    aliases = {}
    if into is None:
        lat_spec, lat_shape = row_out(KV_RANK, _F32)
        kr_spec, kr_shape = row_out(D_ROPE, _F32)
    else:
        layer, depth, bufs = into
        assert nb == 1

        def seq_out(width):
            block = tuple(pl.Element(n) for n in (1, 1, tm, width))
            return (pl.BlockSpec(block, lambda b, t: (layer, b, (1 + t * (tm // N_META)) * N_META, 0)),
                    jax.ShapeDtypeStruct((depth, bsz, N_META + t_len, width), _F32))

        lat_spec, lat_shape = seq_out(KV_RANK)
        kr_spec, kr_shape = seq_out(D_ROPE)
        if bufs is not None:
            body = _skip_refs(body, len(operands), len(bufs))
            aliases = {len(operands): 3, len(operands) + 1: 4}
            in_specs += [pl.BlockSpec(memory_space=pl.ANY)] * len(bufs)
            operands += list(bufs)
    pool_spec, pool_shape = row_out(POOL_WIDTH, _BF16)
    out_specs = [q_spec, head_spec, v_spec, lat_spec, kr_spec, pool_spec,
                 pl.BlockSpec((nb, HIST_ROWS, POOL_WIDTH), lambda b, t: (b, 0, 0))]
    out_shape = [q_shape, head_shape, v_shape, lat_shape, kr_shape, pool_shape,
                 jax.ShapeDtypeStruct((bsz, HIST_ROWS, POOL_WIDTH), _F32)]
    q, k, v, lat, krope, pool, ulast = pl.pallas_call(
        body, grid=grid, in_specs=in_specs, out_specs=out_specs, out_shape=out_shape,
        input_output_aliases=aliases,
        scratch_shapes=[pltpu.VMEM((nb, HIST_ROWS + tm, POOL_WIDTH), _F32)],
        compiler_params=pltpu.CompilerParams(dimension_semantics=("arbitrary", "arbitrary"),
                                             vmem_limit_bytes=VMEM_LIMIT_BYTES),
        name="pre_mixer",
    )(*operands)
    return dict(q=q, k=k, v=v, lat=lat, krope=krope, pool=pool, ulast=ulast)


def _meta_rows_kernel(lat_any, kr_any, mlat_ref, mkr_ref, lat_ref, kr_ref):
    del lat_any, kr_any
    lat_ref[0, 0] = mlat_ref[0]
    kr_ref[0, 0] = mkr_ref[0]


def _meta_rows_call(lat_all, kr_all, meta_lat, meta_kr):
    depth, bsz = lat_all.shape[:2]

    def src(width):
        return pl.BlockSpec((1, N_META, width), lambda l, b: (l, 0, 0))

    def dst(width):
        return pl.BlockSpec((1, 1, N_META, width), lambda l, b: (l, b, 0, 0))

    return pl.pallas_call(
        _meta_rows_kernel, grid=(depth, bsz),
        in_specs=[pl.BlockSpec(memory_space=pl.ANY)] * 2 + [src(KV_RANK), src(D_ROPE)],
        out_specs=[dst(KV_RANK), dst(D_ROPE)],
        out_shape=[jax.ShapeDtypeStruct(lat_all.shape, lat_all.dtype), jax.ShapeDtypeStruct(kr_all.shape, kr_all.dtype)],
        input_output_aliases={0: 0, 1: 1},
        compiler_params=pltpu.CompilerParams(dimension_semantics=("arbitrary", "arbitrary")),
        name="meta_rows",
    )(lat_all, kr_all, meta_lat, meta_kr)


def _post_kernel(x_ref, pool_ref, attn_ref, wop_ref, woa_ref, gffn_ref, wg_ref, wu_ref, wd_ref, o_ref):
    x1 = x_ref[...] + _dot(pool_ref[...], wop_ref[...]) + _dot(attn_ref[...], woa_ref[...])
    hn = _rms(x1, gffn_ref[...]).astype(_BF16)
    g = _dot(hn, wg_ref[...])
    up = _dot(hn, wu_ref[...])
    act = (g * jax.nn.sigmoid(g) * up).astype(_BF16)
    o_ref[...] = x1 + _dot(act, wd_ref[...])


def _post_call(x, pool, attn, lw):
    shape = x.shape
    d_model = shape[-1]
    x2 = x.reshape(-1, d_model)
    n = x2.shape[0]
    tm = min(ROW_TILE, n)
    assert n % tm == 0
    weights = (lw["w_o_pool"], lw["w_o_attn"], lw["gffn"], lw["w_gate"], lw["w_up"], lw["w_down"])

    def rows(width):
        return pl.BlockSpec((tm, width), lambda i: (i, 0))

    out = pl.pallas_call(
        _post_kernel, grid=(n // tm,),
        in_specs=[rows(d_model), rows(POOL_WIDTH), rows(N_HEADS * D_V)] + [_const_spec(w) for w in weights],
        out_specs=rows(d_model), out_shape=jax.ShapeDtypeStruct((n, d_model), _F32),
        compiler_params=pltpu.CompilerParams(dimension_semantics=("arbitrary",),
                                             vmem_limit_bytes=VMEM_LIMIT_BYTES),
        name="post_ffn",
    )(x2, pool.reshape(n, POOL_WIDTH), attn.reshape(n, N_HEADS * D_V), *weights)
    return out.reshape(shape)


def _attn_kernel(kbound_ref, q_ref, q_next_ref, k_ref, v_ref, km_ref, vm_ref, o_ref,
                 acc_ref, m_ref, s_meta_ref, s_first_ref, s_even_ref, s_odd_ref, *, tile, online):
    qi = pl.program_id(2)

    def scores(hh, k_rows, queries_ref=q_ref):
        return _dot(k_rows, queries_ref[0, hh])

    def consume(hh, s_t, v_cols, mask):
        if mask is not None:
            s_t = jnp.where(mask, s_t, MASKED)
        m_prev = m_ref[hh]
        m_next = jnp.maximum(m_prev, jnp.max(s_t, axis=0, keepdims=True)) if online else m_prev
        p_t = jnp.exp2(s_t - m_next).astype(_BF16)
        pv = _dot(v_cols, p_t)
        if online:
            acc_ref[hh] = jnp.exp2(m_prev - m_next) * acc_ref[hh] + pv
            m_ref[hh] = m_next
        else:
            acc_ref[hh] += pv

    def key_tile(hh, j):
        return k_ref[0, hh, pl.ds(pl.multiple_of(j * tile, tile), tile), :]

    def value_tile(hh, j):
        return v_ref[0, hh, j]

    heads = range(q_ref.shape[1])
    for hh in heads:
        acc_ref[hh] = jnp.zeros(acc_ref.shape[1:], _F32)
        if online:
            m_ref[hh] = jnp.full((1, tile), MASKED, _F32)
        else:
            qf = q_ref[0, hh].astype(_F32)
            m_ref[hh] = jnp.sqrt(jnp.sum(qf * qf, axis=0, keepdims=True)) * kbound_ref[...]
        s_meta_ref[hh] = scores(hh, km_ref[hh])

    @pl.when(qi == 0)
    def _():
        for hh in heads:
            s_even_ref[hh] = scores(hh, key_tile(hh, 0))

    def step(j, s_cur_ref, s_next_ref):
        for hh in heads:
            s_next_ref[hh] = scores(hh, key_tile(hh, j + 1))
            consume(hh, s_cur_ref[hh], value_tile(hh, j), None)

    @pl.when(qi >= 1)
    def _():
        step(0, s_first_ref, s_odd_ref)

    def two_steps(jj, carry):
        step(2 * jj + 1, s_odd_ref, s_even_ref)
        step(2 * jj + 2, s_even_ref, s_odd_ref)
        return carry

    lax.fori_loop(0, jnp.maximum(qi - 1, 0) // 2, two_steps, 0)

    @pl.when(jnp.logical_and(qi >= 2, qi % 2 == 0))
    def _():
        step(qi - 1, s_odd_ref, s_even_ref)

    key_chunk = lax.broadcasted_iota(jnp.int32, (tile, tile), 0) // CHUNK
    query_chunk = lax.broadcasted_iota(jnp.int32, (tile, tile), 1) // CHUNK
    meta_mask = lax.broadcasted_iota(jnp.int32, (km_ref.shape[1], tile), 0) < N_META
    mask = jnp.concatenate([query_chunk >= key_chunk, meta_mask], axis=0)
    for parity, s_fin_ref in enumerate((s_even_ref, s_odd_ref)):
        @pl.when(qi % 2 == parity)
        def _():
            for hh in heads:
                s_first_ref[hh] = scores(hh, key_tile(hh, 0), q_next_ref)
                s_t = jnp.concatenate([s_fin_ref[hh], s_meta_ref[hh]], axis=0)
                v_cols = jnp.concatenate([value_tile(hh, qi), vm_ref[hh]], axis=1)
                consume(hh, s_t, v_cols, mask)

    out_t = jnp.concatenate([acc_ref[hh, :D_V] / acc_ref[hh, D_V:D_V + 1] for hh in heads], axis=0)
    o_ref[0] = out_t.T.astype(o_ref.dtype)


def _prompt_attn_call(kbound, q, k, v, km, vm, *, online):
    bsz, n_heads, _, s_len = q.shape
    tile = v.shape[-1]
    n_tiles = s_len // tile
    hps = ATTN_HEADS_PER_STEP
    assert s_len % tile == 0 and tile % CHUNK == 0 and n_heads % hps == 0 and (hps * D_V) % LANES == 0
    grid = (bsz, n_heads // hps, n_tiles)
    return pl.pallas_call(
        functools.partial(_attn_kernel, tile=tile, online=online), grid=grid,
        in_specs=[pl.BlockSpec((1, 1), lambda b, p, i: (0, 0)),
                  pl.BlockSpec((1, hps, LANES, tile), lambda b, p, i: (b, p, 0, i)),
                  pl.BlockSpec((1, hps, LANES, tile), lambda b, p, i: (b, p, 0, jnp.minimum(i + 1, n_tiles - 1))),
                  pl.BlockSpec((1, hps, s_len, LANES), lambda b, p, i: (b, p, 0, 0)),
                  pl.BlockSpec((1, hps, n_tiles, V_ROWS, tile), lambda b, p, i: (b, p, 0, 0, 0)),
                  pl.BlockSpec((hps, LANES, LANES), lambda b, p, i: (p, 0, 0)),
                  pl.BlockSpec((hps, V_ROWS, LANES), lambda b, p, i: (p, 0, 0))],
        out_specs=pl.BlockSpec((1, tile, hps * D_V), lambda b, p, i: (b, i, p)),
        out_shape=jax.ShapeDtypeStruct((bsz, s_len, n_heads * D_V), _BF16),
        scratch_shapes=[pltpu.VMEM((hps, V_ROWS, tile), _F32), pltpu.VMEM((hps, 1, tile), _F32),
                        pltpu.VMEM((hps, LANES, tile), _F32)]
                       + [pltpu.VMEM((hps, tile, tile), _F32)] * 3,
        compiler_params=pltpu.CompilerParams(dimension_semantics=("arbitrary",) * 3,
                                             vmem_limit_bytes=VMEM_LIMIT_BYTES),
        name="prompt_attn",
    )(kbound, q, q, k, v, km, vm)


def _meta_attn_kernel(q_ref, k_ref, v_ref, o_ref):
    for pair in range(N_HEADS // 2):
        out = None
        for hd in (2 * pair, 2 * pair + 1):
            s = _dot_nt(q_ref[hd], k_ref[hd])
            p = jnp.exp2(s - jnp.max(s, axis=1, keepdims=True))
            o = _dot(p.astype(_BF16), v_ref[hd]) / jnp.sum(p, axis=1, keepdims=True)
            out = o if out is None else out + o
        o_ref[:, pair * LANES:(pair + 1) * LANES] = out.astype(o_ref.dtype)


def _meta_attn_call(q, k, v):
    n_rows = q.shape[1]
    return pl.pallas_call(
        _meta_attn_kernel, out_shape=jax.ShapeDtypeStruct((n_rows, N_HEADS * D_V), _BF16),
        name="meta_attn",
    )(q, k, v)


def _row_sums_t(x):
    ones = jnp.ones((8, x.shape[1]), _BF16)
    hi = x.astype(_BF16)
    lo = (x - hi.astype(_F32)).astype(_BF16)
    return (_dot_nt(ones, hi) + _dot_nt(ones, lo))[0:1]


def _sample_attn_kernel(q_ref, kn_ref, vn_ref, lat_ref, kpe_ref, wukt_ref, wukq_ref, wuv_ref, kgain_ref, place_ref,
                        o_ref, *, past):
    t_len = q_ref.shape[2]
    kgain = kgain_ref[...]
    qg = [(q_ref[0, hd].astype(_F32) * kgain).astype(_BF16) for hd in range(N_HEADS)]
    q_all = jnp.concatenate(qg, axis=0)
    q_lat = jnp.concatenate([_dot(qg[hd], wukq_ref[hd]) for hd in range(N_HEADS)], axis=0).astype(_BF16)

    def cached_scores(lo, hi):
        lat = lat_ref[0, 0, lo:hi].astype(_BF16)
        kpe = kpe_ref[0, lo:hi]
        s = _dot_nt(q_lat, lat) + _dot_nt(q_all, _perm(kpe, place_ref[...]).astype(_BF16))
        ss_rope = _row_sums_t(kpe * kpe)
        rows = []
        for pair in range(N_HEADS // 2):
            k_t = _dot_nt(wukt_ref[pair * LANES:(pair + 1) * LANES], lat)
            for hh in range(2):
                hd = 2 * pair + hh
                k_h = k_t[hh * D_NOPE:(hh + 1) * D_NOPE]
                r = lax.rsqrt((jnp.sum(k_h * k_h, axis=0, keepdims=True) + ss_rope) / D_QK + EPS)
                rows.append(s[hd * t_len:(hd + 1) * t_len] * r)
        return jnp.concatenate(rows, axis=0), lat

    s_meta, lat_meta = cached_scores(0, N_META)
    s_past, lat_past = cached_scores(N_META, N_META + past)
    tq = lax.broadcasted_iota(jnp.int32, (t_len, t_len), 0)
    tk = lax.broadcasted_iota(jnp.int32, (t_len, t_len), 1)
    new_mask = (past + tq) // CHUNK >= (past + tk) // CHUNK
    s_new = jnp.concatenate([jnp.where(new_mask, _dot_nt(q_ref[0, hd], kn_ref[0, hd]), MASKED)
                             for hd in range(N_HEADS)], axis=0)
    m = jnp.maximum(jnp.maximum(jnp.max(s_meta, axis=1, keepdims=True), jnp.max(s_past, axis=1, keepdims=True)),
                    jnp.max(s_new, axis=1, keepdims=True))
    p_meta, p_past, p_new = jnp.exp2(s_meta - m), jnp.exp2(s_past - m), jnp.exp2(s_new - m)
    den = (jnp.sum(p_meta, axis=1, keepdims=True) + jnp.sum(p_past, axis=1, keepdims=True)
           + jnp.sum(p_new, axis=1, keepdims=True))
    p_lat = (_dot(p_meta.astype(_BF16), lat_meta) + _dot(p_past.astype(_BF16), lat_past)).astype(_BF16)
    p_new = p_new.astype(_BF16)
    for pair in range(N_HEADS // 2):
        out = None
        for hd in (2 * pair, 2 * pair + 1):
            rows = slice(hd * t_len, (hd + 1) * t_len)
            o = (_dot(p_lat[rows], wuv_ref[:, hd * LANES:(hd + 1) * LANES]) + _dot(p_new[rows], vn_ref[0, hd])) / den[rows]
            out = o if out is None else out + o
        o_ref[0, :, pair * LANES:(pair + 1) * LANES] = out.astype(o_ref.dtype)


def _sample_attn_call(layer, q, k_new, v_new, cache_lat, cache_kpe, lw, consts):
    bsz, n_heads, t_len, _ = q.shape
    l_cache = cache_lat.shape[2]
    past = l_cache - N_META
    assert past % LANES == 0 and t_len % HIST_ROWS == 0
    head_spec = pl.BlockSpec((1, n_heads, t_len, LANES), lambda b: (b, 0, 0, 0))
    weights = (lw["w_uk_t"], lw["w_uk_q"], lw["w_uv"], lw["kgain"], consts["place"])
    return pl.pallas_call(
        functools.partial(_sample_attn_kernel, past=past), grid=(bsz,),
        in_specs=[head_spec, head_spec, head_spec,
                  pl.BlockSpec((1, 1, l_cache, KV_RANK), lambda b: (layer, b, 0, 0)),
                  pl.BlockSpec((1, l_cache, D_ROPE), lambda b: (b, 0, 0))]
                 + [_const_spec(w) for w in weights],
        out_specs=pl.BlockSpec((1, t_len, n_heads * D_V), lambda b: (b, 0, 0)),
        out_shape=jax.ShapeDtypeStruct((bsz, t_len, n_heads * D_V), _BF16),
        compiler_params=pltpu.CompilerParams(dimension_semantics=("arbitrary",),
                                             vmem_limit_bytes=VMEM_LIMIT_BYTES),
        name="sample_attn",
    )(q, k_new, v_new, cache_lat, cache_kpe, *weights)


def _pack_heads(w, width, alternate=False):
    kdim = w.shape[0]
    w3 = w.reshape(kdim, N_HEADS, width)
    low = jnp.pad(w3, ((0, 0), (0, 0), (0, LANES - width)))
    if alternate:
        high = jnp.pad(w3, ((0, 0), (0, 0), (LANES - width, 0)))
        odd = (jnp.arange(N_HEADS) % 2 == 1)[None, :, None]
        low = jnp.where(odd, high, low)
    return low.reshape(kdim, N_HEADS * LANES)


def _layer_weights(l, norm_mix, w_in, q_a_norm, w_uq, kv_a_norm, w_uk, w_uv, q_norm, k_norm,
                   w_pool, pool_scale, w_o, norm_ffn, w_gate, w_up, w_down):
    d_in = w_in.shape[2]
    d_in_pad = -(-d_in // (2 * LANES)) * (2 * LANES)
    pad96 = lambda g: jnp.pad(g.astype(_F32), (0, LANES - D_QK))[None, :]
    q_scale = D_QK ** -0.5 * math.log2(math.e)
    return dict(
        gmix=norm_mix[l][None, :].astype(_F32),
        w_in=jnp.pad(w_in[l], ((0, 0), (0, d_in_pad - d_in))).astype(_BF16),
        gq=q_a_norm[l][None, :].astype(_F32),
        w_uq=_pack_heads(w_uq[l], D_QK).astype(_BF16),
        w_uq_t=_pack_heads(w_uq[l], D_QK).astype(_BF16).T,
        gkv=kv_a_norm[l][None, :].astype(_F32),
        w_uk=_pack_heads(w_uk[l], D_NOPE).astype(_BF16),
        w_uk_t=w_uk[l].astype(_BF16).T,
        w_uk_q=jnp.pad(w_uk[l].astype(_BF16).T.reshape(N_HEADS, D_NOPE, KV_RANK),
                       ((0, 0), (0, LANES - D_NOPE), (0, 0))),
        w_uv=_pack_heads(w_uv[l], D_V, alternate=True).astype(_BF16),
        w_uv_t=w_uv[l].astype(_BF16).T,
        qgain=pad96(q_norm[l]) * q_scale,
        qgain_t=(pad96(q_norm[l]) * q_scale).T,
        kgain=pad96(k_norm[l]),
        w_pool=w_pool[l].astype(_BF16),
        pscale=pool_scale[l][None, :].astype(_F32),
        w_o_pool=w_o[l][:POOL_WIDTH].astype(_BF16),
        w_o_attn=w_o[l][POOL_WIDTH:].astype(_BF16),
        gffn=norm_ffn[l][None, :].astype(_F32),
        w_gate=w_gate[l].astype(_BF16),
        w_up=w_up[l].astype(_BF16),
        w_down=w_down[l].astype(_BF16),
    )


def _rope_tables(pos0, t_len):
    half = D_ROPE // 2
    inv = ROPE_THETA ** (-jnp.arange(half, dtype=_F32) / half)
    ang = (pos0 + jnp.arange(t_len, dtype=jnp.int32)).astype(_F32)[:, None] * inv[None, :]
    cos, sin = jnp.cos(ang), jnp.sin(ang)
    cos32 = jnp.concatenate([cos, cos], axis=1)
    sin32 = jnp.concatenate([-sin, sin], axis=1)
    ones = jnp.ones((t_len, D_NOPE), _F32)
    zeros = jnp.zeros((t_len, D_NOPE), _F32)
    tail1 = jnp.ones((t_len, LANES - D_QK), _F32)
    tail0 = jnp.zeros((t_len, LANES - D_QK), _F32)
    cosp = jnp.concatenate([ones, cos32, tail1], axis=1)
    sinp = jnp.concatenate([zeros, sin32, tail0], axis=1)
    return (cosp, sinp, cos32, sin32, cos.T, sin.T)


def _perm_consts():
    half = D_ROPE // 2
    idx = jnp.arange(D_ROPE)
    place = jnp.zeros((D_ROPE, LANES), _F32).at[idx, D_NOPE + idx].set(1.0)
    swap = jnp.zeros((D_ROPE, D_ROPE), _F32).at[idx, (idx + half) % D_ROPE].set(1.0)
    return dict(place=place.astype(_BF16), swap=swap.astype(_BF16))


def kernel(x_prompt, x_sample, cache_latent, cache_krope, state_pool, meta_tokens, norm_mix, w_in, q_a_norm, w_uq, kv_a_norm, w_uk, w_uv, q_norm, k_norm, w_pool, pool_scale, w_o, norm_ffn, w_gate, w_up, w_down):
    depth = norm_mix.shape[0]
    bp, s_len, d_model = x_prompt.shape
    bs, t_len, _ = x_sample.shape
    past = cache_latent.shape[2] - N_META
    consts = _perm_consts()
    tabs_meta = _rope_tables(0, N_META)
    tabs_prompt = _rope_tables(N_META, s_len)
    tabs_sample = _rope_tables(N_META + past, t_len)
    nb_sample = math.gcd(bs, 8)

    xm = meta_tokens.astype(_F32)[None]
    xp = x_prompt
    xs = x_sample
    zero_hist = jnp.zeros((1, HIST_ROWS, POOL_WIDTH), _F32)
    seq_bufs = None
    meta_lat, meta_kr, pool_p, lat_s, kpe_s, pool_s = [], [], [], [], [], []
    for l in range(depth):
        lw = _layer_weights(l, norm_mix, w_in, q_a_norm, w_uq, kv_a_norm, w_uk, w_uv, q_norm, k_norm,
                            w_pool, pool_scale, w_o, norm_ffn, w_gate, w_up, w_down)
        pm = _pre_call(xm, zero_hist, tabs_meta, lw, consts, nb=1, hist_shared=True, truncated=True)
        pp = _pre_call(xp, pm["ulast"], tabs_prompt, lw, consts, nb=1, hist_shared=True, truncated=False,
                       feature_major=True, into=(l, depth, seq_bufs))
        seq_bufs = (pp["lat"], pp["krope"])
        hist_s = jnp.pad(state_pool[l].astype(_F32), ((0, 0), (HIST_ROWS - POOL_STATE, 0), (0, 0)))
        ps = _pre_call(xs, hist_s, tabs_sample, lw, consts, nb=nb_sample, hist_shared=False, truncated=False)

        km = jnp.pad(pm["k"][0], ((0, 0), (0, LANES - N_META), (0, 0)))
        vm_t = jnp.stack([pm["v"][0, hd, :, (hd % 2) * D_V:(hd % 2 + 1) * D_V].T for hd in range(N_HEADS)])
        vm_t = jnp.concatenate([vm_t, jnp.ones((N_HEADS, V_ROWS - D_V, N_META), vm_t.dtype)], axis=1)
        vm_t = jnp.pad(vm_t, ((0, 0), (0, 0), (0, LANES - N_META)))
        kbound = (D_QK ** 0.5 * jnp.max(jnp.abs(lw["kgain"]))).reshape(1, 1)
        qbound = D_QK ** 0.5 * jnp.max(jnp.abs(lw["qgain"]))
        attn_args = (kbound, pp["q"], pp["k"], pp["v"], km, vm_t)
        attn_p = lax.cond(2.0 * qbound * kbound[0, 0] <= EXP2_SAFE_RANGE,
                          functools.partial(_prompt_attn_call, online=False),
                          functools.partial(_prompt_attn_call, online=True), *attn_args)
        attn_s = _sample_attn_call(l, ps["q"], ps["k"], ps["v"], cache_latent, cache_krope[l], lw, consts)
        xp = _post_call(xp, pp["pool"], attn_p, lw)
        xs = _post_call(xs, ps["pool"], attn_s, lw)
        if l + 1 < depth:
            attn_m = _meta_attn_call(pm["q"][0], pm["k"][0], pm["v"][0])
            xm = _post_call(xm, pm["pool"], attn_m[None], lw)

        meta_lat.append(pm["lat"][0])
        meta_kr.append(pm["krope"][0])
        pool_p.append(pp["ulast"][:, HIST_ROWS - POOL_STATE:])
        lat_s.append(ps["lat"])
        kpe_s.append(ps["krope"])
        pool_s.append(ps["ulast"][:, HIST_ROWS - POOL_STATE:])
    lat_p, kpe_p = _meta_rows_call(*seq_bufs, jnp.stack(meta_lat), jnp.stack(meta_kr))
    return (xp, xs, lat_p, kpe_p, jnp.stack(pool_p), jnp.stack(lat_s), jnp.stack(kpe_s), jnp.stack(pool_s))
```

```python
import functools
import math

import jax
import jax.numpy as jnp
from jax import lax
from jax.experimental import pallas as pl
from jax.experimental.pallas import tpu as pltpu

CHUNK = 64
N_META = 16
N_HEADS = 8
D_NOPE = 64
D_ROPE = 32
D_QK = D_NOPE + D_ROPE
D_V = 64
Q_RANK = 384
KV_RANK = 256
POOL_WINDOWS = (2, 4, 8, 16)
POOL_GROUP = 128
POOL_WIDTH = POOL_GROUP * len(POOL_WINDOWS)
POOL_STATE = max(POOL_WINDOWS) - 1
ROPE_THETA = 10000.0
EPS = 1e-6

LANES = 128
HIST_ROWS = 16
VMEM_LIMIT_BYTES = 56 * 1024 * 1024
ROW_TILE = 512
PRE_SPLITS = 2
ATTN_HEADS_PER_STEP = 4
V_ROWS = D_V + 16
MASKED = -1e30
EXP2_SAFE_RANGE = 120.0

_F32 = jnp.float32
_BF16 = jnp.bfloat16
_NT = (((1,), (1,)), ((), ()))


def _dot(a, b):
    return jnp.dot(a, b, preferred_element_type=_F32)


def _dot_nt(a, b):
    return lax.dot_general(a, b, _NT, preferred_element_type=_F32)


def _rms(x, g):
    return x * lax.rsqrt(jnp.mean(x * x, axis=-1, keepdims=True) + EPS) * g


def _perm(x, p):
    hi = x.astype(_BF16)
    lo = (x - hi.astype(_F32)).astype(_BF16)
    return _dot(hi, p) + _dot(lo, p)


def _head_norm(x, gain):
    ss = jnp.sum(x * x, axis=-1, keepdims=True)
    return x * lax.rsqrt(ss / D_QK + EPS) * gain


def _pre_kernel(x_ref, hist_ref, cosp_ref, sinp_ref, cos32_ref, sin32_ref, cost_ref, sint_ref,
                gmix_ref, win_ref, gq_ref, wuq_ref, gkv_ref, wuk_ref, wuv_ref, qgain_ref, kgain_ref,
                wpool_ref, pscale_ref, place_ref, swap_ref,
                q_ref, k_ref, v_ref, lat_ref, krope_ref, pool_ref, ulast_ref,
                ext_ref, *, nb, tm, hist_shared, truncated, feature_major, splits):
    t = pl.program_id(1)
    d_model = x_ref.shape[-1]
    for b in range(nb):
        @pl.when(t == 0)
        def _():
            ext_ref[b, 0:HIST_ROWS, :] = hist_ref[0 if hist_shared else b]

        @pl.when(t > 0)
        def _():
            ext_ref[b, 0:HIST_ROWS, :] = ext_ref[b, tm:tm + HIST_ROWS, :]

    def row_range(r0, rs):
        rows = nb * rs
        row_sl = slice(r0, r0 + rs)
        x = x_ref[:, row_sl, :].reshape(rows, d_model)
        h = _rms(x, gmix_ref[...]).astype(_BF16)
        z = _dot(h, win_ref[...])
        u = z[:, :POOL_WIDTH]
        q_lat = z[:, POOL_WIDTH:POOL_WIDTH + Q_RANK]
        kv_lat = z[:, POOL_WIDTH + Q_RANK:POOL_WIDTH + Q_RANK + KV_RANK]
        kpe = z[:, POOL_WIDTH + Q_RANK + KV_RANK:POOL_WIDTH + Q_RANK + KV_RANK + D_ROPE]

        first = HIST_ROWS + r0
        for b in range(nb):
            ext_ref[b, first:first + rs, :] = u[b * rs:(b + 1) * rs]
        for b in range(nb):
            cols = []
            for g, w in enumerate(POOL_WINDOWS):
                sl = slice(g * POOL_GROUP, (g + 1) * POOL_GROUP)
                cur = ext_ref[b, first:first + rs, sl]
                acc = cur
                for j in range(1, w):
                    acc = acc + ext_ref[b, first - j:first - j + rs, sl]
                if truncated:
                    row = t * tm + r0 + lax.broadcasted_iota(jnp.int32, (rs, 1), 0)
                    mean = acc / jnp.minimum(row + 1, w).astype(_F32)
                else:
                    mean = acc / float(w)
                cols.append(_dot((mean - cur).astype(_BF16), wpool_ref[g]))
            pool = jnp.concatenate(cols, axis=1) * pscale_ref[...]
            pool_ref[b, row_sl, :] = pool.astype(pool_ref.dtype)

        qn = _rms(q_lat, gq_ref[...]).astype(_BF16)
        c_kv = _rms(kv_lat, gkv_ref[...])
        into_results = len(lat_ref.shape) == 4
        lat_view = lat_ref.at[0] if into_results else lat_ref
        krope_view = krope_ref.at[0] if into_results else krope_ref
        lat_view[:, row_sl, :] = c_kv.reshape(nb, rs, KV_RANK)
        ckv_b = c_kv.astype(_BF16)
        knope = _dot(ckv_b, wuk_ref[...])

        def per_row(ref):
            tab = ref[row_sl, :]
            return tab if nb == 1 else jnp.concatenate([tab] * nb, axis=0)

        krope = kpe * per_row(cos32_ref) + _perm(kpe, swap_ref[...]) * per_row(sin32_ref)
        krope_view[:, row_sl, :] = krope.reshape(nb, rs, D_ROPE)
        kpe_placed = _perm(krope, place_ref[...])

        for hd in range(N_HEADS):
            kh = _head_norm(knope[:, hd * LANES:(hd + 1) * LANES] + kpe_placed, kgain_ref[...])
            k_ref[:, hd, row_sl, :] = kh.astype(k_ref.dtype).reshape(nb, rs, LANES)

        if feature_major:
            half = D_ROPE // 2
            qraw_t = _dot_nt(wuq_ref[...], qn)
            v_t = _dot_nt(wuv_ref[...], ckv_b)
            ones = jnp.ones((V_ROWS - D_V, rows), _F32)
            for hd in range(N_HEADS):
                v_h = jnp.concatenate([v_t[hd * D_V:(hd + 1) * D_V], ones], axis=0)
                v_ref[0, hd, 0, :, row_sl] = v_h.astype(v_ref.dtype)
            cos_t, sin_t, gain = cost_ref[:, row_sl], sint_ref[:, row_sl], qgain_ref[...]
            for hd in range(N_HEADS):
                base = hd * LANES
                nope = qraw_t[base:base + D_NOPE]
                x1 = qraw_t[base + D_NOPE:base + D_NOPE + half]
                x2 = qraw_t[base + D_NOPE + half:base + D_QK]
                r1 = x1 * cos_t - x2 * sin_t
                r2 = x1 * sin_t + x2 * cos_t
                ss = (jnp.sum(nope * nope, axis=0, keepdims=True) + jnp.sum(r1 * r1, axis=0, keepdims=True)
                      + jnp.sum(r2 * r2, axis=0, keepdims=True))
                scale = lax.rsqrt(ss / D_QK + EPS)
                qh = jnp.concatenate([nope * scale * gain[:D_NOPE], r1 * scale * gain[D_NOPE:D_NOPE + half],
                                      r2 * scale * gain[D_NOPE + half:D_QK],
                                      jnp.zeros((LANES - D_QK, rows), _F32)], axis=0)
                q_ref[0, hd, :, row_sl] = qh.astype(q_ref.dtype)
        else:
            cosp, sinp = per_row(cosp_ref), per_row(sinp_ref)
            qraw = _dot(qn, wuq_ref[...])
            vfull = _dot(ckv_b, wuv_ref[...])
            lane = lax.broadcasted_iota(jnp.int32, (1, LANES), 1)
            first_half = lane < D_NOPE + D_ROPE // 2
            for hd in range(N_HEADS):
                sl = slice(hd * LANES, (hd + 1) * LANES)
                qh = qraw[:, sl]
                partner = jnp.where(first_half, pltpu.roll(qh, LANES - D_ROPE // 2, 1),
                                    pltpu.roll(qh, D_ROPE // 2, 1))
                qh = _head_norm(qh * cosp + partner * sinp, qgain_ref[...])
                q_ref[:, hd, row_sl, :] = qh.astype(q_ref.dtype).reshape(nb, rs, LANES)
                v_ref[:, hd, row_sl, :] = vfull[:, sl].astype(v_ref.dtype).reshape(nb, rs, LANES)

    for sp in range(splits):
        row_range(sp * (tm // splits), tm // splits)
    for b in range(nb):
        ulast_ref[b] = ext_ref[b, tm:tm + HIST_ROWS, :]


def _const_spec(a):
    nd = a.ndim
    return pl.BlockSpec(a.shape, lambda *_: (0,) * nd, pipeline_mode=pl.Buffered(1))


def _skip_refs(fn, start, count):
    def wrapped(*refs, **kw):
        return fn(*refs[:start], *refs[start + count:], **kw)
    return wrapped


def _pre_call(x, hist, tabs, lw, consts, *, nb, hist_shared, truncated, feature_major=False, into=None):
    bsz, t_len, d_model = x.shape
    tm = t_len if nb > 1 else min(ROW_TILE, t_len)
    assert bsz % nb == 0 and t_len % tm == 0 and tm % HIST_ROWS == 0 and not (feature_major and nb > 1)
    grid = (bsz // nb, t_len // tm)
    fm = "_t" if feature_major else ""
    weights = (lw["gmix"], lw["w_in"], lw["gq"], lw["w_uq" + fm], lw["gkv"], lw["w_uk"], lw["w_uv" + fm],
               lw["qgain" + fm], lw["kgain"], lw["w_pool"], lw["pscale"], consts["place"], consts["swap"])
    hist_spec = (pl.BlockSpec((1, HIST_ROWS, POOL_WIDTH), lambda b, t: (0, 0, 0)) if hist_shared
                 else pl.BlockSpec((nb, HIST_ROWS, POOL_WIDTH), lambda b, t: (b, 0, 0)))
    in_specs = [pl.BlockSpec((nb, tm, d_model), lambda b, t: (b, t, 0)), hist_spec]
    in_specs += [pl.BlockSpec((tm, a.shape[1]), lambda b, t: (t, 0)) for a in tabs[:4]]
    in_specs += [pl.BlockSpec((a.shape[0], tm), lambda b, t: (0, t)) for a in tabs[4:]]
    in_specs += [_const_spec(w) for w in weights]
    head_spec = pl.BlockSpec((nb, N_HEADS, tm, LANES), lambda b, t: (b, 0, t, 0))
    head_shape = jax.ShapeDtypeStruct((bsz, N_HEADS, t_len, LANES), _BF16)
    if feature_major:
        q_spec = pl.BlockSpec((1, N_HEADS, LANES, tm), lambda b, t: (b, 0, 0, t))
        q_shape = jax.ShapeDtypeStruct((bsz, N_HEADS, LANES, t_len), _BF16)
        v_spec = pl.BlockSpec((1, N_HEADS, 1, V_ROWS, tm), lambda b, t: (b, 0, t, 0, 0))
        v_shape = jax.ShapeDtypeStruct((bsz, N_HEADS, t_len // tm, V_ROWS, tm), _BF16)
    else:
        q_spec, q_shape, v_spec, v_shape = head_spec, head_shape, head_spec, head_shape

    def row_out(width, dtype):
        return (pl.BlockSpec((nb, tm, width), lambda b, t: (b, t, 0)),
                jax.ShapeDtypeStruct((bsz, t_len, width), dtype))

    body = functools.partial(_pre_kernel, nb=nb, tm=tm, hist_shared=hist_shared, truncated=truncated,
                             feature_major=feature_major, splits=PRE_SPLITS if tm == ROW_TILE else 1)
    operands = [x, hist, *tabs, *weights]
    aliases = {}
    if into is None:
        lat_spec, lat_shape = row_out(KV_RANK, _F32)
        kr_spec, kr_shape = row_out(D_ROPE, _F32)
    else:
        layer, depth, bufs = into
        assert nb == 1

        def seq_out(width):
            block = tuple(pl.Element(n) for n in (1, 1, tm, width))
            return (pl.BlockSpec(block, lambda b, t: (layer, b, (1 + t * (tm // N_META)) * N_META, 0)),
                    jax.ShapeDtypeStruct((depth, bsz, N_META + t_len, width), _F32))

        lat_spec, lat_shape = seq_out(KV_RANK)
        kr_spec, kr_shape = seq_out(D_ROPE)
        if bufs is not None:
            body = _skip_refs(body, len(operands), len(bufs))
            aliases = {len(operands): 3, len(operands) + 1: 4}
            in_specs += [pl.BlockSpec(memory_space=pl.ANY)] * len(bufs)
            operands += list(bufs)
    pool_spec, pool_shape = row_out(POOL_WIDTH, _BF16)
    out_specs = [q_spec, head_spec, v_spec, lat_spec, kr_spec, pool_spec,
                 pl.BlockSpec((nb, HIST_ROWS, POOL_WIDTH), lambda b, t: (b, 0, 0))]
    out_shape = [q_shape, head_shape, v_shape, lat_shape, kr_shape, pool_shape,
                 jax.ShapeDtypeStruct((bsz, HIST_ROWS, POOL_WIDTH), _F32)]
    q, k, v, lat, krope, pool, ulast = pl.pallas_call(
        body, grid=grid, in_specs=in_specs, out_specs=out_specs, out_shape=out_shape,
        input_output_aliases=aliases,
        scratch_shapes=[pltpu.VMEM((nb, HIST_ROWS + tm, POOL_WIDTH), _F32)],
        compiler_params=pltpu.CompilerParams(dimension_semantics=("arbitrary", "arbitrary"),
                                             vmem_limit_bytes=VMEM_LIMIT_BYTES),
        name="pre_mixer",
    )(*operands)
    return dict(q=q, k=k, v=v, lat=lat, krope=krope, pool=pool, ulast=ulast)


def _meta_rows_kernel(lat_any, kr_any, mlat_ref, mkr_ref, lat_ref, kr_ref):
    del lat_any, kr_any
    lat_ref[0, 0] = mlat_ref[0]
    kr_ref[0, 0] = mkr_ref[0]


def _meta_rows_call(lat_all, kr_all, meta_lat, meta_kr):
    depth, bsz = lat_all.shape[:2]

    def src(width):
        return pl.BlockSpec((1, N_META, width), lambda l, b: (l, 0, 0))

    def dst(width):
        return pl.BlockSpec((1, 1, N_META, width), lambda l, b: (l, b, 0, 0))

    return pl.pallas_call(
        _meta_rows_kernel, grid=(depth, bsz),
        in_specs=[pl.BlockSpec(memory_space=pl.ANY)] * 2 + [src(KV_RANK), src(D_ROPE)],
        out_specs=[dst(KV_RANK), dst(D_ROPE)],
        out_shape=[jax.ShapeDtypeStruct(lat_all.shape, lat_all.dtype), jax.ShapeDtypeStruct(kr_all.shape, kr_all.dtype)],
        input_output_aliases={0: 0, 1: 1},
        compiler_params=pltpu.CompilerParams(dimension_semantics=("arbitrary", "arbitrary")),
        name="meta_rows",
    )(lat_all, kr_all, meta_lat, meta_kr)


def _post_kernel(x_ref, pool_ref, attn_ref, wop_ref, woa_ref, gffn_ref, wg_ref, wu_ref, wd_ref, o_ref):
    x1 = x_ref[...] + _dot(pool_ref[...], wop_ref[...]) + _dot(attn_ref[...], woa_ref[...])
    hn = _rms(x1, gffn_ref[...]).astype(_BF16)
    g = _dot(hn, wg_ref[...])
    up = _dot(hn, wu_ref[...])
    act = (g * jax.nn.sigmoid(g) * up).astype(_BF16)
    o_ref[...] = x1 + _dot(act, wd_ref[...])


def _post_call(x, pool, attn, lw):
    shape = x.shape
    d_model = shape[-1]
    x2 = x.reshape(-1, d_model)
    n = x2.shape[0]
    tm = min(ROW_TILE, n)
    assert n % tm == 0
    weights = (lw["w_o_pool"], lw["w_o_attn"], lw["gffn"], lw["w_gate"], lw["w_up"], lw["w_down"])

    def rows(width):
        return pl.BlockSpec((tm, width), lambda i: (i, 0))

    out = pl.pallas_call(
        _post_kernel, grid=(n // tm,),
        in_specs=[rows(d_model), rows(POOL_WIDTH), rows(N_HEADS * D_V)] + [_const_spec(w) for w in weights],
        out_specs=rows(d_model), out_shape=jax.ShapeDtypeStruct((n, d_model), _F32),
        compiler_params=pltpu.CompilerParams(dimension_semantics=("arbitrary",),
                                             vmem_limit_bytes=VMEM_LIMIT_BYTES),
        name="post_ffn",
    )(x2, pool.reshape(n, POOL_WIDTH), attn.reshape(n, N_HEADS * D_V), *weights)
    return out.reshape(shape)


def _attn_kernel(kbound_ref, q_ref, q_next_ref, k_ref, v_ref, km_ref, vm_ref, o_ref,
                 acc_ref, m_ref, s_meta_ref, s_first_ref, s_even_ref, s_odd_ref, *, tile, online):
    qi = pl.program_id(2)

    def scores(hh, k_rows, queries_ref=q_ref):
        return _dot(k_rows, queries_ref[0, hh])

    def consume(hh, s_t, v_cols, mask):
        if mask is not None:
            s_t = jnp.where(mask, s_t, MASKED)
        m_prev = m_ref[hh]
        m_next = jnp.maximum(m_prev, jnp.max(s_t, axis=0, keepdims=True)) if online else m_prev
        p_t = jnp.exp2(s_t - m_next).astype(_BF16)
        pv = _dot(v_cols, p_t)
        if online:
            acc_ref[hh] = jnp.exp2(m_prev - m_next) * acc_ref[hh] + pv
            m_ref[hh] = m_next
        else:
            acc_ref[hh] += pv

    def key_tile(hh, j):
        return k_ref[0, hh, pl.ds(pl.multiple_of(j * tile, tile), tile), :]

    def value_tile(hh, j):
        return v_ref[0, hh, j]

    heads = range(q_ref.shape[1])
    for hh in heads:
        acc_ref[hh] = jnp.zeros(acc_ref.shape[1:], _F32)
        if online:
            m_ref[hh] = jnp.full((1, tile), MASKED, _F32)
        else:
            qf = q_ref[0, hh].astype(_F32)
            m_ref[hh] = jnp.sqrt(jnp.sum(qf * qf, axis=0, keepdims=True)) * kbound_ref[...]
        s_meta_ref[hh] = scores(hh, km_ref[hh])

    @pl.when(qi == 0)
    def _():
        for hh in heads:
            s_even_ref[hh] = scores(hh, key_tile(hh, 0))

    def step(j, s_cur_ref, s_next_ref):
        for hh in heads:
            s_next_ref[hh] = scores(hh, key_tile(hh, j + 1))
            consume(hh, s_cur_ref[hh], value_tile(hh, j), None)

    @pl.when(qi >= 1)
    def _():
        step(0, s_first_ref, s_odd_ref)

    def two_steps(jj, carry):
        step(2 * jj + 1, s_odd_ref, s_even_ref)
        step(2 * jj + 2, s_even_ref, s_odd_ref)
        return carry

    lax.fori_loop(0, jnp.maximum(qi - 1, 0) // 2, two_steps, 0)

    @pl.when(jnp.logical_and(qi >= 2, qi % 2 == 0))
    def _():
        step(qi - 1, s_odd_ref, s_even_ref)

    key_chunk = lax.broadcasted_iota(jnp.int32, (tile, tile), 0) // CHUNK
    query_chunk = lax.broadcasted_iota(jnp.int32, (tile, tile), 1) // CHUNK
    meta_mask = lax.broadcasted_iota(jnp.int32, (km_ref.shape[1], tile), 0) < N_META
    mask = jnp.concatenate([query_chunk >= key_chunk, meta_mask], axis=0)
    for parity, s_fin_ref in enumerate((s_even_ref, s_odd_ref)):
        @pl.when(qi % 2 == parity)
        def _():
            for hh in heads:
                s_first_ref[hh] = scores(hh, key_tile(hh, 0), q_next_ref)
                s_t = jnp.concatenate([s_fin_ref[hh], s_meta_ref[hh]], axis=0)
                v_cols = jnp.concatenate([value_tile(hh, qi), vm_ref[hh]], axis=1)
                consume(hh, s_t, v_cols, mask)

    out_t = jnp.concatenate([acc_ref[hh, :D_V] / acc_ref[hh, D_V:D_V + 1] for hh in heads], axis=0)
    o_ref[0] = out_t.T.astype(o_ref.dtype)


def _prompt_attn_call(kbound, q, k, v, km, vm, *, online):
    bsz, n_heads, _, s_len = q.shape
    tile = v.shape[-1]
    n_tiles = s_len // tile
    hps = ATTN_HEADS_PER_STEP
    assert s_len % tile == 0 and tile % CHUNK == 0 and n_heads % hps == 0 and (hps * D_V) % LANES == 0
    grid = (bsz, n_heads // hps, n_tiles)
    return pl.pallas_call(
        functools.partial(_attn_kernel, tile=tile, online=online), grid=grid,
        in_specs=[pl.BlockSpec((1, 1), lambda b, p, i: (0, 0)),
                  pl.BlockSpec((1, hps, LANES, tile), lambda b, p, i: (b, p, 0, i)),
                  pl.BlockSpec((1, hps, LANES, tile), lambda b, p, i: (b, p, 0, jnp.minimum(i + 1, n_tiles - 1))),
                  pl.BlockSpec((1, hps, s_len, LANES), lambda b, p, i: (b, p, 0, 0)),
                  pl.BlockSpec((1, hps, n_tiles, V_ROWS, tile), lambda b, p, i: (b, p, 0, 0, 0)),
                  pl.BlockSpec((hps, LANES, LANES), lambda b, p, i: (p, 0, 0)),
                  pl.BlockSpec((hps, V_ROWS, LANES), lambda b, p, i: (p, 0, 0))],
        out_specs=pl.BlockSpec((1, tile, hps * D_V), lambda b, p, i: (b, i, p)),
        out_shape=jax.ShapeDtypeStruct((bsz, s_len, n_heads * D_V), _BF16),
        scratch_shapes=[pltpu.VMEM((hps, V_ROWS, tile), _F32), pltpu.VMEM((hps, 1, tile), _F32),
                        pltpu.VMEM((hps, LANES, tile), _F32)]
                       + [pltpu.VMEM((hps, tile, tile), _F32)] * 3,
        compiler_params=pltpu.CompilerParams(dimension_semantics=("arbitrary",) * 3,
                                             vmem_limit_bytes=VMEM_LIMIT_BYTES),
        name="prompt_attn",
    )(kbound, q, q, k, v, km, vm)


def _meta_attn_kernel(q_ref, k_ref, v_ref, o_ref):
    for pair in range(N_HEADS // 2):
        out = None
        for hd in (2 * pair, 2 * pair + 1):
            s = _dot_nt(q_ref[hd], k_ref[hd])
            p = jnp.exp2(s - jnp.max(s, axis=1, keepdims=True))
            o = _dot(p.astype(_BF16), v_ref[hd]) / jnp.sum(p, axis=1, keepdims=True)
            out = o if out is None else out + o
        o_ref[:, pair * LANES:(pair + 1) * LANES] = out.astype(o_ref.dtype)


def _meta_attn_call(q, k, v):
    n_rows = q.shape[1]
    return pl.pallas_call(
        _meta_attn_kernel, out_shape=jax.ShapeDtypeStruct((n_rows, N_HEADS * D_V), _BF16),
        name="meta_attn",
    )(q, k, v)


def _row_sums_t(x):
    ones = jnp.ones((8, x.shape[1]), _BF16)
    hi = x.astype(_BF16)
    lo = (x - hi.astype(_F32)).astype(_BF16)
    return (_dot_nt(ones, hi) + _dot_nt(ones, lo))[0:1]


def _sample_attn_kernel(q_ref, kn_ref, vn_ref, lat_ref, kpe_ref, wukt_ref, wukq_ref, wuv_ref, kgain_ref, place_ref,
                        o_ref, *, past):
    t_len = q_ref.shape[2]
    kgain = kgain_ref[...]
    qg = [(q_ref[0, hd].astype(_F32) * kgain).astype(_BF16) for hd in range(N_HEADS)]
    q_all = jnp.concatenate(qg, axis=0)
    q_lat = jnp.concatenate([_dot(qg[hd], wukq_ref[hd]) for hd in range(N_HEADS)], axis=0).astype(_BF16)

    def cached_scores(lo, hi):
        lat = lat_ref[0, 0, lo:hi].astype(_BF16)
        kpe = kpe_ref[0, lo:hi]
        s = _dot_nt(q_lat, lat) + _dot_nt(q_all, _perm(kpe, place_ref[...]).astype(_BF16))
        ss_rope = _row_sums_t(kpe * kpe)
        rows = []
        for pair in range(N_HEADS // 2):
            k_t = _dot_nt(wukt_ref[pair * LANES:(pair + 1) * LANES], lat)
            for hh in range(2):
                hd = 2 * pair + hh
                k_h = k_t[hh * D_NOPE:(hh + 1) * D_NOPE]
                r = lax.rsqrt((jnp.sum(k_h * k_h, axis=0, keepdims=True) + ss_rope) / D_QK + EPS)
                rows.append(s[hd * t_len:(hd + 1) * t_len] * r)
        return jnp.concatenate(rows, axis=0), lat

    s_meta, lat_meta = cached_scores(0, N_META)
    s_past, lat_past = cached_scores(N_META, N_META + past)
    tq = lax.broadcasted_iota(jnp.int32, (t_len, t_len), 0)
    tk = lax.broadcasted_iota(jnp.int32, (t_len, t_len), 1)
    new_mask = (past + tq) // CHUNK >= (past + tk) // CHUNK
    s_new = jnp.concatenate([jnp.where(new_mask, _dot_nt(q_ref[0, hd], kn_ref[0, hd]), MASKED)
                             for hd in range(N_HEADS)], axis=0)
    m = jnp.maximum(jnp.maximum(jnp.max(s_meta, axis=1, keepdims=True), jnp.max(s_past, axis=1, keepdims=True)),
                    jnp.max(s_new, axis=1, keepdims=True))
    p_meta, p_past, p_new = jnp.exp2(s_meta - m), jnp.exp2(s_past - m), jnp.exp2(s_new - m)
    den = (jnp.sum(p_meta, axis=1, keepdims=True) + jnp.sum(p_past, axis=1, keepdims=True)
           + jnp.sum(p_new, axis=1, keepdims=True))
    p_lat = (_dot(p_meta.astype(_BF16), lat_meta) + _dot(p_past.astype(_BF16), lat_past)).astype(_BF16)
    p_new = p_new.astype(_BF16)
    for pair in range(N_HEADS // 2):
        out = None
        for hd in (2 * pair, 2 * pair + 1):
            rows = slice(hd * t_len, (hd + 1) * t_len)
            o = (_dot(p_lat[rows], wuv_ref[:, hd * LANES:(hd + 1) * LANES]) + _dot(p_new[rows], vn_ref[0, hd])) / den[rows]
            out = o if out is None else out + o
        o_ref[0, :, pair * LANES:(pair + 1) * LANES] = out.astype(o_ref.dtype)


def _sample_attn_call(layer, q, k_new, v_new, cache_lat, cache_kpe, lw, consts):
    bsz, n_heads, t_len, _ = q.shape
    l_cache = cache_lat.shape[2]
    past = l_cache - N_META
    assert past % LANES == 0 and t_len % HIST_ROWS == 0
    head_spec = pl.BlockSpec((1, n_heads, t_len, LANES), lambda b: (b, 0, 0, 0))
    weights = (lw["w_uk_t"], lw["w_uk_q"], lw["w_uv"], lw["kgain"], consts["place"])
    return pl.pallas_call(
        functools.partial(_sample_attn_kernel, past=past), grid=(bsz,),
        in_specs=[head_spec, head_spec, head_spec,
                  pl.BlockSpec((1, 1, l_cache, KV_RANK), lambda b: (layer, b, 0, 0)),
                  pl.BlockSpec((1, l_cache, D_ROPE), lambda b: (b, 0, 0))]
                 + [_const_spec(w) for w in weights],
        out_specs=pl.BlockSpec((1, t_len, n_heads * D_V), lambda b: (b, 0, 0)),
        out_shape=jax.ShapeDtypeStruct((bsz, t_len, n_heads * D_V), _BF16),
        compiler_params=pltpu.CompilerParams(dimension_semantics=("arbitrary",),
                                             vmem_limit_bytes=VMEM_LIMIT_BYTES),
        name="sample_attn",
    )(q, k_new, v_new, cache_lat, cache_kpe, *weights)


def _pack_heads(w, width, alternate=False):
    kdim = w.shape[0]
    w3 = w.reshape(kdim, N_HEADS, width)
    low = jnp.pad(w3, ((0, 0), (0, 0), (0, LANES - width)))
    if alternate:
        high = jnp.pad(w3, ((0, 0), (0, 0), (LANES - width, 0)))
        odd = (jnp.arange(N_HEADS) % 2 == 1)[None, :, None]
        low = jnp.where(odd, high, low)
    return low.reshape(kdim, N_HEADS * LANES)


def _layer_weights(l, norm_mix, w_in, q_a_norm, w_uq, kv_a_norm, w_uk, w_uv, q_norm, k_norm,
                   w_pool, pool_scale, w_o, norm_ffn, w_gate, w_up, w_down):
    d_in = w_in.shape[2]
    d_in_pad = -(-d_in // (2 * LANES)) * (2 * LANES)
    pad96 = lambda g: jnp.pad(g.astype(_F32), (0, LANES - D_QK))[None, :]
    q_scale = D_QK ** -0.5 * math.log2(math.e)
    return dict(
        gmix=norm_mix[l][None, :].astype(_F32),
        w_in=jnp.pad(w_in[l], ((0, 0), (0, d_in_pad - d_in))).astype(_BF16),
        gq=q_a_norm[l][None, :].astype(_F32),
        w_uq=_pack_heads(w_uq[l], D_QK).astype(_BF16),
        w_uq_t=_pack_heads(w_uq[l], D_QK).astype(_BF16).T,
        gkv=kv_a_norm[l][None, :].astype(_F32),
        w_uk=_pack_heads(w_uk[l], D_NOPE).astype(_BF16),
        w_uk_t=w_uk[l].astype(_BF16).T,
        w_uk_q=jnp.pad(w_uk[l].astype(_BF16).T.reshape(N_HEADS, D_NOPE, KV_RANK),
                       ((0, 0), (0, LANES - D_NOPE), (0, 0))),
        w_uv=_pack_heads(w_uv[l], D_V, alternate=True).astype(_BF16),
        w_uv_t=w_uv[l].astype(_BF16).T,
        qgain=pad96(q_norm[l]) * q_scale,
        qgain_t=(pad96(q_norm[l]) * q_scale).T,
        kgain=pad96(k_norm[l]),
        w_pool=w_pool[l].astype(_BF16),
        pscale=pool_scale[l][None, :].astype(_F32),
        w_o_pool=w_o[l][:POOL_WIDTH].astype(_BF16),
        w_o_attn=w_o[l][POOL_WIDTH:].astype(_BF16),
        gffn=norm_ffn[l][None, :].astype(_F32),
        w_gate=w_gate[l].astype(_BF16),
        w_up=w_up[l].astype(_BF16),
        w_down=w_down[l].astype(_BF16),
    )


def _rope_tables(pos0, t_len):
    half = D_ROPE // 2
    inv = ROPE_THETA ** (-jnp.arange(half, dtype=_F32) / half)
    ang = (pos0 + jnp.arange(t_len, dtype=jnp.int32)).astype(_F32)[:, None] * inv[None, :]
    cos, sin = jnp.cos(ang), jnp.sin(ang)
    cos32 = jnp.concatenate([cos, cos], axis=1)
    sin32 = jnp.concatenate([-sin, sin], axis=1)
    ones = jnp.ones((t_len, D_NOPE), _F32)
    zeros = jnp.zeros((t_len, D_NOPE), _F32)
    tail1 = jnp.ones((t_len, LANES - D_QK), _F32)
    tail0 = jnp.zeros((t_len, LANES - D_QK), _F32)
    cosp = jnp.concatenate([ones, cos32, tail1], axis=1)
    sinp = jnp.concatenate([zeros, sin32, tail0], axis=1)
    return (cosp, sinp, cos32, sin32, cos.T, sin.T)


def _perm_consts():
    half = D_ROPE // 2
    idx = jnp.arange(D_ROPE)
    place = jnp.zeros((D_ROPE, LANES), _F32).at[idx, D_NOPE + idx].set(1.0)
    swap = jnp.zeros((D_ROPE, D_ROPE), _F32).at[idx, (idx + half) % D_ROPE].set(1.0)
    return dict(place=place.astype(_BF16), swap=swap.astype(_BF16))


def kernel(x_prompt, x_sample, cache_latent, cache_krope, state_pool, meta_tokens, norm_mix, w_in, q_a_norm, w_uq, kv_a_norm, w_uk, w_uv, q_norm, k_norm, w_pool, pool_scale, w_o, norm_ffn, w_gate, w_up, w_down):
    depth = norm_mix.shape[0]
    bp, s_len, d_model = x_prompt.shape
    bs, t_len, _ = x_sample.shape
    past = cache_latent.shape[2] - N_META
    consts = _perm_consts()
    tabs_meta = _rope_tables(0, N_META)
    tabs_prompt = _rope_tables(N_META, s_len)
    tabs_sample = _rope_tables(N_META + past, t_len)
    nb_sample = math.gcd(bs, 8)

    xm = meta_tokens.astype(_F32)[None]
    xp = x_prompt
    xs = x_sample
    zero_hist = jnp.zeros((1, HIST_ROWS, POOL_WIDTH), _F32)
    seq_bufs = None
    meta_lat, meta_kr, pool_p, lat_s, kpe_s, pool_s = [], [], [], [], [], []
    for l in range(depth):
        lw = _layer_weights(l, norm_mix, w_in, q_a_norm, w_uq, kv_a_norm, w_uk, w_uv, q_norm, k_norm,
                            w_pool, pool_scale, w_o, norm_ffn, w_gate, w_up, w_down)
        pm = _pre_call(xm, zero_hist, tabs_meta, lw, consts, nb=1, hist_shared=True, truncated=True)
        pp = _pre_call(xp, pm["ulast"], tabs_prompt, lw, consts, nb=1, hist_shared=True, truncated=False,
                       feature_major=True, into=(l, depth, seq_bufs))
        seq_bufs = (pp["lat"], pp["krope"])
        hist_s = jnp.pad(state_pool[l].astype(_F32), ((0, 0), (HIST_ROWS - POOL_STATE, 0), (0, 0)))
        ps = _pre_call(xs, hist_s, tabs_sample, lw, consts, nb=nb_sample, hist_shared=False, truncated=False)

        km = jnp.pad(pm["k"][0], ((0, 0), (0, LANES - N_META), (0, 0)))
        vm_t = jnp.stack([pm["v"][0, hd, :, (hd % 2) * D_V:(hd % 2 + 1) * D_V].T for hd in range(N_HEADS)])
        vm_t = jnp.concatenate([vm_t, jnp.ones((N_HEADS, V_ROWS - D_V, N_META), vm_t.dtype)], axis=1)
        vm_t = jnp.pad(vm_t, ((0, 0), (0, 0), (0, LANES - N_META)))
        kbound = (D_QK ** 0.5 * jnp.max(jnp.abs(lw["kgain"]))).reshape(1, 1)
        qbound = D_QK ** 0.5 * jnp.max(jnp.abs(lw["qgain"]))
        attn_args = (kbound, pp["q"], pp["k"], pp["v"], km, vm_t)
        attn_p = lax.cond(2.0 * qbound * kbound[0, 0] <= EXP2_SAFE_RANGE,
                          functools.partial(_prompt_attn_call, online=False),
                          functools.partial(_prompt_attn_call, online=True), *attn_args)
        attn_s = _sample_attn_call(l, ps["q"], ps["k"], ps["v"], cache_latent, cache_krope[l], lw, consts)
        xp = _post_call(xp, pp["pool"], attn_p, lw)
        xs = _post_call(xs, ps["pool"], attn_s, lw)
        if l + 1 < depth:
            attn_m = _meta_attn_call(pm["q"][0], pm["k"][0], pm["v"][0])
            xm = _post_call(xm, pm["pool"], attn_m[None], lw)

        meta_lat.append(pm["lat"][0])
        meta_kr.append(pm["krope"][0])
        pool_p.append(pp["ulast"][:, HIST_ROWS - POOL_STATE:])
        lat_s.append(ps["lat"])
        kpe_s.append(ps["krope"])
        pool_s.append(ps["ulast"][:, HIST_ROWS - POOL_STATE:])
    lat_p, kpe_p = _meta_rows_call(*seq_bufs, jnp.stack(meta_lat), jnp.stack(meta_kr))
    return (xp, xs, lat_p, kpe_p, jnp.stack(pool_p), jnp.stack(lat_s), jnp.stack(kpe_s), jnp.stack(pool_s))
```

```python
import functools
import math

import jax
import jax.numpy as jnp
from jax import lax
from jax.experimental import pallas as pl
from jax.experimental.pallas import tpu as pltpu

CHUNK = 64
N_META = 16
N_HEADS = 8
D_NOPE = 64
D_ROPE = 32
D_QK = D_NOPE + D_ROPE
D_V = 64
Q_RANK = 384
KV_RANK = 256
POOL_WINDOWS = (2, 4, 8, 16)
POOL_GROUP = 128
POOL_WIDTH = POOL_GROUP * len(POOL_WINDOWS)
POOL_STATE = max(POOL_WINDOWS) - 1
ROPE_THETA = 10000.0
EPS = 1e-6

LANES = 128
HIST_ROWS = 16
VMEM_LIMIT_BYTES = 56 * 1024 * 1024
ROW_TILE = 512
PRE_SPLITS = 2
ATTN_HEADS_PER_STEP = 4
V_ROWS = D_V + 16
MASKED = -1e30
EXP2_SAFE_RANGE = 120.0

_F32 = jnp.float32
_BF16 = jnp.bfloat16
_NT = (((1,), (1,)), ((), ()))


def _dot(a, b):
    return jnp.dot(a, b, preferred_element_type=_F32)


def _dot_nt(a, b):
    return lax.dot_general(a, b, _NT, preferred_element_type=_F32)


def _rms(x, g):
    return x * lax.rsqrt(jnp.mean(x * x, axis=-1, keepdims=True) + EPS) * g


def _perm(x, p):
    hi = x.astype(_BF16)
    lo = (x - hi.astype(_F32)).astype(_BF16)
    return _dot(hi, p) + _dot(lo, p)


def _head_norm(x, gain):
    ss = jnp.sum(x * x, axis=-1, keepdims=True)
    return x * lax.rsqrt(ss / D_QK + EPS) * gain


def _pre_kernel(x_ref, hist_ref, cosp_ref, sinp_ref, cos32_ref, sin32_ref, cost_ref, sint_ref,
                gmix_ref, win_ref, gq_ref, wuq_ref, gkv_ref, wuk_ref, wuv_ref, qgain_ref, kgain_ref,
                wpool_ref, pscale_ref, place_ref, swap_ref,
                q_ref, k_ref, v_ref, lat_ref, krope_ref, pool_ref, ulast_ref,
                ext_ref, *, nb, tm, hist_shared, truncated, feature_major, splits):
    t = pl.program_id(1)
    d_model = x_ref.shape[-1]
    for b in range(nb):
        @pl.when(t == 0)
        def _():
            ext_ref[b, 0:HIST_ROWS, :] = hist_ref[0 if hist_shared else b]

        @pl.when(t > 0)
        def _():
            ext_ref[b, 0:HIST_ROWS, :] = ext_ref[b, tm:tm + HIST_ROWS, :]

    def row_range(r0, rs):
        rows = nb * rs
        row_sl = slice(r0, r0 + rs)
        x = x_ref[:, row_sl, :].reshape(rows, d_model)
        h = _rms(x, gmix_ref[...]).astype(_BF16)
        z = _dot(h, win_ref[...])
        u = z[:, :POOL_WIDTH]
        q_lat = z[:, POOL_WIDTH:POOL_WIDTH + Q_RANK]
        kv_lat = z[:, POOL_WIDTH + Q_RANK:POOL_WIDTH + Q_RANK + KV_RANK]
        kpe = z[:, POOL_WIDTH + Q_RANK + KV_RANK:POOL_WIDTH + Q_RANK + KV_RANK + D_ROPE]

        first = HIST_ROWS + r0
        for b in range(nb):
            ext_ref[b, first:first + rs, :] = u[b * rs:(b + 1) * rs]
        for b in range(nb):
            cols = []
            for g, w in enumerate(POOL_WINDOWS):
                sl = slice(g * POOL_GROUP, (g + 1) * POOL_GROUP)
                cur = ext_ref[b, first:first + rs, sl]
                acc = cur
                for j in range(1, w):
                    acc = acc + ext_ref[b, first - j:first - j + rs, sl]
                if truncated:
                    row = t * tm + r0 + lax.broadcasted_iota(jnp.int32, (rs, 1), 0)
                    mean = acc / jnp.minimum(row + 1, w).astype(_F32)
                else:
                    mean = acc / float(w)
                cols.append(_dot((mean - cur).astype(_BF16), wpool_ref[g]))
            pool = jnp.concatenate(cols, axis=1) * pscale_ref[...]
            pool_ref[b, row_sl, :] = pool.astype(pool_ref.dtype)

        qn = _rms(q_lat, gq_ref[...]).astype(_BF16)
        c_kv = _rms(kv_lat, gkv_ref[...])
        into_results = len(lat_ref.shape) == 4
        lat_view = lat_ref.at[0] if into_results else lat_ref
        krope_view = krope_ref.at[0] if into_results else krope_ref
        lat_view[:, row_sl, :] = c_kv.reshape(nb, rs, KV_RANK)
        ckv_b = c_kv.astype(_BF16)
        knope = _dot(ckv_b, wuk_ref[...])

        def per_row(ref):
            tab = ref[row_sl, :]
            return tab if nb == 1 else jnp.concatenate([tab] * nb, axis=0)

        krope = kpe * per_row(cos32_ref) + _perm(kpe, swap_ref[...]) * per_row(sin32_ref)
        krope_view[:, row_sl, :] = krope.reshape(nb, rs, D_ROPE)
        kpe_placed = _perm(krope, place_ref[...])

        for hd in range(N_HEADS):
            kh = _head_norm(knope[:, hd * LANES:(hd + 1) * LANES] + kpe_placed, kgain_ref[...])
            k_ref[:, hd, row_sl, :] = kh.astype(k_ref.dtype).reshape(nb, rs, LANES)

        if feature_major:
            half = D_ROPE // 2
            qraw_t = _dot_nt(wuq_ref[...], qn)
            v_t = _dot_nt(wuv_ref[...], ckv_b)
            ones = jnp.ones((V_ROWS - D_V, rows), _F32)
            for hd in range(N_HEADS):
                v_h = jnp.concatenate([v_t[hd * D_V:(hd + 1) * D_V], ones], axis=0)
                v_ref[0, hd, 0, :, row_sl] = v_h.astype(v_ref.dtype)
            cos_t, sin_t, gain = cost_ref[:, row_sl], sint_ref[:, row_sl], qgain_ref[...]
            for hd in range(N_HEADS):
                base = hd * LANES
                nope = qraw_t[base:base + D_NOPE]
                x1 = qraw_t[base + D_NOPE:base + D_NOPE + half]
                x2 = qraw_t[base + D_NOPE + half:base + D_QK]
                r1 = x1 * cos_t - x2 * sin_t
                r2 = x1 * sin_t + x2 * cos_t
                ss = (jnp.sum(nope * nope, axis=0, keepdims=True) + jnp.sum(r1 * r1, axis=0, keepdims=True)
                      + jnp.sum(r2 * r2, axis=0, keepdims=True))
                scale = lax.rsqrt(ss / D_QK + EPS)
                qh = jnp.concatenate([nope * scale * gain[:D_NOPE], r1 * scale * gain[D_NOPE:D_NOPE + half],
                                      r2 * scale * gain[D_NOPE + half:D_QK],
                                      jnp.zeros((LANES - D_QK, rows), _F32)], axis=0)
                q_ref[0, hd, :, row_sl] = qh.astype(q_ref.dtype)
        else:
            cosp, sinp = per_row(cosp_ref), per_row(sinp_ref)
            qraw = _dot(qn, wuq_ref[...])
            vfull = _dot(ckv_b, wuv_ref[...])
            lane = lax.broadcasted_iota(jnp.int32, (1, LANES), 1)
            first_half = lane < D_NOPE + D_ROPE // 2
            for hd in range(N_HEADS):
                sl = slice(hd * LANES, (hd + 1) * LANES)
                qh = qraw[:, sl]
                partner = jnp.where(first_half, pltpu.roll(qh, LANES - D_ROPE // 2, 1),
                                    pltpu.roll(qh, D_ROPE // 2, 1))
                qh = _head_norm(qh * cosp + partner * sinp, qgain_ref[...])
                q_ref[:, hd, row_sl, :] = qh.astype(q_ref.dtype).reshape(nb, rs, LANES)
                v_ref[:, hd, row_sl, :] = vfull[:, sl].astype(v_ref.dtype).reshape(nb, rs, LANES)

    for sp in range(splits):
        row_range(sp * (tm // splits), tm // splits)
    for b in range(nb):
        ulast_ref[b] = ext_ref[b, tm:tm + HIST_ROWS, :]


def _const_spec(a):
    nd = a.ndim
    return pl.BlockSpec(a.shape, lambda *_: (0,) * nd, pipeline_mode=pl.Buffered(1))


def _skip_refs(fn, start, count):
    def wrapped(*refs, **kw):
        return fn(*refs[:start], *refs[start + count:], **kw)
    return wrapped


def _pre_call(x, hist, tabs, lw, consts, *, nb, hist_shared, truncated, feature_major=False, into=None):
    bsz, t_len, d_model = x.shape
    tm = t_len if nb > 1 else min(ROW_TILE, t_len)
    assert bsz % nb == 0 and t_len % tm == 0 and tm % HIST_ROWS == 0 and not (feature_major and nb > 1)
    grid = (bsz // nb, t_len // tm)
    fm = "_t" if feature_major else ""
    weights = (lw["gmix"], lw["w_in"], lw["gq"], lw["w_uq" + fm], lw["gkv"], lw["w_uk"], lw["w_uv" + fm],
               lw["qgain" + fm], lw["kgain"], lw["w_pool"], lw["pscale"], consts["place"], consts["swap"])
    hist_spec = (pl.BlockSpec((1, HIST_ROWS, POOL_WIDTH), lambda b, t: (0, 0, 0)) if hist_shared
                 else pl.BlockSpec((nb, HIST_ROWS, POOL_WIDTH), lambda b, t: (b, 0, 0)))
    in_specs = [pl.BlockSpec((nb, tm, d_model), lambda b, t: (b, t, 0)), hist_spec]
    in_specs += [pl.BlockSpec((tm, a.shape[1]), lambda b, t: (t, 0)) for a in tabs[:4]]
    in_specs += [pl.BlockSpec((a.shape[0], tm), lambda b, t: (0, t)) for a in tabs[4:]]
    in_specs += [_const_spec(w) for w in weights]
    head_spec = pl.BlockSpec((nb, N_HEADS, tm, LANES), lambda b, t: (b, 0, t, 0))
    head_shape = jax.ShapeDtypeStruct((bsz, N_HEADS, t_len, LANES), _BF16)
    if feature_major:
        q_spec = pl.BlockSpec((1, N_HEADS, LANES, tm), lambda b, t: (b, 0, 0, t))
        q_shape = jax.ShapeDtypeStruct((bsz, N_HEADS, LANES, t_len), _BF16)
        v_spec = pl.BlockSpec((1, N_HEADS, 1, V_ROWS, tm), lambda b, t: (b, 0, t, 0, 0))
        v_shape = jax.ShapeDtypeStruct((bsz, N_HEADS, t_len // tm, V_ROWS, tm), _BF16)
    else:
        q_spec, q_shape, v_spec, v_shape = head_spec, head_shape, head_spec, head_shape

    def row_out(width, dtype):
        return (pl.BlockSpec((nb, tm, width), lambda b, t: (b, t, 0)),
                jax.ShapeDtypeStruct((bsz, t_len, width), dtype))

    body = functools.partial(_pre_kernel, nb=nb, tm=tm, hist_shared=hist_shared, truncated=truncated,
                             feature_major=feature_major, splits=PRE_SPLITS if tm == ROW_TILE else 1)
    operands = [x, hist, *tabs, *weights]
    aliases = {}
    if into is None:
        lat_spec, lat_shape = row_out(KV_RANK, _F32)
        kr_spec, kr_shape = row_out(D_ROPE, _F32)
    else:
        layer, depth, bufs = into
        assert nb == 1

        def seq_out(width):
            block = tuple(pl.Element(n) for n in (1, 1, tm, width))
            return (pl.BlockSpec(block, lambda b, t: (layer, b, (1 + t * (tm // N_META)) * N_META, 0)),
                    jax.ShapeDtypeStruct((depth, bsz, N_META + t_len, width), _F32))

        lat_spec, lat_shape = seq_out(KV_RANK)
        kr_spec, kr_shape = seq_out(D_ROPE)
        body = _skip_refs(body, len(operands), len(bufs))
        aliases = {len(operands): 3, len(operands) + 1: 4}
        in_specs += [pl.BlockSpec(memory_space=pl.ANY)] * len(bufs)
        operands += list(bufs)
    pool_spec, pool_shape = row_out(POOL_WIDTH, _BF16)
    out_specs = [q_spec, head_spec, v_spec, lat_spec, kr_spec, pool_spec,
                 pl.BlockSpec((nb, HIST_ROWS, POOL_WIDTH), lambda b, t: (b, 0, 0))]
    out_shape = [q_shape, head_shape, v_shape, lat_shape, kr_shape, pool_shape,
                 jax.ShapeDtypeStruct((bsz, HIST_ROWS, POOL_WIDTH), _F32)]
    q, k, v, lat, krope, pool, ulast = pl.pallas_call(
        body, grid=grid, in_specs=in_specs, out_specs=out_specs, out_shape=out_shape,
        input_output_aliases=aliases,
        scratch_shapes=[pltpu.VMEM((nb, HIST_ROWS + tm, POOL_WIDTH), _F32)],
        compiler_params=pltpu.CompilerParams(dimension_semantics=("arbitrary", "arbitrary"),
                                             vmem_limit_bytes=VMEM_LIMIT_BYTES),
        name="pre_mixer",
    )(*operands)
    return dict(q=q, k=k, v=v, lat=lat, krope=krope, pool=pool, ulast=ulast)


def _meta_rows_kernel(lat_any, kr_any, mlat_ref, mkr_ref, lat_ref, kr_ref):
    del lat_any, kr_any
    lat_ref[0, 0] = mlat_ref[0]
    kr_ref[0, 0] = mkr_ref[0]


def _meta_rows_call(lat_all, kr_all, meta_lat, meta_kr):
    depth, bsz = lat_all.shape[:2]

    def src(width):
        return pl.BlockSpec((1, N_META, width), lambda l, b: (l, 0, 0))

    def dst(width):
        return pl.BlockSpec((1, 1, N_META, width), lambda l, b: (l, b, 0, 0))

    return pl.pallas_call(
        _meta_rows_kernel, grid=(depth, bsz),
        in_specs=[pl.BlockSpec(memory_space=pl.ANY)] * 2 + [src(KV_RANK), src(D_ROPE)],
        out_specs=[dst(KV_RANK), dst(D_ROPE)],
        out_shape=[jax.ShapeDtypeStruct(lat_all.shape, lat_all.dtype), jax.ShapeDtypeStruct(kr_all.shape, kr_all.dtype)],
        input_output_aliases={0: 0, 1: 1},
        compiler_params=pltpu.CompilerParams(dimension_semantics=("arbitrary", "arbitrary")),
        name="meta_rows",
    )(lat_all, kr_all, meta_lat, meta_kr)


def _post_kernel(x_ref, pool_ref, attn_ref, wop_ref, woa_ref, gffn_ref, wg_ref, wu_ref, wd_ref, o_ref):
    x1 = x_ref[...] + _dot(pool_ref[...], wop_ref[...]) + _dot(attn_ref[...], woa_ref[...])
    hn = _rms(x1, gffn_ref[...]).astype(_BF16)
    g = _dot(hn, wg_ref[...])
    up = _dot(hn, wu_ref[...])
    act = (g * jax.nn.sigmoid(g) * up).astype(_BF16)
    o_ref[...] = x1 + _dot(act, wd_ref[...])


def _post_call(x, pool, attn, lw):
    shape = x.shape
    d_model = shape[-1]
    x2 = x.reshape(-1, d_model)
    n = x2.shape[0]
    tm = min(ROW_TILE, n)
    assert n % tm == 0
    weights = (lw["w_o_pool"], lw["w_o_attn"], lw["gffn"], lw["w_gate"], lw["w_up"], lw["w_down"])

    def rows(width):
        return pl.BlockSpec((tm, width), lambda i: (i, 0))

    out = pl.pallas_call(
        _post_kernel, grid=(n // tm,),
        in_specs=[rows(d_model), rows(POOL_WIDTH), rows(N_HEADS * D_V)] + [_const_spec(w) for w in weights],
        out_specs=rows(d_model), out_shape=jax.ShapeDtypeStruct((n, d_model), _F32),
        compiler_params=pltpu.CompilerParams(dimension_semantics=("arbitrary",),
                                             vmem_limit_bytes=VMEM_LIMIT_BYTES),
        name="post_ffn",
    )(x2, pool.reshape(n, POOL_WIDTH), attn.reshape(n, N_HEADS * D_V), *weights)
    return out.reshape(shape)


def _attn_kernel(kbound_ref, q_ref, q_next_ref, k_ref, v_ref, km_ref, vm_ref, o_ref,
                 acc_ref, m_ref, s_meta_ref, s_first_ref, s_even_ref, s_odd_ref, *, tile, online):
    qi = pl.program_id(2)

    def scores(hh, k_rows, queries_ref=q_ref):
        return _dot(k_rows, queries_ref[0, hh])

    def consume(hh, s_t, v_cols, mask):
        if mask is not None:
            s_t = jnp.where(mask, s_t, MASKED)
        m_prev = m_ref[hh]
        m_next = jnp.maximum(m_prev, jnp.max(s_t, axis=0, keepdims=True)) if online else m_prev
        p_t = jnp.exp2(s_t - m_next).astype(_BF16)
        pv = _dot(v_cols, p_t)
        if online:
            acc_ref[hh] = jnp.exp2(m_prev - m_next) * acc_ref[hh] + pv
            m_ref[hh] = m_next
        else:
            acc_ref[hh] += pv

    def key_tile(hh, j):
        return k_ref[0, hh, pl.ds(pl.multiple_of(j * tile, tile), tile), :]

    def value_tile(hh, j):
        return v_ref[0, hh, j]

    heads = range(q_ref.shape[1])
    for hh in heads:
        acc_ref[hh] = jnp.zeros(acc_ref.shape[1:], _F32)
        if online:
            m_ref[hh] = jnp.full((1, tile), MASKED, _F32)
        else:
            qf = q_ref[0, hh].astype(_F32)
            m_ref[hh] = jnp.sqrt(jnp.sum(qf * qf, axis=0, keepdims=True)) * kbound_ref[...]
        s_meta_ref[hh] = scores(hh, km_ref[hh])

    @pl.when(qi == 0)
    def _():
        for hh in heads:
            s_even_ref[hh] = scores(hh, key_tile(hh, 0))

    def step(j, s_cur_ref, s_next_ref):
        for hh in heads:
            s_next_ref[hh] = scores(hh, key_tile(hh, j + 1))
            consume(hh, s_cur_ref[hh], value_tile(hh, j), None)

    @pl.when(qi >= 1)
    def _():
        step(0, s_first_ref, s_odd_ref)

    def two_steps(jj, carry):
        step(2 * jj + 1, s_odd_ref, s_even_ref)
        step(2 * jj + 2, s_even_ref, s_odd_ref)
        return carry

    lax.fori_loop(0, jnp.maximum(qi - 1, 0) // 2, two_steps, 0)

    @pl.when(jnp.logical_and(qi >= 2, qi % 2 == 0))
    def _():
        step(qi - 1, s_odd_ref, s_even_ref)

    key_chunk = lax.broadcasted_iota(jnp.int32, (tile, tile), 0) // CHUNK
    query_chunk = lax.broadcasted_iota(jnp.int32, (tile, tile), 1) // CHUNK
    meta_mask = lax.broadcasted_iota(jnp.int32, (km_ref.shape[1], tile), 0) < N_META
    mask = jnp.concatenate([query_chunk >= key_chunk, meta_mask], axis=0)
    for parity, s_fin_ref in enumerate((s_even_ref, s_odd_ref)):
        @pl.when(qi % 2 == parity)
        def _():
            for hh in heads:
                s_first_ref[hh] = scores(hh, key_tile(hh, 0), q_next_ref)
                s_t = jnp.concatenate([s_fin_ref[hh], s_meta_ref[hh]], axis=0)
                v_cols = jnp.concatenate([value_tile(hh, qi), vm_ref[hh]], axis=1)
                consume(hh, s_t, v_cols, mask)

    out_t = jnp.concatenate([acc_ref[hh, :D_V] / acc_ref[hh, D_V:D_V + 1] for hh in heads], axis=0)
    o_ref[0] = out_t.T.astype(o_ref.dtype)


def _prompt_attn_call(kbound, q, k, v, km, vm, *, online):
    bsz, n_heads, _, s_len = q.shape
    tile = v.shape[-1]
    n_tiles = s_len // tile
    hps = ATTN_HEADS_PER_STEP
    assert s_len % tile == 0 and tile % CHUNK == 0 and n_heads % hps == 0 and (hps * D_V) % LANES == 0
    grid = (bsz, n_heads // hps, n_tiles)
    return pl.pallas_call(
        functools.partial(_attn_kernel, tile=tile, online=online), grid=grid,
        in_specs=[pl.BlockSpec((1, 1), lambda b, p, i: (0, 0)),
                  pl.BlockSpec((1, hps, LANES, tile), lambda b, p, i: (b, p, 0, i)),
                  pl.BlockSpec((1, hps, LANES, tile), lambda b, p, i: (b, p, 0, jnp.minimum(i + 1, n_tiles - 1))),
                  pl.BlockSpec((1, hps, s_len, LANES), lambda b, p, i: (b, p, 0, 0)),
                  pl.BlockSpec((1, hps, n_tiles, V_ROWS, tile), lambda b, p, i: (b, p, 0, 0, 0)),
                  pl.BlockSpec((hps, LANES, LANES), lambda b, p, i: (p, 0, 0)),
                  pl.BlockSpec((hps, V_ROWS, LANES), lambda b, p, i: (p, 0, 0))],
        out_specs=pl.BlockSpec((1, tile, hps * D_V), lambda b, p, i: (b, i, p)),
        out_shape=jax.ShapeDtypeStruct((bsz, s_len, n_heads * D_V), _BF16),
        scratch_shapes=[pltpu.VMEM((hps, V_ROWS, tile), _F32), pltpu.VMEM((hps, 1, tile), _F32),
                        pltpu.VMEM((hps, LANES, tile), _F32)]
                       + [pltpu.VMEM((hps, tile, tile), _F32)] * 3,
        compiler_params=pltpu.CompilerParams(dimension_semantics=("arbitrary",) * 3,
                                             vmem_limit_bytes=VMEM_LIMIT_BYTES),
        name="prompt_attn",
    )(kbound, q, q, k, v, km, vm)


def _meta_attn_kernel(q_ref, k_ref, v_ref, o_ref):
    for pair in range(N_HEADS // 2):
        out = None
        for hd in (2 * pair, 2 * pair + 1):
            s = _dot_nt(q_ref[hd], k_ref[hd])
            p = jnp.exp2(s - jnp.max(s, axis=1, keepdims=True))
            o = _dot(p.astype(_BF16), v_ref[hd]) / jnp.sum(p, axis=1, keepdims=True)
            out = o if out is None else out + o
        o_ref[:, pair * LANES:(pair + 1) * LANES] = out.astype(o_ref.dtype)


def _meta_attn_call(q, k, v):
    n_rows = q.shape[1]
    return pl.pallas_call(
        _meta_attn_kernel, out_shape=jax.ShapeDtypeStruct((n_rows, N_HEADS * D_V), _BF16),
        name="meta_attn",
    )(q, k, v)


def _row_sums_t(x):
    ones = jnp.ones((8, x.shape[1]), _BF16)
    hi = x.astype(_BF16)
    lo = (x - hi.astype(_F32)).astype(_BF16)
    return (_dot_nt(ones, hi) + _dot_nt(ones, lo))[0:1]


def _sample_attn_kernel(q_ref, kn_ref, vn_ref, lat_ref, kpe_ref, wukt_ref, wukq_ref, wuv_ref, kgain_ref, place_ref,
                        o_ref, *, past):
    t_len = q_ref.shape[2]
    kgain = kgain_ref[...]
    qg = [(q_ref[0, hd].astype(_F32) * kgain).astype(_BF16) for hd in range(N_HEADS)]
    q_all = jnp.concatenate(qg, axis=0)
    q_lat = jnp.concatenate([_dot(qg[hd], wukq_ref[hd]) for hd in range(N_HEADS)], axis=0).astype(_BF16)

    def cached_scores(lo, hi):
        lat = lat_ref[0, 0, lo:hi].astype(_BF16)
        kpe = kpe_ref[0, lo:hi]
        s = _dot_nt(q_lat, lat) + _dot_nt(q_all, _perm(kpe, place_ref[...]).astype(_BF16))
        ss_rope = _row_sums_t(kpe * kpe)
        rows = []
        for pair in range(N_HEADS // 2):
            k_t = _dot_nt(wukt_ref[pair * LANES:(pair + 1) * LANES], lat)
            for hh in range(2):
                hd = 2 * pair + hh
                k_h = k_t[hh * D_NOPE:(hh + 1) * D_NOPE]
                r = lax.rsqrt((jnp.sum(k_h * k_h, axis=0, keepdims=True) + ss_rope) / D_QK + EPS)
                rows.append(s[hd * t_len:(hd + 1) * t_len] * r)
        return jnp.concatenate(rows, axis=0), lat

    s_meta, lat_meta = cached_scores(0, N_META)
    s_past, lat_past = cached_scores(N_META, N_META + past)
    tq = lax.broadcasted_iota(jnp.int32, (t_len, t_len), 0)
    tk = lax.broadcasted_iota(jnp.int32, (t_len, t_len), 1)
    new_mask = (past + tq) // CHUNK >= (past + tk) // CHUNK
    s_new = jnp.concatenate([jnp.where(new_mask, _dot_nt(q_ref[0, hd], kn_ref[0, hd]), MASKED)
                             for hd in range(N_HEADS)], axis=0)
    m = jnp.maximum(jnp.maximum(jnp.max(s_meta, axis=1, keepdims=True), jnp.max(s_past, axis=1, keepdims=True)),
                    jnp.max(s_new, axis=1, keepdims=True))
    p_meta, p_past, p_new = jnp.exp2(s_meta - m), jnp.exp2(s_past - m), jnp.exp2(s_new - m)
    den = (jnp.sum(p_meta, axis=1, keepdims=True) + jnp.sum(p_past, axis=1, keepdims=True)
           + jnp.sum(p_new, axis=1, keepdims=True))
    p_lat = (_dot(p_meta.astype(_BF16), lat_meta) + _dot(p_past.astype(_BF16), lat_past)).astype(_BF16)
    p_new = p_new.astype(_BF16)
    for pair in range(N_HEADS // 2):
        out = None
        for hd in (2 * pair, 2 * pair + 1):
            rows = slice(hd * t_len, (hd + 1) * t_len)
            o = (_dot(p_lat[rows], wuv_ref[:, hd * LANES:(hd + 1) * LANES]) + _dot(p_new[rows], vn_ref[0, hd])) / den[rows]
            out = o if out is None else out + o
        o_ref[0, :, pair * LANES:(pair + 1) * LANES] = out.astype(o_ref.dtype)


def _sample_attn_call(layer, q, k_new, v_new, cache_lat, cache_kpe, lw, consts):
    bsz, n_heads, t_len, _ = q.shape
    l_cache = cache_lat.shape[2]
    past = l_cache - N_META
    assert past % LANES == 0 and t_len % HIST_ROWS == 0
    head_spec = pl.BlockSpec((1, n_heads, t_len, LANES), lambda b: (b, 0, 0, 0))
    weights = (lw["w_uk_t"], lw["w_uk_q"], lw["w_uv"], lw["kgain"], consts["place"])
    return pl.pallas_call(
        functools.partial(_sample_attn_kernel, past=past), grid=(bsz,),
        in_specs=[head_spec, head_spec, head_spec,
                  pl.BlockSpec((1, 1, l_cache, KV_RANK), lambda b: (layer, b, 0, 0)),
                  pl.BlockSpec((1, l_cache, D_ROPE), lambda b: (b, 0, 0))]
                 + [_const_spec(w) for w in weights],
        out_specs=pl.BlockSpec((1, t_len, n_heads * D_V), lambda b: (b, 0, 0)),
        out_shape=jax.ShapeDtypeStruct((bsz, t_len, n_heads * D_V), _BF16),
        compiler_params=pltpu.CompilerParams(dimension_semantics=("arbitrary",),
                                             vmem_limit_bytes=VMEM_LIMIT_BYTES),
        name="sample_attn",
    )(q, k_new, v_new, cache_lat, cache_kpe, *weights)


def _pack_heads(w, width, alternate=False):
    kdim = w.shape[0]
    w3 = w.reshape(kdim, N_HEADS, width)
    low = jnp.pad(w3, ((0, 0), (0, 0), (0, LANES - width)))
    if alternate:
        high = jnp.pad(w3, ((0, 0), (0, 0), (LANES - width, 0)))
        odd = (jnp.arange(N_HEADS) % 2 == 1)[None, :, None]
        low = jnp.where(odd, high, low)
    return low.reshape(kdim, N_HEADS * LANES)


def _layer_weights(l, norm_mix, w_in, q_a_norm, w_uq, kv_a_norm, w_uk, w_uv, q_norm, k_norm,
                   w_pool, pool_scale, w_o, norm_ffn, w_gate, w_up, w_down):
    d_in = w_in.shape[2]
    d_in_pad = -(-d_in // (2 * LANES)) * (2 * LANES)
    pad96 = lambda g: jnp.pad(g.astype(_F32), (0, LANES - D_QK))[None, :]
    q_scale = D_QK ** -0.5 * math.log2(math.e)
    return dict(
        gmix=norm_mix[l][None, :].astype(_F32),
        w_in=jnp.pad(w_in[l], ((0, 0), (0, d_in_pad - d_in))).astype(_BF16),
        gq=q_a_norm[l][None, :].astype(_F32),
        w_uq=_pack_heads(w_uq[l], D_QK).astype(_BF16),
        w_uq_t=_pack_heads(w_uq[l], D_QK).astype(_BF16).T,
        gkv=kv_a_norm[l][None, :].astype(_F32),
        w_uk=_pack_heads(w_uk[l], D_NOPE).astype(_BF16),
        w_uk_t=w_uk[l].astype(_BF16).T,
        w_uk_q=jnp.pad(w_uk[l].astype(_BF16).T.reshape(N_HEADS, D_NOPE, KV_RANK),
                       ((0, 0), (0, LANES - D_NOPE), (0, 0))),
        w_uv=_pack_heads(w_uv[l], D_V, alternate=True).astype(_BF16),
        w_uv_t=w_uv[l].astype(_BF16).T,
        qgain=pad96(q_norm[l]) * q_scale,
        qgain_t=(pad96(q_norm[l]) * q_scale).T,
        kgain=pad96(k_norm[l]),
        w_pool=w_pool[l].astype(_BF16),
        pscale=pool_scale[l][None, :].astype(_F32),
        w_o_pool=w_o[l][:POOL_WIDTH].astype(_BF16),
        w_o_attn=w_o[l][POOL_WIDTH:].astype(_BF16),
        gffn=norm_ffn[l][None, :].astype(_F32),
        w_gate=w_gate[l].astype(_BF16),
        w_up=w_up[l].astype(_BF16),
        w_down=w_down[l].astype(_BF16),
    )


def _rope_tables(pos0, t_len):
    half = D_ROPE // 2
    inv = ROPE_THETA ** (-jnp.arange(half, dtype=_F32) / half)
    ang = (pos0 + jnp.arange(t_len, dtype=jnp.int32)).astype(_F32)[:, None] * inv[None, :]
    cos, sin = jnp.cos(ang), jnp.sin(ang)
    cos32 = jnp.concatenate([cos, cos], axis=1)
    sin32 = jnp.concatenate([-sin, sin], axis=1)
    ones = jnp.ones((t_len, D_NOPE), _F32)
    zeros = jnp.zeros((t_len, D_NOPE), _F32)
    tail1 = jnp.ones((t_len, LANES - D_QK), _F32)
    tail0 = jnp.zeros((t_len, LANES - D_QK), _F32)
    cosp = jnp.concatenate([ones, cos32, tail1], axis=1)
    sinp = jnp.concatenate([zeros, sin32, tail0], axis=1)
    return (cosp, sinp, cos32, sin32, cos.T, sin.T)


def _perm_consts():
    half = D_ROPE // 2
    idx = jnp.arange(D_ROPE)
    place = jnp.zeros((D_ROPE, LANES), _F32).at[idx, D_NOPE + idx].set(1.0)
    swap = jnp.zeros((D_ROPE, D_ROPE), _F32).at[idx, (idx + half) % D_ROPE].set(1.0)
    return dict(place=place.astype(_BF16), swap=swap.astype(_BF16))


def kernel(x_prompt, x_sample, cache_latent, cache_krope, state_pool, meta_tokens, norm_mix, w_in, q_a_norm, w_uq, kv_a_norm, w_uk, w_uv, q_norm, k_norm, w_pool, pool_scale, w_o, norm_ffn, w_gate, w_up, w_down):
    depth = norm_mix.shape[0]
    bp, s_len, d_model = x_prompt.shape
    bs, t_len, _ = x_sample.shape
    past = cache_latent.shape[2] - N_META
    consts = _perm_consts()
    tabs_meta = _rope_tables(0, N_META)
    tabs_prompt = _rope_tables(N_META, s_len)
    tabs_sample = _rope_tables(N_META + past, t_len)
    nb_sample = math.gcd(bs, 8)

    xm = meta_tokens.astype(_F32)[None]
    xp = x_prompt
    xs = x_sample
    zero_hist = jnp.zeros((1, HIST_ROWS, POOL_WIDTH), _F32)
    seq_bufs = tuple(jnp.zeros((depth, bp, N_META + s_len, width), _F32) for width in (KV_RANK, D_ROPE))
    meta_lat, meta_kr, pool_p, lat_s, kpe_s, pool_s = [], [], [], [], [], []
    for l in range(depth):
        lw = _layer_weights(l, norm_mix, w_in, q_a_norm, w_uq, kv_a_norm, w_uk, w_uv, q_norm, k_norm,
                            w_pool, pool_scale, w_o, norm_ffn, w_gate, w_up, w_down)
        pm = _pre_call(xm, zero_hist, tabs_meta, lw, consts, nb=1, hist_shared=True, truncated=True)
        pp = _pre_call(xp, pm["ulast"], tabs_prompt, lw, consts, nb=1, hist_shared=True, truncated=False,
                       feature_major=True, into=(l, depth, seq_bufs))
        seq_bufs = (pp["lat"], pp["krope"])
        hist_s = jnp.pad(state_pool[l].astype(_F32), ((0, 0), (HIST_ROWS - POOL_STATE, 0), (0, 0)))
        ps = _pre_call(xs, hist_s, tabs_sample, lw, consts, nb=nb_sample, hist_shared=False, truncated=False)

        km = jnp.pad(pm["k"][0], ((0, 0), (0, LANES - N_META), (0, 0)))
        vm_t = jnp.stack([pm["v"][0, hd, :, (hd % 2) * D_V:(hd % 2 + 1) * D_V].T for hd in range(N_HEADS)])
        vm_t = jnp.concatenate([vm_t, jnp.ones((N_HEADS, V_ROWS - D_V, N_META), vm_t.dtype)], axis=1)
        vm_t = jnp.pad(vm_t, ((0, 0), (0, 0), (0, LANES - N_META)))
        kbound = (D_QK ** 0.5 * jnp.max(jnp.abs(lw["kgain"]))).reshape(1, 1)
        qbound = D_QK ** 0.5 * jnp.max(jnp.abs(lw["qgain"]))
        attn_args = (kbound, pp["q"], pp["k"], pp["v"], km, vm_t)
        attn_p = lax.cond(2.0 * qbound * kbound[0, 0] <= EXP2_SAFE_RANGE,
                          functools.partial(_prompt_attn_call, online=False),
                          functools.partial(_prompt_attn_call, online=True), *attn_args)
        attn_s = _sample_attn_call(l, ps["q"], ps["k"], ps["v"], cache_latent, cache_krope[l], lw, consts)
        xp = _post_call(xp, pp["pool"], attn_p, lw)
        xs = _post_call(xs, ps["pool"], attn_s, lw)
        if l + 1 < depth:
            attn_m = _meta_attn_call(pm["q"][0], pm["k"][0], pm["v"][0])
            xm = _post_call(xm, pm["pool"], attn_m[None], lw)

        meta_lat.append(pm["lat"][0])
        meta_kr.append(pm["krope"][0])
        pool_p.append(pp["ulast"][:, HIST_ROWS - POOL_STATE:])
        lat_s.append(ps["lat"])
        kpe_s.append(ps["krope"])
        pool_s.append(ps["ulast"][:, HIST_ROWS - POOL_STATE:])
    lat_p, kpe_p = _meta_rows_call(*seq_bufs, jnp.stack(meta_lat), jnp.stack(meta_kr))
    return (xp, xs, lat_p, kpe_p, jnp.stack(pool_p), jnp.stack(lat_s), jnp.stack(kpe_s), jnp.stack(pool_s))
```

```python
import functools
import math

import jax
import jax.numpy as jnp
from jax import lax
from jax.experimental import pallas as pl
from jax.experimental.pallas import tpu as pltpu

CHUNK = 64
N_META = 16
N_HEADS = 8
D_NOPE = 64
D_ROPE = 32
D_QK = D_NOPE + D_ROPE
D_V = 64
Q_RANK = 384
KV_RANK = 256
POOL_WINDOWS = (2, 4, 8, 16)
POOL_GROUP = 128
POOL_WIDTH = POOL_GROUP * len(POOL_WINDOWS)
POOL_STATE = max(POOL_WINDOWS) - 1
ROPE_THETA = 10000.0
EPS = 1e-6

LANES = 128
HIST_ROWS = 16
VMEM_LIMIT_BYTES = 56 * 1024 * 1024
ROW_TILE = 512
PRE_SPLITS = 2
ATTN_HEADS_PER_STEP = 4
V_ROWS = D_V + 16
MASKED = -1e30
EXP2_SAFE_RANGE = 120.0

_F32 = jnp.float32
_BF16 = jnp.bfloat16
_NT = (((1,), (1,)), ((), ()))


def _dot(a, b):
    return jnp.dot(a, b, preferred_element_type=_F32)


def _dot_nt(a, b):
    return lax.dot_general(a, b, _NT, preferred_element_type=_F32)


def _rms(x, g):
    return x * lax.rsqrt(jnp.mean(x * x, axis=-1, keepdims=True) + EPS) * g


def _perm(x, p):
    hi = x.astype(_BF16)
    lo = (x - hi.astype(_F32)).astype(_BF16)
    return _dot(hi, p) + _dot(lo, p)


def _head_norm(x, gain):
    ss = jnp.sum(x * x, axis=-1, keepdims=True)
    return x * lax.rsqrt(ss / D_QK + EPS) * gain


def _pre_kernel(x_ref, hist_ref, cosp_ref, sinp_ref, cos32_ref, sin32_ref, cost_ref, sint_ref,
                gmix_ref, win_ref, gq_ref, wuq_ref, gkv_ref, wuk_ref, wuv_ref, qgain_ref, kgain_ref,
                wpool_ref, pscale_ref, place_ref, swap_ref,
                q_ref, k_ref, v_ref, lat_ref, krope_ref, pool_ref, ulast_ref,
                ext_ref, *, nb, tm, hist_shared, truncated, feature_major, splits):
    t = pl.program_id(1)
    d_model = x_ref.shape[-1]
    for b in range(nb):
        @pl.when(t == 0)
        def _():
            ext_ref[b, 0:HIST_ROWS, :] = hist_ref[0 if hist_shared else b]

        @pl.when(t > 0)
        def _():
            ext_ref[b, 0:HIST_ROWS, :] = ext_ref[b, tm:tm + HIST_ROWS, :]

    def row_range(r0, rs):
        rows = nb * rs
        row_sl = slice(r0, r0 + rs)
        x = x_ref[:, row_sl, :].reshape(rows, d_model)
        h = _rms(x, gmix_ref[...]).astype(_BF16)
        z = _dot(h, win_ref[...])
        u = z[:, :POOL_WIDTH]
        q_lat = z[:, POOL_WIDTH:POOL_WIDTH + Q_RANK]
        kv_lat = z[:, POOL_WIDTH + Q_RANK:POOL_WIDTH + Q_RANK + KV_RANK]
        kpe = z[:, POOL_WIDTH + Q_RANK + KV_RANK:POOL_WIDTH + Q_RANK + KV_RANK + D_ROPE]

        first = HIST_ROWS + r0
        for b in range(nb):
            ext_ref[b, first:first + rs, :] = u[b * rs:(b + 1) * rs]
        for b in range(nb):
            cols = []
            for g, w in enumerate(POOL_WINDOWS):
                sl = slice(g * POOL_GROUP, (g + 1) * POOL_GROUP)
                cur = ext_ref[b, first:first + rs, sl]
                acc = cur
                for j in range(1, w):
                    acc = acc + ext_ref[b, first - j:first - j + rs, sl]
                if truncated:
                    row = t * tm + r0 + lax.broadcasted_iota(jnp.int32, (rs, 1), 0)
                    mean = acc / jnp.minimum(row + 1, w).astype(_F32)
                else:
                    mean = acc / float(w)
                cols.append(_dot((mean - cur).astype(_BF16), wpool_ref[g]))
            pool = jnp.concatenate(cols, axis=1) * pscale_ref[...]
            pool_ref[b, row_sl, :] = pool.astype(pool_ref.dtype)

        qn = _rms(q_lat, gq_ref[...]).astype(_BF16)
        c_kv = _rms(kv_lat, gkv_ref[...])
        into_results = len(lat_ref.shape) == 4
        lat_view = lat_ref.at[0] if into_results else lat_ref
        krope_view = krope_ref.at[0] if into_results else krope_ref
        lat_view[:, row_sl, :] = c_kv.reshape(nb, rs, KV_RANK)
        ckv_b = c_kv.astype(_BF16)
        knope = _dot(ckv_b, wuk_ref[...])

        def per_row(ref):
            tab = ref[row_sl, :]
            return tab if nb == 1 else jnp.concatenate([tab] * nb, axis=0)

        krope = kpe * per_row(cos32_ref) + _perm(kpe, swap_ref[...]) * per_row(sin32_ref)
        krope_view[:, row_sl, :] = krope.reshape(nb, rs, D_ROPE)
        kpe_placed = _perm(krope, place_ref[...])

        for hd in range(N_HEADS):
            kh = _head_norm(knope[:, hd * LANES:(hd + 1) * LANES] + kpe_placed, kgain_ref[...])
            k_ref[:, hd, row_sl, :] = kh.astype(k_ref.dtype).reshape(nb, rs, LANES)

        if feature_major:
            half = D_ROPE // 2
            qraw_t = _dot_nt(wuq_ref[...], qn)
            v_t = _dot_nt(wuv_ref[...], ckv_b)
            ones = jnp.ones((V_ROWS - D_V, rows), _F32)
            for hd in range(N_HEADS):
                v_h = jnp.concatenate([v_t[hd * D_V:(hd + 1) * D_V], ones], axis=0)
                v_ref[0, hd, 0, :, row_sl] = v_h.astype(v_ref.dtype)
            cos_t, sin_t, gain = cost_ref[:, row_sl], sint_ref[:, row_sl], qgain_ref[...]
            for hd in range(N_HEADS):
                base = hd * LANES
                nope = qraw_t[base:base + D_NOPE]
                x1 = qraw_t[base + D_NOPE:base + D_NOPE + half]
                x2 = qraw_t[base + D_NOPE + half:base + D_QK]
                r1 = x1 * cos_t - x2 * sin_t
                r2 = x1 * sin_t + x2 * cos_t
                ss = (jnp.sum(nope * nope, axis=0, keepdims=True) + jnp.sum(r1 * r1, axis=0, keepdims=True)
                      + jnp.sum(r2 * r2, axis=0, keepdims=True))
                scale = lax.rsqrt(ss / D_QK + EPS)
                qh = jnp.concatenate([nope * scale * gain[:D_NOPE], r1 * scale * gain[D_NOPE:D_NOPE + half],
                                      r2 * scale * gain[D_NOPE + half:D_QK],
                                      jnp.zeros((LANES - D_QK, rows), _F32)], axis=0)
                q_ref[0, hd, :, row_sl] = qh.astype(q_ref.dtype)
        else:
            cosp, sinp = per_row(cosp_ref), per_row(sinp_ref)
            qraw = _dot(qn, wuq_ref[...])
            vfull = _dot(ckv_b, wuv_ref[...])
            lane = lax.broadcasted_iota(jnp.int32, (1, LANES), 1)
            first_half = lane < D_NOPE + D_ROPE // 2
            for hd in range(N_HEADS):
                sl = slice(hd * LANES, (hd + 1) * LANES)
                qh = qraw[:, sl]
                partner = jnp.where(first_half, pltpu.roll(qh, LANES - D_ROPE // 2, 1),
                                    pltpu.roll(qh, D_ROPE // 2, 1))
                qh = _head_norm(qh * cosp + partner * sinp, qgain_ref[...])
                q_ref[:, hd, row_sl, :] = qh.astype(q_ref.dtype).reshape(nb, rs, LANES)
                v_ref[:, hd, row_sl, :] = vfull[:, sl].astype(v_ref.dtype).reshape(nb, rs, LANES)

    for sp in range(splits):
        row_range(sp * (tm // splits), tm // splits)
    for b in range(nb):
        ulast_ref[b] = ext_ref[b, tm:tm + HIST_ROWS, :]


def _const_spec(a):
    nd = a.ndim
    return pl.BlockSpec(a.shape, lambda *_: (0,) * nd, pipeline_mode=pl.Buffered(1))


def _skip_refs(fn, start, count):
    def wrapped(*refs, **kw):
        return fn(*refs[:start], *refs[start + count:], **kw)
    return wrapped


def _pre_call(x, hist, tabs, lw, consts, *, nb, hist_shared, truncated, feature_major=False, into=None):
    bsz, t_len, d_model = x.shape
    tm = t_len if nb > 1 else min(ROW_TILE, t_len)
    assert bsz % nb == 0 and t_len % tm == 0 and tm % HIST_ROWS == 0 and not (feature_major and nb > 1)
    grid = (bsz // nb, t_len // tm)
    fm = "_t" if feature_major else ""
    weights = (lw["gmix"], lw["w_in"], lw["gq"], lw["w_uq" + fm], lw["gkv"], lw["w_uk"], lw["w_uv" + fm],
               lw["qgain" + fm], lw["kgain"], lw["w_pool"], lw["pscale"], consts["place"], consts["swap"])
    hist_spec = (pl.BlockSpec((1, HIST_ROWS, POOL_WIDTH), lambda b, t: (0, 0, 0)) if hist_shared
                 else pl.BlockSpec((nb, HIST_ROWS, POOL_WIDTH), lambda b, t: (b, 0, 0)))
    in_specs = [pl.BlockSpec((nb, tm, d_model), lambda b, t: (b, t, 0)), hist_spec]
    in_specs += [pl.BlockSpec((tm, a.shape[1]), lambda b, t: (t, 0)) for a in tabs[:4]]
    in_specs += [pl.BlockSpec((a.shape[0], tm), lambda b, t: (0, t)) for a in tabs[4:]]
    in_specs += [_const_spec(w) for w in weights]
    head_spec = pl.BlockSpec((nb, N_HEADS, tm, LANES), lambda b, t: (b, 0, t, 0))
    head_shape = jax.ShapeDtypeStruct((bsz, N_HEADS, t_len, LANES), _BF16)
    if feature_major:
        q_spec = pl.BlockSpec((1, N_HEADS, LANES, tm), lambda b, t: (b, 0, 0, t))
        q_shape = jax.ShapeDtypeStruct((bsz, N_HEADS, LANES, t_len), _BF16)
        v_spec = pl.BlockSpec((1, N_HEADS, 1, V_ROWS, tm), lambda b, t: (b, 0, t, 0, 0))
        v_shape = jax.ShapeDtypeStruct((bsz, N_HEADS, t_len // tm, V_ROWS, tm), _BF16)
    else:
        q_spec, q_shape, v_spec, v_shape = head_spec, head_shape, head_spec, head_shape

    def row_out(width, dtype):
        return (pl.BlockSpec((nb, tm, width), lambda b, t: (b, t, 0)),
                jax.ShapeDtypeStruct((bsz, t_len, width), dtype))

    body = functools.partial(_pre_kernel, nb=nb, tm=tm, hist_shared=hist_shared, truncated=truncated,
                             feature_major=feature_major, splits=PRE_SPLITS if tm == ROW_TILE else 1)
    operands = [x, hist, *tabs, *weights]
    aliases = {}
    if into is None:
        lat_spec, lat_shape = row_out(KV_RANK, _F32)
        kr_spec, kr_shape = row_out(D_ROPE, _F32)
    else:
        layer, depth, bufs = into
        assert nb == 1

        def seq_out(width):
            block = tuple(pl.Element(n) for n in (1, 1, tm, width))
            return (pl.BlockSpec(block, lambda b, t: (layer, b, (1 + t * (tm // N_META)) * N_META, 0)),
                    jax.ShapeDtypeStruct((depth, bsz, N_META + t_len, width), _F32))

        lat_spec, lat_shape = seq_out(KV_RANK)
        kr_spec, kr_shape = seq_out(D_ROPE)
        body = _skip_refs(body, len(operands), len(bufs))
        aliases = {len(operands): 3, len(operands) + 1: 4}
        in_specs += [pl.BlockSpec(memory_space=pl.ANY)] * len(bufs)
        operands += list(bufs)
    pool_spec, pool_shape = row_out(POOL_WIDTH, _BF16)
    out_specs = [q_spec, head_spec, v_spec, lat_spec, kr_spec, pool_spec,
                 pl.BlockSpec((nb, HIST_ROWS, POOL_WIDTH), lambda b, t: (b, 0, 0))]
    out_shape = [q_shape, head_shape, v_shape, lat_shape, kr_shape, pool_shape,
                 jax.ShapeDtypeStruct((bsz, HIST_ROWS, POOL_WIDTH), _F32)]
    q, k, v, lat, krope, pool, ulast = pl.pallas_call(
        body, grid=grid, in_specs=in_specs, out_specs=out_specs, out_shape=out_shape,
        input_output_aliases=aliases,
        scratch_shapes=[pltpu.VMEM((nb, HIST_ROWS + tm, POOL_WIDTH), _F32)],
        compiler_params=pltpu.CompilerParams(dimension_semantics=("arbitrary", "arbitrary"),
                                             vmem_limit_bytes=VMEM_LIMIT_BYTES),
        name="pre_mixer",
    )(*operands)
    return dict(q=q, k=k, v=v, lat=lat, krope=krope, pool=pool, ulast=ulast)


def _meta_rows_kernel(lat_any, kr_any, mlat_ref, mkr_ref, lat_ref, kr_ref):
    del lat_any, kr_any
    lat_ref[0, 0] = mlat_ref[0]
    kr_ref[0, 0] = mkr_ref[0]


def _meta_rows_call(lat_all, kr_all, meta_lat, meta_kr):
    depth, bsz = lat_all.shape[:2]

    def src(width):
        return pl.BlockSpec((1, N_META, width), lambda l, b: (l, 0, 0))

    def dst(width):
        return pl.BlockSpec((1, 1, N_META, width), lambda l, b: (l, b, 0, 0))

    return pl.pallas_call(
        _meta_rows_kernel, grid=(depth, bsz),
        in_specs=[pl.BlockSpec(memory_space=pl.ANY)] * 2 + [src(KV_RANK), src(D_ROPE)],
        out_specs=[dst(KV_RANK), dst(D_ROPE)],
        out_shape=[jax.ShapeDtypeStruct(lat_all.shape, lat_all.dtype), jax.ShapeDtypeStruct(kr_all.shape, kr_all.dtype)],
        input_output_aliases={0: 0, 1: 1},
        compiler_params=pltpu.CompilerParams(dimension_semantics=("arbitrary", "arbitrary")),
        name="meta_rows",
    )(lat_all, kr_all, meta_lat, meta_kr)


def _post_kernel(x_ref, pool_ref, attn_ref, wop_ref, woa_ref, gffn_ref, wg_ref, wu_ref, wd_ref, o_ref):
    x1 = x_ref[...] + _dot(pool_ref[...], wop_ref[...]) + _dot(attn_ref[...], woa_ref[...])
    hn = _rms(x1, gffn_ref[...]).astype(_BF16)
    g = _dot(hn, wg_ref[...])
    up = _dot(hn, wu_ref[...])
    act = (g * jax.nn.sigmoid(g) * up).astype(_BF16)
    o_ref[...] = x1 + _dot(act, wd_ref[...])


def _post_call(x, pool, attn, lw):
    shape = x.shape
    d_model = shape[-1]
    x2 = x.reshape(-1, d_model)
    n = x2.shape[0]
    tm = min(ROW_TILE, n)
    assert n % tm == 0
    weights = (lw["w_o_pool"], lw["w_o_attn"], lw["gffn"], lw["w_gate"], lw["w_up"], lw["w_down"])

    def rows(width):
        return pl.BlockSpec((tm, width), lambda i: (i, 0))

    out = pl.pallas_call(
        _post_kernel, grid=(n // tm,),
        in_specs=[rows(d_model), rows(POOL_WIDTH), rows(N_HEADS * D_V)] + [_const_spec(w) for w in weights],
        out_specs=rows(d_model), out_shape=jax.ShapeDtypeStruct((n, d_model), _F32),
        compiler_params=pltpu.CompilerParams(dimension_semantics=("arbitrary",),
                                             vmem_limit_bytes=VMEM_LIMIT_BYTES),
        name="post_ffn",
    )(x2, pool.reshape(n, POOL_WIDTH), attn.reshape(n, N_HEADS * D_V), *weights)
    return out.reshape(shape)


def _attn_kernel(kbound_ref, q_ref, q_next_ref, k_ref, v_ref, km_ref, vm_ref, o_ref,
                 acc_ref, m_ref, s_meta_ref, s_first_ref, s_even_ref, s_odd_ref, *, tile, online):
    qi = pl.program_id(2)

    def scores(hh, k_rows, queries_ref=q_ref):
        return _dot(k_rows, queries_ref[0, hh])

    def consume(hh, s_t, v_cols, mask):
        if mask is not None:
            s_t = jnp.where(mask, s_t, MASKED)
        m_prev = m_ref[hh]
        m_next = jnp.maximum(m_prev, jnp.max(s_t, axis=0, keepdims=True)) if online else m_prev
        p_t = jnp.exp2(s_t - m_next).astype(_BF16)
        pv = _dot(v_cols, p_t)
        if online:
            acc_ref[hh] = jnp.exp2(m_prev - m_next) * acc_ref[hh] + pv
            m_ref[hh] = m_next
        else:
            acc_ref[hh] += pv

    def key_tile(hh, j):
        return k_ref[0, hh, pl.ds(pl.multiple_of(j * tile, tile), tile), :]

    def value_tile(hh, j):
        return v_ref[0, hh, j]

    heads = range(q_ref.shape[1])
    for hh in heads:
        acc_ref[hh] = jnp.zeros(acc_ref.shape[1:], _F32)
        if online:
            m_ref[hh] = jnp.full((1, tile), MASKED, _F32)
        else:
            qf = q_ref[0, hh].astype(_F32)
            m_ref[hh] = jnp.sqrt(jnp.sum(qf * qf, axis=0, keepdims=True)) * kbound_ref[...]
        s_meta_ref[hh] = scores(hh, km_ref[hh])

    @pl.when(qi == 0)
    def _():
        for hh in heads:
            s_even_ref[hh] = scores(hh, key_tile(hh, 0))

    def step(j, s_cur_ref, s_next_ref):
        for hh in heads:
            s_next_ref[hh] = scores(hh, key_tile(hh, j + 1))
            consume(hh, s_cur_ref[hh], value_tile(hh, j), None)

    @pl.when(qi >= 1)
    def _():
        step(0, s_first_ref, s_odd_ref)

    def two_steps(j):
        step(j, s_odd_ref, s_even_ref)
        step(j + 1, s_even_ref, s_odd_ref)

    def four_steps(jj, carry):
        two_steps(4 * jj + 1)
        two_steps(4 * jj + 3)
        return carry

    n_full = jnp.maximum(qi - 1, 0)
    rest = n_full % 4
    lax.fori_loop(0, n_full // 4, four_steps, 0)

    @pl.when(rest >= 2)
    def _():
        two_steps(n_full - rest + 1)

    @pl.when(rest % 2 == 1)
    def _():
        step(n_full, s_odd_ref, s_even_ref)

    key_chunk = lax.broadcasted_iota(jnp.int32, (tile, tile), 0) // CHUNK
    query_chunk = lax.broadcasted_iota(jnp.int32, (tile, tile), 1) // CHUNK
    meta_mask = lax.broadcasted_iota(jnp.int32, (km_ref.shape[1], tile), 0) < N_META
    mask = jnp.concatenate([query_chunk >= key_chunk, meta_mask], axis=0)
    for parity, s_fin_ref in enumerate((s_even_ref, s_odd_ref)):
        @pl.when(qi % 2 == parity)
        def _():
            for hh in heads:
                s_first_ref[hh] = scores(hh, key_tile(hh, 0), q_next_ref)
                s_t = jnp.concatenate([s_fin_ref[hh], s_meta_ref[hh]], axis=0)
                v_cols = jnp.concatenate([value_tile(hh, qi), vm_ref[hh]], axis=1)
                consume(hh, s_t, v_cols, mask)

    out_t = jnp.concatenate([acc_ref[hh, :D_V] / acc_ref[hh, D_V:D_V + 1] for hh in heads], axis=0)
    o_ref[0] = out_t.T.astype(o_ref.dtype)


def _prompt_attn_call(kbound, q, k, v, km, vm, *, online):
    bsz, n_heads, _, s_len = q.shape
    tile = v.shape[-1]
    n_tiles = s_len // tile
    hps = ATTN_HEADS_PER_STEP
    assert s_len % tile == 0 and tile % CHUNK == 0 and n_heads % hps == 0 and (hps * D_V) % LANES == 0
    grid = (bsz, n_heads // hps, n_tiles)
    return pl.pallas_call(
        functools.partial(_attn_kernel, tile=tile, online=online), grid=grid,
        in_specs=[pl.BlockSpec((1, 1), lambda b, p, i: (0, 0)),
                  pl.BlockSpec((1, hps, LANES, tile), lambda b, p, i: (b, p, 0, i)),
                  pl.BlockSpec((1, hps, LANES, tile), lambda b, p, i: (b, p, 0, jnp.minimum(i + 1, n_tiles - 1))),
                  pl.BlockSpec((1, hps, s_len, LANES), lambda b, p, i: (b, p, 0, 0)),
                  pl.BlockSpec((1, hps, n_tiles, V_ROWS, tile), lambda b, p, i: (b, p, 0, 0, 0)),
                  pl.BlockSpec((hps, LANES, LANES), lambda b, p, i: (p, 0, 0)),
                  pl.BlockSpec((hps, V_ROWS, LANES), lambda b, p, i: (p, 0, 0))],
        out_specs=pl.BlockSpec((1, tile, hps * D_V), lambda b, p, i: (b, i, p)),
        out_shape=jax.ShapeDtypeStruct((bsz, s_len, n_heads * D_V), _BF16),
        scratch_shapes=[pltpu.VMEM((hps, V_ROWS, tile), _F32), pltpu.VMEM((hps, 1, tile), _F32),
                        pltpu.VMEM((hps, LANES, tile), _F32)]
                       + [pltpu.VMEM((hps, tile, tile), _F32)] * 3,
        compiler_params=pltpu.CompilerParams(dimension_semantics=("arbitrary",) * 3,
                                             vmem_limit_bytes=VMEM_LIMIT_BYTES),
        name="prompt_attn",
    )(kbound, q, q, k, v, km, vm)


def _meta_attn_kernel(q_ref, k_ref, v_ref, o_ref):
    for pair in range(N_HEADS // 2):
        out = None
        for hd in (2 * pair, 2 * pair + 1):
            s = _dot_nt(q_ref[hd], k_ref[hd])
            p = jnp.exp2(s - jnp.max(s, axis=1, keepdims=True))
            o = _dot(p.astype(_BF16), v_ref[hd]) / jnp.sum(p, axis=1, keepdims=True)
            out = o if out is None else out + o
        o_ref[:, pair * LANES:(pair + 1) * LANES] = out.astype(o_ref.dtype)


def _meta_attn_call(q, k, v):
    n_rows = q.shape[1]
    return pl.pallas_call(
        _meta_attn_kernel, out_shape=jax.ShapeDtypeStruct((n_rows, N_HEADS * D_V), _BF16),
        name="meta_attn",
    )(q, k, v)


def _row_sums_t(x):
    ones = jnp.ones((8, x.shape[1]), _BF16)
    hi = x.astype(_BF16)
    lo = (x - hi.astype(_F32)).astype(_BF16)
    return (_dot_nt(ones, hi) + _dot_nt(ones, lo))[0:1]


def _sample_attn_kernel(q_ref, kn_ref, vn_ref, lat_ref, kpe_ref, wukt_ref, wukq_ref, wuv_ref, kgain_ref, place_ref,
                        o_ref, *, past):
    t_len = q_ref.shape[2]
    kgain = kgain_ref[...]
    qg = [(q_ref[0, hd].astype(_F32) * kgain).astype(_BF16) for hd in range(N_HEADS)]
    q_all = jnp.concatenate(qg, axis=0)
    q_lat = jnp.concatenate([_dot(qg[hd], wukq_ref[hd]) for hd in range(N_HEADS)], axis=0).astype(_BF16)

    def cached_scores(lo, hi):
        lat = lat_ref[0, 0, lo:hi].astype(_BF16)
        kpe = kpe_ref[0, lo:hi]
        s = _dot_nt(q_lat, lat) + _dot_nt(q_all, _perm(kpe, place_ref[...]).astype(_BF16))
        ss_rope = _row_sums_t(kpe * kpe)
        rows = []
        for pair in range(N_HEADS // 2):
            k_t = _dot_nt(wukt_ref[pair * LANES:(pair + 1) * LANES], lat)
            for hh in range(2):
                hd = 2 * pair + hh
                k_h = k_t[hh * D_NOPE:(hh + 1) * D_NOPE]
                r = lax.rsqrt((jnp.sum(k_h * k_h, axis=0, keepdims=True) + ss_rope) / D_QK + EPS)
                rows.append(s[hd * t_len:(hd + 1) * t_len] * r)
        return jnp.concatenate(rows, axis=0), lat

    s_meta, lat_meta = cached_scores(0, N_META)
    s_past, lat_past = cached_scores(N_META, N_META + past)
    tq = lax.broadcasted_iota(jnp.int32, (t_len, t_len), 0)
    tk = lax.broadcasted_iota(jnp.int32, (t_len, t_len), 1)
    new_mask = (past + tq) // CHUNK >= (past + tk) // CHUNK
    s_new = jnp.concatenate([jnp.where(new_mask, _dot_nt(q_ref[0, hd], kn_ref[0, hd]), MASKED)
                             for hd in range(N_HEADS)], axis=0)
    m = jnp.maximum(jnp.maximum(jnp.max(s_meta, axis=1, keepdims=True), jnp.max(s_past, axis=1, keepdims=True)),
                    jnp.max(s_new, axis=1, keepdims=True))
    p_meta, p_past, p_new = jnp.exp2(s_meta - m), jnp.exp2(s_past - m), jnp.exp2(s_new - m)
    den = (jnp.sum(p_meta, axis=1, keepdims=True) + jnp.sum(p_past, axis=1, keepdims=True)
           + jnp.sum(p_new, axis=1, keepdims=True))
    p_lat = (_dot(p_meta.astype(_BF16), lat_meta) + _dot(p_past.astype(_BF16), lat_past)).astype(_BF16)
    p_new = p_new.astype(_BF16)
    for pair in range(N_HEADS // 2):
        out = None
        for hd in (2 * pair, 2 * pair + 1):
            rows = slice(hd * t_len, (hd + 1) * t_len)
            o = (_dot(p_lat[rows], wuv_ref[:, hd * LANES:(hd + 1) * LANES]) + _dot(p_new[rows], vn_ref[0, hd])) / den[rows]
            out = o if out is None else out + o
        o_ref[0, :, pair * LANES:(pair + 1) * LANES] = out.astype(o_ref.dtype)


def _sample_attn_call(layer, q, k_new, v_new, cache_lat, cache_kpe, lw, consts):
    bsz, n_heads, t_len, _ = q.shape
    l_cache = cache_lat.shape[2]
    past = l_cache - N_META
    assert past % LANES == 0 and t_len % HIST_ROWS == 0
    head_spec = pl.BlockSpec((1, n_heads, t_len, LANES), lambda b: (b, 0, 0, 0))
    weights = (lw["w_uk_t"], lw["w_uk_q"], lw["w_uv"], lw["kgain"], consts["place"])
    return pl.pallas_call(
        functools.partial(_sample_attn_kernel, past=past), grid=(bsz,),
        in_specs=[head_spec, head_spec, head_spec,
                  pl.BlockSpec((1, 1, l_cache, KV_RANK), lambda b: (layer, b, 0, 0)),
                  pl.BlockSpec((1, l_cache, D_ROPE), lambda b: (b, 0, 0))]
                 + [_const_spec(w) for w in weights],
        out_specs=pl.BlockSpec((1, t_len, n_heads * D_V), lambda b: (b, 0, 0)),
        out_shape=jax.ShapeDtypeStruct((bsz, t_len, n_heads * D_V), _BF16),
        compiler_params=pltpu.CompilerParams(dimension_semantics=("arbitrary",),
                                             vmem_limit_bytes=VMEM_LIMIT_BYTES),
        name="sample_attn",
    )(q, k_new, v_new, cache_lat, cache_kpe, *weights)


def _pack_heads(w, width, alternate=False):
    kdim = w.shape[0]
    w3 = w.reshape(kdim, N_HEADS, width)
    low = jnp.pad(w3, ((0, 0), (0, 0), (0, LANES - width)))
    if alternate:
        high = jnp.pad(w3, ((0, 0), (0, 0), (LANES - width, 0)))
        odd = (jnp.arange(N_HEADS) % 2 == 1)[None, :, None]
        low = jnp.where(odd, high, low)
    return low.reshape(kdim, N_HEADS * LANES)


def _layer_weights(l, norm_mix, w_in, q_a_norm, w_uq, kv_a_norm, w_uk, w_uv, q_norm, k_norm,
                   w_pool, pool_scale, w_o, norm_ffn, w_gate, w_up, w_down):
    d_in = w_in.shape[2]
    d_in_pad = -(-d_in // (2 * LANES)) * (2 * LANES)
    pad96 = lambda g: jnp.pad(g.astype(_F32), (0, LANES - D_QK))[None, :]
    q_scale = D_QK ** -0.5 * math.log2(math.e)
    return dict(
        gmix=norm_mix[l][None, :].astype(_F32),
        w_in=jnp.pad(w_in[l], ((0, 0), (0, d_in_pad - d_in))).astype(_BF16),
        gq=q_a_norm[l][None, :].astype(_F32),
        w_uq=_pack_heads(w_uq[l], D_QK).astype(_BF16),
        w_uq_t=_pack_heads(w_uq[l], D_QK).astype(_BF16).T,
        gkv=kv_a_norm[l][None, :].astype(_F32),
        w_uk=_pack_heads(w_uk[l], D_NOPE).astype(_BF16),
        w_uk_t=w_uk[l].astype(_BF16).T,
        w_uk_q=jnp.pad(w_uk[l].astype(_BF16).T.reshape(N_HEADS, D_NOPE, KV_RANK),
                       ((0, 0), (0, LANES - D_NOPE), (0, 0))),
        w_uv=_pack_heads(w_uv[l], D_V, alternate=True).astype(_BF16),
        w_uv_t=w_uv[l].astype(_BF16).T,
        qgain=pad96(q_norm[l]) * q_scale,
        qgain_t=(pad96(q_norm[l]) * q_scale).T,
        kgain=pad96(k_norm[l]),
        w_pool=w_pool[l].astype(_BF16),
        pscale=pool_scale[l][None, :].astype(_F32),
        w_o_pool=w_o[l][:POOL_WIDTH].astype(_BF16),
        w_o_attn=w_o[l][POOL_WIDTH:].astype(_BF16),
        gffn=norm_ffn[l][None, :].astype(_F32),
        w_gate=w_gate[l].astype(_BF16),
        w_up=w_up[l].astype(_BF16),
        w_down=w_down[l].astype(_BF16),
    )


def _rope_tables(pos0, t_len):
    half = D_ROPE // 2
    inv = ROPE_THETA ** (-jnp.arange(half, dtype=_F32) / half)
    ang = (pos0 + jnp.arange(t_len, dtype=jnp.int32)).astype(_F32)[:, None] * inv[None, :]
    cos, sin = jnp.cos(ang), jnp.sin(ang)
    cos32 = jnp.concatenate([cos, cos], axis=1)
    sin32 = jnp.concatenate([-sin, sin], axis=1)
    ones = jnp.ones((t_len, D_NOPE), _F32)
    zeros = jnp.zeros((t_len, D_NOPE), _F32)
    tail1 = jnp.ones((t_len, LANES - D_QK), _F32)
    tail0 = jnp.zeros((t_len, LANES - D_QK), _F32)
    cosp = jnp.concatenate([ones, cos32, tail1], axis=1)
    sinp = jnp.concatenate([zeros, sin32, tail0], axis=1)
    return (cosp, sinp, cos32, sin32, cos.T, sin.T)


def _perm_consts():
    half = D_ROPE // 2
    idx = jnp.arange(D_ROPE)
    place = jnp.zeros((D_ROPE, LANES), _F32).at[idx, D_NOPE + idx].set(1.0)
    swap = jnp.zeros((D_ROPE, D_ROPE), _F32).at[idx, (idx + half) % D_ROPE].set(1.0)
    return dict(place=place.astype(_BF16), swap=swap.astype(_BF16))


def kernel(x_prompt, x_sample, cache_latent, cache_krope, state_pool, meta_tokens, norm_mix, w_in, q_a_norm, w_uq, kv_a_norm, w_uk, w_uv, q_norm, k_norm, w_pool, pool_scale, w_o, norm_ffn, w_gate, w_up, w_down):
    depth = norm_mix.shape[0]
    bp, s_len, d_model = x_prompt.shape
    bs, t_len, _ = x_sample.shape
    past = cache_latent.shape[2] - N_META
    consts = _perm_consts()
    tabs_meta = _rope_tables(0, N_META)
    tabs_prompt = _rope_tables(N_META, s_len)
    tabs_sample = _rope_tables(N_META + past, t_len)
    nb_sample = math.gcd(bs, 8)

    xm = meta_tokens.astype(_F32)[None]
    xp = x_prompt
    xs = x_sample
    zero_hist = jnp.zeros((1, HIST_ROWS, POOL_WIDTH), _F32)
    seq_bufs = tuple(jnp.zeros((depth, bp, N_META + s_len, width), _F32) for width in (KV_RANK, D_ROPE))
    meta_lat, meta_kr, pool_p, lat_s, kpe_s, pool_s = [], [], [], [], [], []
    for l in range(depth):
        lw = _layer_weights(l, norm_mix, w_in, q_a_norm, w_uq, kv_a_norm, w_uk, w_uv, q_norm, k_norm,
                            w_pool, pool_scale, w_o, norm_ffn, w_gate, w_up, w_down)
        pm = _pre_call(xm, zero_hist, tabs_meta, lw, consts, nb=1, hist_shared=True, truncated=True)
        pp = _pre_call(xp, pm["ulast"], tabs_prompt, lw, consts, nb=1, hist_shared=True, truncated=False,
                       feature_major=True, into=(l, depth, seq_bufs))
        seq_bufs = (pp["lat"], pp["krope"])
        hist_s = jnp.pad(state_pool[l].astype(_F32), ((0, 0), (HIST_ROWS - POOL_STATE, 0), (0, 0)))
        ps = _pre_call(xs, hist_s, tabs_sample, lw, consts, nb=nb_sample, hist_shared=False, truncated=False)

        km = jnp.pad(pm["k"][0], ((0, 0), (0, LANES - N_META), (0, 0)))
        vm_t = jnp.stack([pm["v"][0, hd, :, (hd % 2) * D_V:(hd % 2 + 1) * D_V].T for hd in range(N_HEADS)])
        vm_t = jnp.concatenate([vm_t, jnp.ones((N_HEADS, V_ROWS - D_V, N_META), vm_t.dtype)], axis=1)
        vm_t = jnp.pad(vm_t, ((0, 0), (0, 0), (0, LANES - N_META)))
        kbound = (D_QK ** 0.5 * jnp.max(jnp.abs(lw["kgain"]))).reshape(1, 1)
        qbound = D_QK ** 0.5 * jnp.max(jnp.abs(lw["qgain"]))
        attn_args = (kbound, pp["q"], pp["k"], pp["v"], km, vm_t)
        attn_p = lax.cond(2.0 * qbound * kbound[0, 0] <= EXP2_SAFE_RANGE,
                          functools.partial(_prompt_attn_call, online=False),
                          functools.partial(_prompt_attn_call, online=True), *attn_args)
        attn_s = _sample_attn_call(l, ps["q"], ps["k"], ps["v"], cache_latent, cache_krope[l], lw, consts)
        xp = _post_call(xp, pp["pool"], attn_p, lw)
        xs = _post_call(xs, ps["pool"], attn_s, lw)
        if l + 1 < depth:
            attn_m = _meta_attn_call(pm["q"][0], pm["k"][0], pm["v"][0])
            xm = _post_call(xm, pm["pool"], attn_m[None], lw)

        meta_lat.append(pm["lat"][0])
        meta_kr.append(pm["krope"][0])
        pool_p.append(pp["ulast"][:, HIST_ROWS - POOL_STATE:])
        lat_s.append(ps["lat"])
        kpe_s.append(ps["krope"])
        pool_s.append(ps["ulast"][:, HIST_ROWS - POOL_STATE:])
    lat_p, kpe_p = _meta_rows_call(*seq_bufs, jnp.stack(meta_lat), jnp.stack(meta_kr))
    return (xp, xs, lat_p, kpe_p, jnp.stack(pool_p), jnp.stack(lat_s), jnp.stack(kpe_s), jnp.stack(pool_s))
```

```python
import functools
import math

import jax
import jax.numpy as jnp
from jax import lax
from jax.experimental import pallas as pl
from jax.experimental.pallas import tpu as pltpu

CHUNK = 64
N_META = 16
N_HEADS = 8
D_NOPE = 64
D_ROPE = 32
D_QK = D_NOPE + D_ROPE
D_V = 64
Q_RANK = 384
KV_RANK = 256
POOL_WINDOWS = (2, 4, 8, 16)
POOL_GROUP = 128
POOL_WIDTH = POOL_GROUP * len(POOL_WINDOWS)
POOL_STATE = max(POOL_WINDOWS) - 1
ROPE_THETA = 10000.0
EPS = 1e-6

LANES = 128
HIST_ROWS = 16
VMEM_LIMIT_BYTES = 56 * 1024 * 1024
ROW_TILE = 512
PRE_SPLITS = 2
ATTN_HEADS_PER_STEP = 4
SAMPLE_KEY_BLOCK = 1024
V_ROWS = D_V + 16
MASKED = -1e30
EXP2_SAFE_RANGE = 120.0

_F32 = jnp.float32
_BF16 = jnp.bfloat16
_NT = (((1,), (1,)), ((), ()))


def _dot(a, b):
    return jnp.dot(a, b, preferred_element_type=_F32)


def _dot_nt(a, b):
    return lax.dot_general(a, b, _NT, preferred_element_type=_F32)


def _rms(x, g):
    return x * lax.rsqrt(jnp.mean(x * x, axis=-1, keepdims=True) + EPS) * g


def _perm(x, p):
    hi = x.astype(_BF16)
    lo = (x - hi.astype(_F32)).astype(_BF16)
    return _dot(hi, p) + _dot(lo, p)


def _head_norm(x, gain):
    ss = jnp.sum(x * x, axis=-1, keepdims=True)
    return x * lax.rsqrt(ss / D_QK + EPS) * gain


def _pre_kernel(x_ref, hist_ref, cosp_ref, sinp_ref, cos32_ref, sin32_ref, cost_ref, sint_ref,
                gmix_ref, win_ref, gq_ref, wuq_ref, gkv_ref, wuk_ref, wuv_ref, qgain_ref, kgain_ref,
                wpool_ref, pscale_ref, place_ref, swap_ref,
                q_ref, k_ref, v_ref, lat_ref, krope_ref, pool_ref, ulast_ref,
                ext_ref, *, nb, tm, hist_shared, truncated, feature_major, splits):
    t = pl.program_id(1)
    d_model = x_ref.shape[-1]
    for b in range(nb):
        @pl.when(t == 0)
        def _():
            ext_ref[b, 0:HIST_ROWS, :] = hist_ref[0 if hist_shared else b]

        @pl.when(t > 0)
        def _():
            ext_ref[b, 0:HIST_ROWS, :] = ext_ref[b, tm:tm + HIST_ROWS, :]

    def row_range(r0, rs):
        rows = nb * rs
        row_sl = slice(r0, r0 + rs)
        x = x_ref[:, row_sl, :].reshape(rows, d_model)
        h = _rms(x, gmix_ref[...]).astype(_BF16)
        z = _dot(h, win_ref[...])
        u = z[:, :POOL_WIDTH]
        q_lat = z[:, POOL_WIDTH:POOL_WIDTH + Q_RANK]
        kv_lat = z[:, POOL_WIDTH + Q_RANK:POOL_WIDTH + Q_RANK + KV_RANK]
        kpe = z[:, POOL_WIDTH + Q_RANK + KV_RANK:POOL_WIDTH + Q_RANK + KV_RANK + D_ROPE]

        first = HIST_ROWS + r0
        for b in range(nb):
            ext_ref[b, first:first + rs, :] = u[b * rs:(b + 1) * rs]
        for b in range(nb):
            cols = []
            for g, w in enumerate(POOL_WINDOWS):
                sl = slice(g * POOL_GROUP, (g + 1) * POOL_GROUP)
                cur = ext_ref[b, first:first + rs, sl]
                acc = cur
                for j in range(1, w):
                    acc = acc + ext_ref[b, first - j:first - j + rs, sl]
                if truncated:
                    row = t * tm + r0 + lax.broadcasted_iota(jnp.int32, (rs, 1), 0)
                    mean = acc / jnp.minimum(row + 1, w).astype(_F32)
                else:
                    mean = acc / float(w)
                cols.append(_dot((mean - cur).astype(_BF16), wpool_ref[g]))
            pool = jnp.concatenate(cols, axis=1) * pscale_ref[...]
            pool_ref[b, row_sl, :] = pool.astype(pool_ref.dtype)

        qn = _rms(q_lat, gq_ref[...]).astype(_BF16)
        c_kv = _rms(kv_lat, gkv_ref[...])
        into_results = len(lat_ref.shape) == 4
        lat_view = lat_ref.at[0] if into_results else lat_ref
        krope_view = krope_ref.at[0] if into_results else krope_ref
        lat_view[:, row_sl, :] = c_kv.reshape(nb, rs, KV_RANK)
        ckv_b = c_kv.astype(_BF16)
        knope = _dot(ckv_b, wuk_ref[...])

        def per_row(ref):
            tab = ref[row_sl, :]
            return tab if nb == 1 else jnp.concatenate([tab] * nb, axis=0)

        krope = kpe * per_row(cos32_ref) + _perm(kpe, swap_ref[...]) * per_row(sin32_ref)
        krope_view[:, row_sl, :] = krope.reshape(nb, rs, D_ROPE)
        kpe_placed = _perm(krope, place_ref[...])

        for hd in range(N_HEADS):
            kh = _head_norm(knope[:, hd * LANES:(hd + 1) * LANES] + kpe_placed, kgain_ref[...])
            k_ref[:, hd, row_sl, :] = kh.astype(k_ref.dtype).reshape(nb, rs, LANES)

        if feature_major:
            half = D_ROPE // 2
            qraw_t = _dot_nt(wuq_ref[...], qn)
            v_t = _dot_nt(wuv_ref[...], ckv_b)
            ones = jnp.ones((V_ROWS - D_V, rows), _F32)
            for hd in range(N_HEADS):
                v_h = jnp.concatenate([v_t[hd * D_V:(hd + 1) * D_V], ones], axis=0)
                v_ref[0, hd, 0, :, row_sl] = v_h.astype(v_ref.dtype)
            cos_t, sin_t, gain = cost_ref[:, row_sl], sint_ref[:, row_sl], qgain_ref[...]
            for hd in range(N_HEADS):
                base = hd * LANES
                nope = qraw_t[base:base + D_NOPE]
                x1 = qraw_t[base + D_NOPE:base + D_NOPE + half]
                x2 = qraw_t[base + D_NOPE + half:base + D_QK]
                r1 = x1 * cos_t - x2 * sin_t
                r2 = x1 * sin_t + x2 * cos_t
                ss = (jnp.sum(nope * nope, axis=0, keepdims=True) + jnp.sum(r1 * r1, axis=0, keepdims=True)
                      + jnp.sum(r2 * r2, axis=0, keepdims=True))
                scale = lax.rsqrt(ss / D_QK + EPS)
                qh = jnp.concatenate([nope * scale * gain[:D_NOPE], r1 * scale * gain[D_NOPE:D_NOPE + half],
                                      r2 * scale * gain[D_NOPE + half:D_QK],
                                      jnp.zeros((LANES - D_QK, rows), _F32)], axis=0)
                q_ref[0, hd, :, row_sl] = qh.astype(q_ref.dtype)
        else:
            cosp, sinp = per_row(cosp_ref), per_row(sinp_ref)
            qraw = _dot(qn, wuq_ref[...])
            vfull = _dot(ckv_b, wuv_ref[...])
            lane = lax.broadcasted_iota(jnp.int32, (1, LANES), 1)
            first_half = lane < D_NOPE + D_ROPE // 2
            for hd in range(N_HEADS):
                sl = slice(hd * LANES, (hd + 1) * LANES)
                qh = qraw[:, sl]
                partner = jnp.where(first_half, pltpu.roll(qh, LANES - D_ROPE // 2, 1),
                                    pltpu.roll(qh, D_ROPE // 2, 1))
                qh = _head_norm(qh * cosp + partner * sinp, qgain_ref[...])
                q_ref[:, hd, row_sl, :] = qh.astype(q_ref.dtype).reshape(nb, rs, LANES)
                v_ref[:, hd, row_sl, :] = vfull[:, sl].astype(v_ref.dtype).reshape(nb, rs, LANES)

    for sp in range(splits):
        row_range(sp * (tm // splits), tm // splits)
    for b in range(nb):
        ulast_ref[b] = ext_ref[b, tm:tm + HIST_ROWS, :]


def _const_spec(a):
    nd = a.ndim
    return pl.BlockSpec(a.shape, lambda *_: (0,) * nd, pipeline_mode=pl.Buffered(1))


def _skip_refs(fn, start, count):
    def wrapped(*refs, **kw):
        return fn(*refs[:start], *refs[start + count:], **kw)
    return wrapped


def _pre_call(x, hist, tabs, lw, consts, *, nb, hist_shared, truncated, feature_major=False, into=None):
    bsz, t_len, d_model = x.shape
    tm = t_len if nb > 1 else min(ROW_TILE, t_len)
    assert bsz % nb == 0 and t_len % tm == 0 and tm % HIST_ROWS == 0 and not (feature_major and nb > 1)
    grid = (bsz // nb, t_len // tm)
    fm = "_t" if feature_major else ""
    weights = (lw["gmix"], lw["w_in"], lw["gq"], lw["w_uq" + fm], lw["gkv"], lw["w_uk"], lw["w_uv" + fm],
               lw["qgain" + fm], lw["kgain"], lw["w_pool"], lw["pscale"], consts["place"], consts["swap"])
    hist_spec = (pl.BlockSpec((1, HIST_ROWS, POOL_WIDTH), lambda b, t: (0, 0, 0)) if hist_shared
                 else pl.BlockSpec((nb, HIST_ROWS, POOL_WIDTH), lambda b, t: (b, 0, 0)))
    in_specs = [pl.BlockSpec((nb, tm, d_model), lambda b, t: (b, t, 0)), hist_spec]
    in_specs += [pl.BlockSpec((tm, a.shape[1]), lambda b, t: (t, 0)) for a in tabs[:4]]
    in_specs += [pl.BlockSpec((a.shape[0], tm), lambda b, t: (0, t)) for a in tabs[4:]]
    in_specs += [_const_spec(w) for w in weights]
    head_spec = pl.BlockSpec((nb, N_HEADS, tm, LANES), lambda b, t: (b, 0, t, 0))
    head_shape = jax.ShapeDtypeStruct((bsz, N_HEADS, t_len, LANES), _BF16)
    if feature_major:
        q_spec = pl.BlockSpec((1, N_HEADS, LANES, tm), lambda b, t: (b, 0, 0, t))
        q_shape = jax.ShapeDtypeStruct((bsz, N_HEADS, LANES, t_len), _BF16)
        v_spec = pl.BlockSpec((1, N_HEADS, 1, V_ROWS, tm), lambda b, t: (b, 0, t, 0, 0))
        v_shape = jax.ShapeDtypeStruct((bsz, N_HEADS, t_len // tm, V_ROWS, tm), _BF16)
    else:
        q_spec, q_shape, v_spec, v_shape = head_spec, head_shape, head_spec, head_shape

    def row_out(width, dtype):
        return (pl.BlockSpec((nb, tm, width), lambda b, t: (b, t, 0)),
                jax.ShapeDtypeStruct((bsz, t_len, width), dtype))

    body = functools.partial(_pre_kernel, nb=nb, tm=tm, hist_shared=hist_shared, truncated=truncated,
                             feature_major=feature_major, splits=PRE_SPLITS if tm == ROW_TILE else 1)
    operands = [x, hist, *tabs, *weights]
    aliases = {}
    if into is None:
        lat_spec, lat_shape = row_out(KV_RANK, _F32)
        kr_spec, kr_shape = row_out(D_ROPE, _F32)
    else:
        layer, depth, bufs = into
        assert nb == 1

        def seq_out(width):
            block = tuple(pl.Element(n) for n in (1, 1, tm, width))
            return (pl.BlockSpec(block, lambda b, t: (layer, b, (1 + t * (tm // N_META)) * N_META, 0)),
                    jax.ShapeDtypeStruct((depth, bsz, N_META + t_len, width), _F32))

        lat_spec, lat_shape = seq_out(KV_RANK)
        kr_spec, kr_shape = seq_out(D_ROPE)
        body = _skip_refs(body, len(operands), len(bufs))
        aliases = {len(operands): 3, len(operands) + 1: 4}
        in_specs += [pl.BlockSpec(memory_space=pl.ANY)] * len(bufs)
        operands += list(bufs)
    pool_spec, pool_shape = row_out(POOL_WIDTH, _BF16)
    out_specs = [q_spec, head_spec, v_spec, lat_spec, kr_spec, pool_spec,
                 pl.BlockSpec((nb, HIST_ROWS, POOL_WIDTH), lambda b, t: (b, 0, 0))]
    out_shape = [q_shape, head_shape, v_shape, lat_shape, kr_shape, pool_shape,
                 jax.ShapeDtypeStruct((bsz, HIST_ROWS, POOL_WIDTH), _F32)]
    q, k, v, lat, krope, pool, ulast = pl.pallas_call(
        body, grid=grid, in_specs=in_specs, out_specs=out_specs, out_shape=out_shape,
        input_output_aliases=aliases,
        scratch_shapes=[pltpu.VMEM((nb, HIST_ROWS + tm, POOL_WIDTH), _F32)],
        compiler_params=pltpu.CompilerParams(dimension_semantics=("arbitrary", "arbitrary"),
                                             vmem_limit_bytes=VMEM_LIMIT_BYTES),
        name="pre_mixer",
    )(*operands)
    return dict(q=q, k=k, v=v, lat=lat, krope=krope, pool=pool, ulast=ulast)


def _meta_rows_kernel(lat_any, kr_any, mlat_ref, mkr_ref, lat_ref, kr_ref):
    del lat_any, kr_any
    lat_ref[0, 0] = mlat_ref[0]
    kr_ref[0, 0] = mkr_ref[0]


def _meta_rows_call(lat_all, kr_all, meta_lat, meta_kr):
    depth, bsz = lat_all.shape[:2]

    def src(width):
        return pl.BlockSpec((1, N_META, width), lambda l, b: (l, 0, 0))

    def dst(width):
        return pl.BlockSpec((1, 1, N_META, width), lambda l, b: (l, b, 0, 0))

    return pl.pallas_call(
        _meta_rows_kernel, grid=(depth, bsz),
        in_specs=[pl.BlockSpec(memory_space=pl.ANY)] * 2 + [src(KV_RANK), src(D_ROPE)],
        out_specs=[dst(KV_RANK), dst(D_ROPE)],
        out_shape=[jax.ShapeDtypeStruct(lat_all.shape, lat_all.dtype), jax.ShapeDtypeStruct(kr_all.shape, kr_all.dtype)],
        input_output_aliases={0: 0, 1: 1},
        compiler_params=pltpu.CompilerParams(dimension_semantics=("arbitrary", "arbitrary")),
        name="meta_rows",
    )(lat_all, kr_all, meta_lat, meta_kr)


def _post_kernel(x_ref, pool_ref, attn_ref, wop_ref, woa_ref, gffn_ref, wg_ref, wu_ref, wd_ref, o_ref):
    x1 = x_ref[...] + _dot(pool_ref[...], wop_ref[...]) + _dot(attn_ref[...], woa_ref[...])
    hn = _rms(x1, gffn_ref[...]).astype(_BF16)
    g = _dot(hn, wg_ref[...])
    up = _dot(hn, wu_ref[...])
    act = (g * jax.nn.sigmoid(g) * up).astype(_BF16)
    o_ref[...] = x1 + _dot(act, wd_ref[...])


def _post_call(x, pool, attn, lw):
    shape = x.shape
    d_model = shape[-1]
    x2 = x.reshape(-1, d_model)
    n = x2.shape[0]
    tm = min(ROW_TILE, n)
    assert n % tm == 0
    weights = (lw["w_o_pool"], lw["w_o_attn"], lw["gffn"], lw["w_gate"], lw["w_up"], lw["w_down"])

    def rows(width):
        return pl.BlockSpec((tm, width), lambda i: (i, 0))

    out = pl.pallas_call(
        _post_kernel, grid=(n // tm,),
        in_specs=[rows(d_model), rows(POOL_WIDTH), rows(N_HEADS * D_V)] + [_const_spec(w) for w in weights],
        out_specs=rows(d_model), out_shape=jax.ShapeDtypeStruct((n, d_model), _F32),
        compiler_params=pltpu.CompilerParams(dimension_semantics=("arbitrary",),
                                             vmem_limit_bytes=VMEM_LIMIT_BYTES),
        name="post_ffn",
    )(x2, pool.reshape(n, POOL_WIDTH), attn.reshape(n, N_HEADS * D_V), *weights)
    return out.reshape(shape)


def _attn_kernel(kbound_ref, q_ref, q_next_ref, k_ref, v_ref, km_ref, vm_ref, o_ref,
                 acc_ref, m_ref, s_meta_ref, s_first_ref, s_even_ref, s_odd_ref, *, tile, online):
    qi = pl.program_id(2)

    def scores(hh, k_rows, queries_ref=q_ref):
        return _dot(k_rows, queries_ref[0, hh])

    def consume(hh, s_t, v_cols, mask):
        if mask is not None:
            s_t = jnp.where(mask, s_t, MASKED)
        m_prev = m_ref[hh]
        m_next = jnp.maximum(m_prev, jnp.max(s_t, axis=0, keepdims=True)) if online else m_prev
        p_t = jnp.exp2(s_t - m_next).astype(_BF16)
        pv = _dot(v_cols, p_t)
        if online:
            acc_ref[hh] = jnp.exp2(m_prev - m_next) * acc_ref[hh] + pv
            m_ref[hh] = m_next
        else:
            acc_ref[hh] += pv

    def key_tile(hh, j):
        return k_ref[0, hh, pl.ds(pl.multiple_of(j * tile, tile), tile), :]

    def value_tile(hh, j):
        return v_ref[0, hh, j]

    heads = range(q_ref.shape[1])
    for hh in heads:
        acc_ref[hh] = jnp.zeros(acc_ref.shape[1:], _F32)
        if online:
            m_ref[hh] = jnp.full((1, tile), MASKED, _F32)
        else:
            qf = q_ref[0, hh].astype(_F32)
            m_ref[hh] = jnp.sqrt(jnp.sum(qf * qf, axis=0, keepdims=True)) * kbound_ref[...]
        s_meta_ref[hh] = scores(hh, km_ref[hh])

    @pl.when(qi == 0)
    def _():
        for hh in heads:
            s_even_ref[hh] = scores(hh, key_tile(hh, 0))

    def step(j, s_cur_ref, s_next_ref):
        for hh in heads:
            s_next_ref[hh] = scores(hh, key_tile(hh, j + 1))
            consume(hh, s_cur_ref[hh], value_tile(hh, j), None)

    @pl.when(qi >= 1)
    def _():
        step(0, s_first_ref, s_odd_ref)

    def two_steps(j):
        step(j, s_odd_ref, s_even_ref)
        step(j + 1, s_even_ref, s_odd_ref)

    def four_steps(jj, carry):
        two_steps(4 * jj + 1)
        two_steps(4 * jj + 3)
        return carry

    n_full = jnp.maximum(qi - 1, 0)
    rest = n_full % 4
    lax.fori_loop(0, n_full // 4, four_steps, 0)

    @pl.when(rest >= 2)
    def _():
        two_steps(n_full - rest + 1)

    @pl.when(rest % 2 == 1)
    def _():
        step(n_full, s_odd_ref, s_even_ref)

    key_chunk = lax.broadcasted_iota(jnp.int32, (tile, tile), 0) // CHUNK
    query_chunk = lax.broadcasted_iota(jnp.int32, (tile, tile), 1) // CHUNK
    meta_mask = lax.broadcasted_iota(jnp.int32, (km_ref.shape[1], tile), 0) < N_META
    mask = jnp.concatenate([query_chunk >= key_chunk, meta_mask], axis=0)
    for parity, s_fin_ref in enumerate((s_even_ref, s_odd_ref)):
        @pl.when(qi % 2 == parity)
        def _():
            for hh in heads:
                s_first_ref[hh] = scores(hh, key_tile(hh, 0), q_next_ref)
                s_t = jnp.concatenate([s_fin_ref[hh], s_meta_ref[hh]], axis=0)
                v_cols = jnp.concatenate([value_tile(hh, qi), vm_ref[hh]], axis=1)
                consume(hh, s_t, v_cols, mask)

    out_t = jnp.concatenate([acc_ref[hh, :D_V] / acc_ref[hh, D_V:D_V + 1] for hh in heads], axis=0)
    o_ref[0] = out_t.T.astype(o_ref.dtype)


def _prompt_attn_call(kbound, q, k, v, km, vm, *, online):
    bsz, n_heads, _, s_len = q.shape
    tile = v.shape[-1]
    n_tiles = s_len // tile
    hps = ATTN_HEADS_PER_STEP
    assert s_len % tile == 0 and tile % CHUNK == 0 and n_heads % hps == 0 and (hps * D_V) % LANES == 0
    grid = (bsz, n_heads // hps, n_tiles)
    return pl.pallas_call(
        functools.partial(_attn_kernel, tile=tile, online=online), grid=grid,
        in_specs=[pl.BlockSpec((1, 1), lambda b, p, i: (0, 0)),
                  pl.BlockSpec((1, hps, LANES, tile), lambda b, p, i: (b, p, 0, i)),
                  pl.BlockSpec((1, hps, LANES, tile), lambda b, p, i: (b, p, 0, jnp.minimum(i + 1, n_tiles - 1))),
                  pl.BlockSpec((1, hps, s_len, LANES), lambda b, p, i: (b, p, 0, 0)),
                  pl.BlockSpec((1, hps, n_tiles, V_ROWS, tile), lambda b, p, i: (b, p, 0, 0, 0)),
                  pl.BlockSpec((hps, LANES, LANES), lambda b, p, i: (p, 0, 0)),
                  pl.BlockSpec((hps, V_ROWS, LANES), lambda b, p, i: (p, 0, 0))],
        out_specs=pl.BlockSpec((1, tile, hps * D_V), lambda b, p, i: (b, i, p)),
        out_shape=jax.ShapeDtypeStruct((bsz, s_len, n_heads * D_V), _BF16),
        scratch_shapes=[pltpu.VMEM((hps, V_ROWS, tile), _F32), pltpu.VMEM((hps, 1, tile), _F32),
                        pltpu.VMEM((hps, LANES, tile), _F32)]
                       + [pltpu.VMEM((hps, tile, tile), _F32)] * 3,
        compiler_params=pltpu.CompilerParams(dimension_semantics=("arbitrary",) * 3,
                                             vmem_limit_bytes=VMEM_LIMIT_BYTES),
        name="prompt_attn",
    )(kbound, q, q, k, v, km, vm)


def _meta_attn_kernel(q_ref, k_ref, v_ref, o_ref):
    for pair in range(N_HEADS // 2):
        out = None
        for hd in (2 * pair, 2 * pair + 1):
            s = _dot_nt(q_ref[hd], k_ref[hd])
            p = jnp.exp2(s - jnp.max(s, axis=1, keepdims=True))
            o = _dot(p.astype(_BF16), v_ref[hd]) / jnp.sum(p, axis=1, keepdims=True)
            out = o if out is None else out + o
        o_ref[:, pair * LANES:(pair + 1) * LANES] = out.astype(o_ref.dtype)


def _meta_attn_call(q, k, v):
    n_rows = q.shape[1]
    return pl.pallas_call(
        _meta_attn_kernel, out_shape=jax.ShapeDtypeStruct((n_rows, N_HEADS * D_V), _BF16),
        name="meta_attn",
    )(q, k, v)


def _row_sums_t(x):
    ones = jnp.ones((8, x.shape[1]), _BF16)
    hi = x.astype(_BF16)
    lo = (x - hi.astype(_F32)).astype(_BF16)
    return (_dot_nt(ones, hi) + _dot_nt(ones, lo))[0:1]


def _sample_attn_kernel(q_ref, kn_ref, vn_ref, lat_ref, kpe_ref, wukt_ref, wukq_ref, wuv_ref, kgain_ref, place_ref,
                        o_ref, *, past):
    t_len = q_ref.shape[2]
    kgain = kgain_ref[...]
    qg = [(q_ref[0, hd].astype(_F32) * kgain).astype(_BF16) for hd in range(N_HEADS)]
    q_all = jnp.concatenate(qg, axis=0)
    q_lat = jnp.concatenate([_dot(qg[hd], wukq_ref[hd]) for hd in range(N_HEADS)], axis=0).astype(_BF16)
    q_rope = _dot_nt(q_all, place_ref[...]).astype(_BF16)

    def cached_scores(lo, hi):
        lat = lat_ref[0, 0, lo:hi].astype(_BF16)
        kpe = kpe_ref[0, lo:hi]
        s = _dot_nt(q_lat, lat) + _dot_nt(q_rope, kpe.astype(_BF16))
        ss_rope = _row_sums_t(kpe * kpe)
        rows = []
        for pair in range(N_HEADS // 2):
            k_t = _dot_nt(wukt_ref[pair * LANES:(pair + 1) * LANES], lat)
            for hh in range(2):
                hd = 2 * pair + hh
                k_h = k_t[hh * D_NOPE:(hh + 1) * D_NOPE]
                r = lax.rsqrt((jnp.sum(k_h * k_h, axis=0, keepdims=True) + ss_rope) / D_QK + EPS)
                rows.append(s[hd * t_len:(hd + 1) * t_len] * r)
        return jnp.concatenate(rows, axis=0), lat

    block = min(SAMPLE_KEY_BLOCK, past)
    bounds = [(0, N_META)] + [(lo, lo + block) for lo in range(N_META, N_META + past, block)]
    m = jnp.full((N_HEADS * t_len, 1), MASKED, _F32)
    den = jnp.zeros((N_HEADS * t_len, 1), _F32)
    p_lat = jnp.zeros((N_HEADS * t_len, KV_RANK), _F32)
    for lo, hi in bounds:
        s, lat = cached_scores(lo, hi)
        m_next = jnp.maximum(m, jnp.max(s, axis=1, keepdims=True))
        alpha = jnp.exp2(m - m_next)
        p = jnp.exp2(s - m_next)
        den = alpha * den + jnp.sum(p, axis=1, keepdims=True)
        p_lat = alpha * p_lat + _dot(p.astype(_BF16), lat)
        m = m_next
    tq = lax.broadcasted_iota(jnp.int32, (t_len, t_len), 0)
    tk = lax.broadcasted_iota(jnp.int32, (t_len, t_len), 1)
    new_mask = (past + tq) // CHUNK >= (past + tk) // CHUNK
    s_new = jnp.concatenate([jnp.where(new_mask, _dot_nt(q_ref[0, hd], kn_ref[0, hd]), MASKED)
                             for hd in range(N_HEADS)], axis=0)
    m_next = jnp.maximum(m, jnp.max(s_new, axis=1, keepdims=True))
    alpha = jnp.exp2(m - m_next)
    p_new = jnp.exp2(s_new - m_next)
    den = alpha * den + jnp.sum(p_new, axis=1, keepdims=True)
    p_lat = (alpha * p_lat).astype(_BF16)
    p_new = p_new.astype(_BF16)
    for pair in range(N_HEADS // 2):
        out = None
        for hd in (2 * pair, 2 * pair + 1):
            rows = slice(hd * t_len, (hd + 1) * t_len)
            o = (_dot(p_lat[rows], wuv_ref[:, hd * LANES:(hd + 1) * LANES]) + _dot(p_new[rows], vn_ref[0, hd])) / den[rows]
            out = o if out is None else out + o
        o_ref[0, :, pair * LANES:(pair + 1) * LANES] = out.astype(o_ref.dtype)


def _sample_attn_call(layer, q, k_new, v_new, cache_lat, cache_kpe, lw, consts):
    bsz, n_heads, t_len, _ = q.shape
    l_cache = cache_lat.shape[2]
    past = l_cache - N_META
    assert past % min(SAMPLE_KEY_BLOCK, past) == 0 and past % LANES == 0 and t_len % HIST_ROWS == 0
    head_spec = pl.BlockSpec((1, n_heads, t_len, LANES), lambda b: (b, 0, 0, 0))
    weights = (lw["w_uk_t"], lw["w_uk_q"], lw["w_uv"], lw["kgain"], consts["place"])
    return pl.pallas_call(
        functools.partial(_sample_attn_kernel, past=past), grid=(bsz,),
        in_specs=[head_spec, head_spec, head_spec,
                  pl.BlockSpec((1, 1, l_cache, KV_RANK), lambda b: (layer, b, 0, 0)),
                  pl.BlockSpec((1, l_cache, D_ROPE), lambda b: (b, 0, 0))]
                 + [_const_spec(w) for w in weights],
        out_specs=pl.BlockSpec((1, t_len, n_heads * D_V), lambda b: (b, 0, 0)),
        out_shape=jax.ShapeDtypeStruct((bsz, t_len, n_heads * D_V), _BF16),
        compiler_params=pltpu.CompilerParams(dimension_semantics=("arbitrary",),
                                             vmem_limit_bytes=VMEM_LIMIT_BYTES),
        name="sample_attn",
    )(q, k_new, v_new, cache_lat, cache_kpe, *weights)


def _pack_heads(w, width, alternate=False):
    kdim = w.shape[0]
    w3 = w.reshape(kdim, N_HEADS, width)
    low = jnp.pad(w3, ((0, 0), (0, 0), (0, LANES - width)))
    if alternate:
        high = jnp.pad(w3, ((0, 0), (0, 0), (LANES - width, 0)))
        odd = (jnp.arange(N_HEADS) % 2 == 1)[None, :, None]
        low = jnp.where(odd, high, low)
    return low.reshape(kdim, N_HEADS * LANES)


def _layer_weights(l, norm_mix, w_in, q_a_norm, w_uq, kv_a_norm, w_uk, w_uv, q_norm, k_norm,
                   w_pool, pool_scale, w_o, norm_ffn, w_gate, w_up, w_down):
    d_in = w_in.shape[2]
    d_in_pad = -(-d_in // (2 * LANES)) * (2 * LANES)
    pad96 = lambda g: jnp.pad(g.astype(_F32), (0, LANES - D_QK))[None, :]
    q_scale = D_QK ** -0.5 * math.log2(math.e)
    return dict(
        gmix=norm_mix[l][None, :].astype(_F32),
        w_in=jnp.pad(w_in[l], ((0, 0), (0, d_in_pad - d_in))).astype(_BF16),
        gq=q_a_norm[l][None, :].astype(_F32),
        w_uq=_pack_heads(w_uq[l], D_QK).astype(_BF16),
        w_uq_t=_pack_heads(w_uq[l], D_QK).astype(_BF16).T,
        gkv=kv_a_norm[l][None, :].astype(_F32),
        w_uk=_pack_heads(w_uk[l], D_NOPE).astype(_BF16),
        w_uk_t=w_uk[l].astype(_BF16).T,
        w_uk_q=jnp.pad(w_uk[l].astype(_BF16).T.reshape(N_HEADS, D_NOPE, KV_RANK),
                       ((0, 0), (0, LANES - D_NOPE), (0, 0))),
        w_uv=_pack_heads(w_uv[l], D_V, alternate=True).astype(_BF16),
        w_uv_t=w_uv[l].astype(_BF16).T,
        qgain=pad96(q_norm[l]) * q_scale,
        qgain_t=(pad96(q_norm[l]) * q_scale).T,
        kgain=pad96(k_norm[l]),
        w_pool=w_pool[l].astype(_BF16),
        pscale=pool_scale[l][None, :].astype(_F32),
        w_o_pool=w_o[l][:POOL_WIDTH].astype(_BF16),
        w_o_attn=w_o[l][POOL_WIDTH:].astype(_BF16),
        gffn=norm_ffn[l][None, :].astype(_F32),
        w_gate=w_gate[l].astype(_BF16),
        w_up=w_up[l].astype(_BF16),
        w_down=w_down[l].astype(_BF16),
    )


def _rope_tables(pos0, t_len):
    half = D_ROPE // 2
    inv = ROPE_THETA ** (-jnp.arange(half, dtype=_F32) / half)
    ang = (pos0 + jnp.arange(t_len, dtype=jnp.int32)).astype(_F32)[:, None] * inv[None, :]
    cos, sin = jnp.cos(ang), jnp.sin(ang)
    cos32 = jnp.concatenate([cos, cos], axis=1)
    sin32 = jnp.concatenate([-sin, sin], axis=1)
    ones = jnp.ones((t_len, D_NOPE), _F32)
    zeros = jnp.zeros((t_len, D_NOPE), _F32)
    tail1 = jnp.ones((t_len, LANES - D_QK), _F32)
    tail0 = jnp.zeros((t_len, LANES - D_QK), _F32)
    cosp = jnp.concatenate([ones, cos32, tail1], axis=1)
    sinp = jnp.concatenate([zeros, sin32, tail0], axis=1)
    return (cosp, sinp, cos32, sin32, cos.T, sin.T)


def _perm_consts():
    half = D_ROPE // 2
    idx = jnp.arange(D_ROPE)
    place = jnp.zeros((D_ROPE, LANES), _F32).at[idx, D_NOPE + idx].set(1.0)
    swap = jnp.zeros((D_ROPE, D_ROPE), _F32).at[idx, (idx + half) % D_ROPE].set(1.0)
    return dict(place=place.astype(_BF16), swap=swap.astype(_BF16))


def kernel(x_prompt, x_sample, cache_latent, cache_krope, state_pool, meta_tokens, norm_mix, w_in, q_a_norm, w_uq, kv_a_norm, w_uk, w_uv, q_norm, k_norm, w_pool, pool_scale, w_o, norm_ffn, w_gate, w_up, w_down):
    depth = norm_mix.shape[0]
    bp, s_len, d_model = x_prompt.shape
    bs, t_len, _ = x_sample.shape
    past = cache_latent.shape[2] - N_META
    consts = _perm_consts()
    tabs_meta = _rope_tables(0, N_META)
    tabs_prompt = _rope_tables(N_META, s_len)
    tabs_sample = _rope_tables(N_META + past, t_len)
    nb_sample = math.gcd(bs, 8)

    xm = meta_tokens.astype(_F32)[None]
    xp = x_prompt
    xs = x_sample
    zero_hist = jnp.zeros((1, HIST_ROWS, POOL_WIDTH), _F32)
    seq_bufs = tuple(jnp.zeros((depth, bp, N_META + s_len, width), _F32) for width in (KV_RANK, D_ROPE))
    meta_lat, meta_kr, pool_p, lat_s, kpe_s, pool_s = [], [], [], [], [], []
    for l in range(depth):
        lw = _layer_weights(l, norm_mix, w_in, q_a_norm, w_uq, kv_a_norm, w_uk, w_uv, q_norm, k_norm,
                            w_pool, pool_scale, w_o, norm_ffn, w_gate, w_up, w_down)
        pm = _pre_call(xm, zero_hist, tabs_meta, lw, consts, nb=1, hist_shared=True, truncated=True)
        pp = _pre_call(xp, pm["ulast"], tabs_prompt, lw, consts, nb=1, hist_shared=True, truncated=False,
                       feature_major=True, into=(l, depth, seq_bufs))
        seq_bufs = (pp["lat"], pp["krope"])
        hist_s = jnp.pad(state_pool[l].astype(_F32), ((0, 0), (HIST_ROWS - POOL_STATE, 0), (0, 0)))
        ps = _pre_call(xs, hist_s, tabs_sample, lw, consts, nb=nb_sample, hist_shared=False, truncated=False)

        km = jnp.pad(pm["k"][0], ((0, 0), (0, LANES - N_META), (0, 0)))
        vm_t = jnp.stack([pm["v"][0, hd, :, (hd % 2) * D_V:(hd % 2 + 1) * D_V].T for hd in range(N_HEADS)])
        vm_t = jnp.concatenate([vm_t, jnp.ones((N_HEADS, V_ROWS - D_V, N_META), vm_t.dtype)], axis=1)
        vm_t = jnp.pad(vm_t, ((0, 0), (0, 0), (0, LANES - N_META)))
        kbound = (D_QK ** 0.5 * jnp.max(jnp.abs(lw["kgain"]))).reshape(1, 1)
        qbound = D_QK ** 0.5 * jnp.max(jnp.abs(lw["qgain"]))
        attn_args = (kbound, pp["q"], pp["k"], pp["v"], km, vm_t)
        attn_p = lax.cond(2.0 * qbound * kbound[0, 0] <= EXP2_SAFE_RANGE,
                          functools.partial(_prompt_attn_call, online=False),
                          functools.partial(_prompt_attn_call, online=True), *attn_args)
        attn_s = _sample_attn_call(l, ps["q"], ps["k"], ps["v"], cache_latent, cache_krope[l], lw, consts)
        xp = _post_call(xp, pp["pool"], attn_p, lw)
        xs = _post_call(xs, ps["pool"], attn_s, lw)
        if l + 1 < depth:
            attn_m = _meta_attn_call(pm["q"][0], pm["k"][0], pm["v"][0])
            xm = _post_call(xm, pm["pool"], attn_m[None], lw)

        meta_lat.append(pm["lat"][0])
        meta_kr.append(pm["krope"][0])
        pool_p.append(pp["ulast"][:, HIST_ROWS - POOL_STATE:])
        lat_s.append(ps["lat"])
        kpe_s.append(ps["krope"])
        pool_s.append(ps["ulast"][:, HIST_ROWS - POOL_STATE:])
    lat_p, kpe_p = _meta_rows_call(*seq_bufs, jnp.stack(meta_lat), jnp.stack(meta_kr))
    return (xp, xs, lat_p, kpe_p, jnp.stack(pool_p), jnp.stack(lat_s), jnp.stack(kpe_s), jnp.stack(pool_s))
```

```python
import functools
import math

import jax
import jax.numpy as jnp
from jax import lax
from jax.experimental import pallas as pl
from jax.experimental.pallas import tpu as pltpu

CHUNK = 64
N_META = 16
N_HEADS = 8
D_NOPE = 64
D_ROPE = 32
D_QK = D_NOPE + D_ROPE
D_V = 64
Q_RANK = 384
KV_RANK = 256
POOL_WINDOWS = (2, 4, 8, 16)
POOL_GROUP = 128
POOL_WIDTH = POOL_GROUP * len(POOL_WINDOWS)
POOL_STATE = max(POOL_WINDOWS) - 1
ROPE_THETA = 10000.0
EPS = 1e-6

LANES = 128
HIST_ROWS = 16
VMEM_LIMIT_BYTES = 56 * 1024 * 1024
ROW_TILE = 512
PRE_SPLITS = 2
ATTN_HEADS_PER_STEP = 4
SAMPLE_KEY_BLOCK = 1024
V_ROWS = D_V + 16
MASKED = -1e30
EXP2_SAFE_RANGE = 120.0

_F32 = jnp.float32
_BF16 = jnp.bfloat16
_NT = (((1,), (1,)), ((), ()))


def _dot(a, b):
    return jnp.dot(a, b, preferred_element_type=_F32)


def _dot_nt(a, b):
    return lax.dot_general(a, b, _NT, preferred_element_type=_F32)


def _rms(x, g):
    return x * lax.rsqrt(jnp.mean(x * x, axis=-1, keepdims=True) + EPS) * g


def _head_norm(x, gain):
    ss = jnp.sum(x * x, axis=-1, keepdims=True)
    return x * lax.rsqrt(ss / D_QK + EPS) * gain


def _pre_kernel(x_ref, hist_ref, cosp_ref, sinp_ref, cosk_ref, sink_ref, cost_ref, sint_ref,
                gmix_ref, win_ref, gq_ref, wuq_ref, gkv_ref, wuk_ref, wuv_ref, qgain_ref, kgain_ref,
                wpool_ref, pscale_ref,
                q_ref, k_ref, v_ref, lat_ref, krope_ref, pool_ref, ulast_ref,
                ext_ref, *, nb, tm, hist_shared, truncated, feature_major, splits):
    t = pl.program_id(1)
    d_model = x_ref.shape[-1]
    for b in range(nb):
        @pl.when(t == 0)
        def _():
            ext_ref[b, 0:HIST_ROWS, :] = hist_ref[0 if hist_shared else b]

        @pl.when(t > 0)
        def _():
            ext_ref[b, 0:HIST_ROWS, :] = ext_ref[b, tm:tm + HIST_ROWS, :]

    def row_range(r0, rs):
        rows = nb * rs
        row_sl = slice(r0, r0 + rs)
        x = x_ref[:, row_sl, :].reshape(rows, d_model)
        h = _rms(x, gmix_ref[...]).astype(_BF16)
        z = _dot(h, win_ref[...])
        u = z[:, :POOL_WIDTH]
        q_lat = z[:, POOL_WIDTH:POOL_WIDTH + Q_RANK]
        kv_lat = z[:, POOL_WIDTH + Q_RANK:POOL_WIDTH + Q_RANK + KV_RANK]
        kpe = z[:, POOL_WIDTH + Q_RANK + KV_RANK:POOL_WIDTH + Q_RANK + KV_RANK + LANES]

        first = HIST_ROWS + r0
        for b in range(nb):
            ext_ref[b, first:first + rs, :] = u[b * rs:(b + 1) * rs]
        for b in range(nb):
            cols = []
            for g, w in enumerate(POOL_WINDOWS):
                sl = slice(g * POOL_GROUP, (g + 1) * POOL_GROUP)
                cur = ext_ref[b, first:first + rs, sl]
                acc = cur
                for j in range(1, w):
                    acc = acc + ext_ref[b, first - j:first - j + rs, sl]
                if truncated:
                    row = t * tm + r0 + lax.broadcasted_iota(jnp.int32, (rs, 1), 0)
                    mean = acc / jnp.minimum(row + 1, w).astype(_F32)
                else:
                    mean = acc / float(w)
                cols.append(_dot((mean - cur).astype(_BF16), wpool_ref[g]))
            pool = jnp.concatenate(cols, axis=1) * pscale_ref[...]
            pool_ref[b, row_sl, :] = pool.astype(pool_ref.dtype)

        qn = _rms(q_lat, gq_ref[...]).astype(_BF16)
        c_kv = _rms(kv_lat, gkv_ref[...])
        into_results = len(lat_ref.shape) == 4
        lat_view = lat_ref.at[0] if into_results else lat_ref
        krope_view = krope_ref.at[0] if into_results else krope_ref
        lat_view[:, row_sl, :] = c_kv.reshape(nb, rs, KV_RANK)
        ckv_b = c_kv.astype(_BF16)
        knope = _dot(ckv_b, wuk_ref[...])

        def per_row(ref):
            tab = ref[row_sl, :]
            return tab if nb == 1 else jnp.concatenate([tab] * nb, axis=0)

        lane = lax.broadcasted_iota(jnp.int32, (1, LANES), 1)
        half = D_ROPE // 2
        partner = jnp.where(lane < half, pltpu.roll(kpe, LANES - half, 1), pltpu.roll(kpe, half, 1))
        krope = kpe * per_row(cosk_ref) + partner * per_row(sink_ref)
        krope_view[:, row_sl, :] = krope[:, :D_ROPE].reshape(nb, rs, D_ROPE)
        kpe_placed = pltpu.roll(krope, D_NOPE, 1)

        for hd in range(N_HEADS):
            kh = _head_norm(knope[:, hd * LANES:(hd + 1) * LANES] + kpe_placed, kgain_ref[...])
            k_ref[:, hd, row_sl, :] = kh.astype(k_ref.dtype).reshape(nb, rs, LANES)

        if feature_major:
            half = D_ROPE // 2
            qraw_t = _dot_nt(wuq_ref[...], qn)
            v_t = _dot_nt(wuv_ref[...], ckv_b)
            ones = jnp.ones((V_ROWS - D_V, rows), _F32)
            for hd in range(N_HEADS):
                v_h = jnp.concatenate([v_t[hd * D_V:(hd + 1) * D_V], ones], axis=0)
                v_ref[0, hd, 0, :, row_sl] = v_h.astype(v_ref.dtype)
            cos_t, sin_t, gain = cost_ref[:, row_sl], sint_ref[:, row_sl], qgain_ref[...]
            for hd in range(N_HEADS):
                base = hd * LANES
                nope = qraw_t[base:base + D_NOPE]
                x1 = qraw_t[base + D_NOPE:base + D_NOPE + half]
                x2 = qraw_t[base + D_NOPE + half:base + D_QK]
                r1 = x1 * cos_t - x2 * sin_t
                r2 = x1 * sin_t + x2 * cos_t
                ss = (jnp.sum(nope * nope, axis=0, keepdims=True) + jnp.sum(r1 * r1, axis=0, keepdims=True)
                      + jnp.sum(r2 * r2, axis=0, keepdims=True))
                scale = lax.rsqrt(ss / D_QK + EPS)
                qh = jnp.concatenate([nope * scale * gain[:D_NOPE], r1 * scale * gain[D_NOPE:D_NOPE + half],
                                      r2 * scale * gain[D_NOPE + half:D_QK],
                                      jnp.zeros((LANES - D_QK, rows), _F32)], axis=0)
                q_ref[0, hd, :, row_sl] = qh.astype(q_ref.dtype)
        else:
            cosp, sinp = per_row(cosp_ref), per_row(sinp_ref)
            qraw = _dot(qn, wuq_ref[...])
            vfull = _dot(ckv_b, wuv_ref[...])
            lane = lax.broadcasted_iota(jnp.int32, (1, LANES), 1)
            first_half = lane < D_NOPE + D_ROPE // 2
            for hd in range(N_HEADS):
                sl = slice(hd * LANES, (hd + 1) * LANES)
                qh = qraw[:, sl]
                partner = jnp.where(first_half, pltpu.roll(qh, LANES - D_ROPE // 2, 1),
                                    pltpu.roll(qh, D_ROPE // 2, 1))
                qh = _head_norm(qh * cosp + partner * sinp, qgain_ref[...])
                q_ref[:, hd, row_sl, :] = qh.astype(q_ref.dtype).reshape(nb, rs, LANES)
                v_ref[:, hd, row_sl, :] = vfull[:, sl].astype(v_ref.dtype).reshape(nb, rs, LANES)

    for sp in range(splits):
        row_range(sp * (tm // splits), tm // splits)
    for b in range(nb):
        ulast_ref[b] = ext_ref[b, tm:tm + HIST_ROWS, :]


def _const_spec(a):
    nd = a.ndim
    return pl.BlockSpec(a.shape, lambda *_: (0,) * nd, pipeline_mode=pl.Buffered(1))


def _skip_refs(fn, start, count):
    def wrapped(*refs, **kw):
        return fn(*refs[:start], *refs[start + count:], **kw)
    return wrapped


def _pre_call(x, hist, tabs, lw, *, nb, hist_shared, truncated, feature_major=False, into=None):
    bsz, t_len, d_model = x.shape
    tm = t_len if nb > 1 else min(ROW_TILE, t_len)
    assert bsz % nb == 0 and t_len % tm == 0 and tm % HIST_ROWS == 0 and not (feature_major and nb > 1)
    grid = (bsz // nb, t_len // tm)
    fm = "_t" if feature_major else ""
    weights = (lw["gmix"], lw["w_in"], lw["gq"], lw["w_uq" + fm], lw["gkv"], lw["w_uk"], lw["w_uv" + fm],
               lw["qgain" + fm], lw["kgain"], lw["w_pool"], lw["pscale"])
    hist_spec = (pl.BlockSpec((1, HIST_ROWS, POOL_WIDTH), lambda b, t: (0, 0, 0)) if hist_shared
                 else pl.BlockSpec((nb, HIST_ROWS, POOL_WIDTH), lambda b, t: (b, 0, 0)))
    in_specs = [pl.BlockSpec((nb, tm, d_model), lambda b, t: (b, t, 0)), hist_spec]
    in_specs += [pl.BlockSpec((tm, a.shape[1]), lambda b, t: (t, 0)) for a in tabs[:4]]
    in_specs += [pl.BlockSpec((a.shape[0], tm), lambda b, t: (0, t)) for a in tabs[4:]]
    in_specs += [_const_spec(w) for w in weights]
    head_spec = pl.BlockSpec((nb, N_HEADS, tm, LANES), lambda b, t: (b, 0, t, 0))
    head_shape = jax.ShapeDtypeStruct((bsz, N_HEADS, t_len, LANES), _BF16)
    if feature_major:
        q_spec = pl.BlockSpec((1, N_HEADS, LANES, tm), lambda b, t: (b, 0, 0, t))
        q_shape = jax.ShapeDtypeStruct((bsz, N_HEADS, LANES, t_len), _BF16)
        v_spec = pl.BlockSpec((1, N_HEADS, 1, V_ROWS, tm), lambda b, t: (b, 0, t, 0, 0))
        v_shape = jax.ShapeDtypeStruct((bsz, N_HEADS, t_len // tm, V_ROWS, tm), _BF16)
    else:
        q_spec, q_shape, v_spec, v_shape = head_spec, head_shape, head_spec, head_shape

    def row_out(width, dtype):
        return (pl.BlockSpec((nb, tm, width), lambda b, t: (b, t, 0)),
                jax.ShapeDtypeStruct((bsz, t_len, width), dtype))

    body = functools.partial(_pre_kernel, nb=nb, tm=tm, hist_shared=hist_shared, truncated=truncated,
                             feature_major=feature_major, splits=PRE_SPLITS if tm == ROW_TILE else 1)
    operands = [x, hist, *tabs, *weights]
    aliases = {}
    if into is None:
        lat_spec, lat_shape = row_out(KV_RANK, _F32)
        kr_spec, kr_shape = row_out(D_ROPE, _F32)
    else:
        layer, depth, bufs = into
        assert nb == 1

        def seq_out(width):
            block = tuple(pl.Element(n) for n in (1, 1, tm, width))
            return (pl.BlockSpec(block, lambda b, t: (layer, b, (1 + t * (tm // N_META)) * N_META, 0)),
                    jax.ShapeDtypeStruct((depth, bsz, N_META + t_len, width), _F32))

        lat_spec, lat_shape = seq_out(KV_RANK)
        kr_spec, kr_shape = seq_out(D_ROPE)
        body = _skip_refs(body, len(operands), len(bufs))
        aliases = {len(operands): 3, len(operands) + 1: 4}
        in_specs += [pl.BlockSpec(memory_space=pl.ANY)] * len(bufs)
        operands += list(bufs)
    pool_spec, pool_shape = row_out(POOL_WIDTH, _BF16)
    out_specs = [q_spec, head_spec, v_spec, lat_spec, kr_spec, pool_spec,
                 pl.BlockSpec((nb, HIST_ROWS, POOL_WIDTH), lambda b, t: (b, 0, 0))]
    out_shape = [q_shape, head_shape, v_shape, lat_shape, kr_shape, pool_shape,
                 jax.ShapeDtypeStruct((bsz, HIST_ROWS, POOL_WIDTH), _F32)]
    q, k, v, lat, krope, pool, ulast = pl.pallas_call(
        body, grid=grid, in_specs=in_specs, out_specs=out_specs, out_shape=out_shape,
        input_output_aliases=aliases,
        scratch_shapes=[pltpu.VMEM((nb, HIST_ROWS + tm, POOL_WIDTH), _F32)],
        compiler_params=pltpu.CompilerParams(dimension_semantics=("arbitrary", "arbitrary"),
                                             vmem_limit_bytes=VMEM_LIMIT_BYTES),
        name="pre_mixer",
    )(*operands)
    return dict(q=q, k=k, v=v, lat=lat, krope=krope, pool=pool, ulast=ulast)


def _meta_rows_kernel(lat_any, kr_any, mlat_ref, mkr_ref, lat_ref, kr_ref):
    del lat_any, kr_any
    lat_ref[0, 0] = mlat_ref[0]
    kr_ref[0, 0] = mkr_ref[0]


def _meta_rows_call(lat_all, kr_all, meta_lat, meta_kr):
    depth, bsz = lat_all.shape[:2]

    def src(width):
        return pl.BlockSpec((1, N_META, width), lambda l, b: (l, 0, 0))

    def dst(width):
        return pl.BlockSpec((1, 1, N_META, width), lambda l, b: (l, b, 0, 0))

    return pl.pallas_call(
        _meta_rows_kernel, grid=(depth, bsz),
        in_specs=[pl.BlockSpec(memory_space=pl.ANY)] * 2 + [src(KV_RANK), src(D_ROPE)],
        out_specs=[dst(KV_RANK), dst(D_ROPE)],
        out_shape=[jax.ShapeDtypeStruct(lat_all.shape, lat_all.dtype), jax.ShapeDtypeStruct(kr_all.shape, kr_all.dtype)],
        input_output_aliases={0: 0, 1: 1},
        compiler_params=pltpu.CompilerParams(dimension_semantics=("arbitrary", "arbitrary")),
        name="meta_rows",
    )(lat_all, kr_all, meta_lat, meta_kr)


def _post_kernel(x_ref, pool_ref, attn_ref, wop_ref, woa_ref, gffn_ref, wg_ref, wu_ref, wd_ref, o_ref):
    x1 = x_ref[...] + _dot(pool_ref[...], wop_ref[...]) + _dot(attn_ref[...], woa_ref[...])
    hn = _rms(x1, gffn_ref[...]).astype(_BF16)
    g = _dot(hn, wg_ref[...])
    up = _dot(hn, wu_ref[...])
    act = (g * jax.nn.sigmoid(g) * up).astype(_BF16)
    o_ref[...] = x1 + _dot(act, wd_ref[...])


def _post_call(x, pool, attn, lw):
    shape = x.shape
    d_model = shape[-1]
    x2 = x.reshape(-1, d_model)
    n = x2.shape[0]
    tm = min(ROW_TILE, n)
    assert n % tm == 0
    weights = (lw["w_o_pool"], lw["w_o_attn"], lw["gffn"], lw["w_gate"], lw["w_up"], lw["w_down"])

    def rows(width):
        return pl.BlockSpec((tm, width), lambda i: (i, 0))

    out = pl.pallas_call(
        _post_kernel, grid=(n // tm,),
        in_specs=[rows(d_model), rows(POOL_WIDTH), rows(N_HEADS * D_V)] + [_const_spec(w) for w in weights],
        out_specs=rows(d_model), out_shape=jax.ShapeDtypeStruct((n, d_model), _F32),
        compiler_params=pltpu.CompilerParams(dimension_semantics=("arbitrary",),
                                             vmem_limit_bytes=VMEM_LIMIT_BYTES),
        name="post_ffn",
    )(x2, pool.reshape(n, POOL_WIDTH), attn.reshape(n, N_HEADS * D_V), *weights)
    return out.reshape(shape)


def _attn_kernel(kbound_ref, q_ref, q_next_ref, k_ref, v_ref, km_ref, vm_ref, o_ref,
                 acc_ref, m_ref, s_meta_ref, s_first_ref, s_even_ref, s_odd_ref, *, tile, online):
    qi = pl.program_id(2)

    def scores(hh, k_rows, queries_ref=q_ref):
        return _dot(k_rows, queries_ref[0, hh])

    def consume(hh, s_t, v_cols, mask):
        if mask is not None:
            s_t = jnp.where(mask, s_t, MASKED)
        m_prev = m_ref[hh]
        m_next = jnp.maximum(m_prev, jnp.max(s_t, axis=0, keepdims=True)) if online else m_prev
        p_t = jnp.exp2(s_t - m_next).astype(_BF16)
        pv = _dot(v_cols, p_t)
        if online:
            acc_ref[hh] = jnp.exp2(m_prev - m_next) * acc_ref[hh] + pv
            m_ref[hh] = m_next
        else:
            acc_ref[hh] += pv

    def key_tile(hh, j):
        return k_ref[0, hh, pl.ds(pl.multiple_of(j * tile, tile), tile), :]

    def value_tile(hh, j):
        return v_ref[0, hh, j]

    heads = range(q_ref.shape[1])
    for hh in heads:
        acc_ref[hh] = jnp.zeros(acc_ref.shape[1:], _F32)
        if online:
            m_ref[hh] = jnp.full((1, tile), MASKED, _F32)
        else:
            qf = q_ref[0, hh].astype(_F32)
            m_ref[hh] = jnp.sqrt(jnp.sum(qf * qf, axis=0, keepdims=True)) * kbound_ref[...]
        s_meta_ref[hh] = scores(hh, km_ref[hh])

    @pl.when(qi == 0)
    def _():
        for hh in heads:
            s_even_ref[hh] = scores(hh, key_tile(hh, 0))

    def step(j, s_cur_ref, s_next_ref):
        for hh in heads:
            s_next_ref[hh] = scores(hh, key_tile(hh, j + 1))
            consume(hh, s_cur_ref[hh], value_tile(hh, j), None)

    @pl.when(qi >= 1)
    def _():
        step(0, s_first_ref, s_odd_ref)

    def two_steps(j):
        step(j, s_odd_ref, s_even_ref)
        step(j + 1, s_even_ref, s_odd_ref)

    def four_steps(jj, carry):
        two_steps(4 * jj + 1)
        two_steps(4 * jj + 3)
        return carry

    n_full = jnp.maximum(qi - 1, 0)
    rest = n_full % 4
    lax.fori_loop(0, n_full // 4, four_steps, 0)

    @pl.when(rest >= 2)
    def _():
        two_steps(n_full - rest + 1)

    @pl.when(rest % 2 == 1)
    def _():
        step(n_full, s_odd_ref, s_even_ref)

    key_chunk = lax.broadcasted_iota(jnp.int32, (tile, tile), 0) // CHUNK
    query_chunk = lax.broadcasted_iota(jnp.int32, (tile, tile), 1) // CHUNK
    meta_mask = lax.broadcasted_iota(jnp.int32, (km_ref.shape[1], tile), 0) < N_META
    mask = jnp.concatenate([query_chunk >= key_chunk, meta_mask], axis=0)
    for parity, s_fin_ref in enumerate((s_even_ref, s_odd_ref)):
        @pl.when(qi % 2 == parity)
        def _():
            for hh in heads:
                s_first_ref[hh] = scores(hh, key_tile(hh, 0), q_next_ref)
                s_t = jnp.concatenate([s_fin_ref[hh], s_meta_ref[hh]], axis=0)
                v_cols = jnp.concatenate([value_tile(hh, qi), vm_ref[hh]], axis=1)
                consume(hh, s_t, v_cols, mask)

    out_t = jnp.concatenate([acc_ref[hh, :D_V] / acc_ref[hh, D_V:D_V + 1] for hh in heads], axis=0)
    o_ref[0] = out_t.T.astype(o_ref.dtype)


def _prompt_attn_call(kbound, q, k, v, km, vm, *, online):
    bsz, n_heads, _, s_len = q.shape
    tile = v.shape[-1]
    n_tiles = s_len // tile
    hps = ATTN_HEADS_PER_STEP
    assert s_len % tile == 0 and tile % CHUNK == 0 and n_heads % hps == 0 and (hps * D_V) % LANES == 0
    grid = (bsz, n_heads // hps, n_tiles)
    return pl.pallas_call(
        functools.partial(_attn_kernel, tile=tile, online=online), grid=grid,
        in_specs=[pl.BlockSpec((1, 1), lambda b, p, i: (0, 0)),
                  pl.BlockSpec((1, hps, LANES, tile), lambda b, p, i: (b, p, 0, i)),
                  pl.BlockSpec((1, hps, LANES, tile), lambda b, p, i: (b, p, 0, jnp.minimum(i + 1, n_tiles - 1))),
                  pl.BlockSpec((1, hps, s_len, LANES), lambda b, p, i: (b, p, 0, 0)),
                  pl.BlockSpec((1, hps, n_tiles, V_ROWS, tile), lambda b, p, i: (b, p, 0, 0, 0)),
                  pl.BlockSpec((hps, LANES, LANES), lambda b, p, i: (p, 0, 0)),
                  pl.BlockSpec((hps, V_ROWS, LANES), lambda b, p, i: (p, 0, 0))],
        out_specs=pl.BlockSpec((1, tile, hps * D_V), lambda b, p, i: (b, i, p)),
        out_shape=jax.ShapeDtypeStruct((bsz, s_len, n_heads * D_V), _BF16),
        scratch_shapes=[pltpu.VMEM((hps, V_ROWS, tile), _F32), pltpu.VMEM((hps, 1, tile), _F32),
                        pltpu.VMEM((hps, LANES, tile), _F32)]
                       + [pltpu.VMEM((hps, tile, tile), _F32)] * 3,
        compiler_params=pltpu.CompilerParams(dimension_semantics=("arbitrary",) * 3,
                                             vmem_limit_bytes=VMEM_LIMIT_BYTES),
        name="prompt_attn",
    )(kbound, q, q, k, v, km, vm)


def _meta_attn_kernel(q_ref, k_ref, v_ref, o_ref):
    for pair in range(N_HEADS // 2):
        out = None
        for hd in (2 * pair, 2 * pair + 1):
            s = _dot_nt(q_ref[hd], k_ref[hd])
            p = jnp.exp2(s - jnp.max(s, axis=1, keepdims=True))
            o = _dot(p.astype(_BF16), v_ref[hd]) / jnp.sum(p, axis=1, keepdims=True)
            out = o if out is None else out + o
        o_ref[:, pair * LANES:(pair + 1) * LANES] = out.astype(o_ref.dtype)


def _meta_attn_call(q, k, v):
    n_rows = q.shape[1]
    return pl.pallas_call(
        _meta_attn_kernel, out_shape=jax.ShapeDtypeStruct((n_rows, N_HEADS * D_V), _BF16),
        name="meta_attn",
    )(q, k, v)


def _row_sums_t(x):
    ones = jnp.ones((8, x.shape[1]), _BF16)
    hi = x.astype(_BF16)
    lo = (x - hi.astype(_F32)).astype(_BF16)
    return (_dot_nt(ones, hi) + _dot_nt(ones, lo))[0:1]


def _sample_attn_kernel(q_ref, kn_ref, vn_ref, lat_ref, kpe_ref, wukt_ref, wukq_ref, wuv_ref, kgain_ref, place_ref,
                        o_ref, *, past):
    t_len = q_ref.shape[2]
    kgain = kgain_ref[...]
    qg = [(q_ref[0, hd].astype(_F32) * kgain).astype(_BF16) for hd in range(N_HEADS)]
    q_all = jnp.concatenate(qg, axis=0)
    q_lat = jnp.concatenate([_dot(qg[hd], wukq_ref[hd]) for hd in range(N_HEADS)], axis=0).astype(_BF16)
    q_rope = _dot_nt(q_all, place_ref[...]).astype(_BF16)

    def cached_scores(lo, hi):
        lat = lat_ref[0, 0, lo:hi].astype(_BF16)
        kpe = kpe_ref[0, lo:hi]
        s = _dot_nt(q_lat, lat) + _dot_nt(q_rope, kpe.astype(_BF16))
        ss_rope = _row_sums_t(kpe * kpe)
        rows = []
        for pair in range(N_HEADS // 2):
            k_t = _dot_nt(wukt_ref[pair * LANES:(pair + 1) * LANES], lat)
            for hh in range(2):
                hd = 2 * pair + hh
                k_h = k_t[hh * D_NOPE:(hh + 1) * D_NOPE]
                r = lax.rsqrt((jnp.sum(k_h * k_h, axis=0, keepdims=True) + ss_rope) / D_QK + EPS)
                rows.append(s[hd * t_len:(hd + 1) * t_len] * r)
        return jnp.concatenate(rows, axis=0), lat

    block = min(SAMPLE_KEY_BLOCK, past)
    bounds = [(0, N_META)] + [(lo, lo + block) for lo in range(N_META, N_META + past, block)]
    m = jnp.full((N_HEADS * t_len, 1), MASKED, _F32)
    den = jnp.zeros((N_HEADS * t_len, 1), _F32)
    p_lat = jnp.zeros((N_HEADS * t_len, KV_RANK), _F32)
    for lo, hi in bounds:
        s, lat = cached_scores(lo, hi)
        m_next = jnp.maximum(m, jnp.max(s, axis=1, keepdims=True))
        alpha = jnp.exp2(m - m_next)
        p = jnp.exp2(s - m_next)
        den = alpha * den + jnp.sum(p, axis=1, keepdims=True)
        p_lat = alpha * p_lat + _dot(p.astype(_BF16), lat)
        m = m_next
    tq = lax.broadcasted_iota(jnp.int32, (t_len, t_len), 0)
    tk = lax.broadcasted_iota(jnp.int32, (t_len, t_len), 1)
    new_mask = (past + tq) // CHUNK >= (past + tk) // CHUNK
    s_new = jnp.concatenate([jnp.where(new_mask, _dot_nt(q_ref[0, hd], kn_ref[0, hd]), MASKED)
                             for hd in range(N_HEADS)], axis=0)
    m_next = jnp.maximum(m, jnp.max(s_new, axis=1, keepdims=True))
    alpha = jnp.exp2(m - m_next)
    p_new = jnp.exp2(s_new - m_next)
    den = alpha * den + jnp.sum(p_new, axis=1, keepdims=True)
    p_lat = (alpha * p_lat).astype(_BF16)
    p_new = p_new.astype(_BF16)
    for pair in range(N_HEADS // 2):
        out = None
        for hd in (2 * pair, 2 * pair + 1):
            rows = slice(hd * t_len, (hd + 1) * t_len)
            o = (_dot(p_lat[rows], wuv_ref[:, hd * LANES:(hd + 1) * LANES]) + _dot(p_new[rows], vn_ref[0, hd])) / den[rows]
            out = o if out is None else out + o
        o_ref[0, :, pair * LANES:(pair + 1) * LANES] = out.astype(o_ref.dtype)


def _sample_attn_call(layer, q, k_new, v_new, cache_lat, cache_kpe, lw, consts):
    bsz, n_heads, t_len, _ = q.shape
    l_cache = cache_lat.shape[2]
    past = l_cache - N_META
    assert past % min(SAMPLE_KEY_BLOCK, past) == 0 and past % LANES == 0 and t_len % HIST_ROWS == 0
    head_spec = pl.BlockSpec((1, n_heads, t_len, LANES), lambda b: (b, 0, 0, 0))
    weights = (lw["w_uk_t"], lw["w_uk_q"], lw["w_uv"], lw["kgain"], consts["place"])
    return pl.pallas_call(
        functools.partial(_sample_attn_kernel, past=past), grid=(bsz,),
        in_specs=[head_spec, head_spec, head_spec,
                  pl.BlockSpec((1, 1, l_cache, KV_RANK), lambda b: (layer, b, 0, 0)),
                  pl.BlockSpec((1, l_cache, D_ROPE), lambda b: (b, 0, 0))]
                 + [_const_spec(w) for w in weights],
        out_specs=pl.BlockSpec((1, t_len, n_heads * D_V), lambda b: (b, 0, 0)),
        out_shape=jax.ShapeDtypeStruct((bsz, t_len, n_heads * D_V), _BF16),
        compiler_params=pltpu.CompilerParams(dimension_semantics=("arbitrary",),
                                             vmem_limit_bytes=VMEM_LIMIT_BYTES),
        name="sample_attn",
    )(q, k_new, v_new, cache_lat, cache_kpe, *weights)


def _pack_heads(w, width, alternate=False):
    kdim = w.shape[0]
    w3 = w.reshape(kdim, N_HEADS, width)
    low = jnp.pad(w3, ((0, 0), (0, 0), (0, LANES - width)))
    if alternate:
        high = jnp.pad(w3, ((0, 0), (0, 0), (LANES - width, 0)))
        odd = (jnp.arange(N_HEADS) % 2 == 1)[None, :, None]
        low = jnp.where(odd, high, low)
    return low.reshape(kdim, N_HEADS * LANES)


def _layer_weights(l, norm_mix, w_in, q_a_norm, w_uq, kv_a_norm, w_uk, w_uv, q_norm, k_norm,
                   w_pool, pool_scale, w_o, norm_ffn, w_gate, w_up, w_down):
    d_in = w_in.shape[2]
    d_in_pad = -(-d_in // (2 * LANES)) * (2 * LANES)
    pad96 = lambda g: jnp.pad(g.astype(_F32), (0, LANES - D_QK))[None, :]
    q_scale = D_QK ** -0.5 * math.log2(math.e)
    return dict(
        gmix=norm_mix[l][None, :].astype(_F32),
        w_in=jnp.pad(w_in[l], ((0, 0), (0, d_in_pad - d_in))).astype(_BF16),
        gq=q_a_norm[l][None, :].astype(_F32),
        w_uq=_pack_heads(w_uq[l], D_QK).astype(_BF16),
        w_uq_t=_pack_heads(w_uq[l], D_QK).astype(_BF16).T,
        gkv=kv_a_norm[l][None, :].astype(_F32),
        w_uk=_pack_heads(w_uk[l], D_NOPE).astype(_BF16),
        w_uk_t=w_uk[l].astype(_BF16).T,
        w_uk_q=jnp.pad(w_uk[l].astype(_BF16).T.reshape(N_HEADS, D_NOPE, KV_RANK),
                       ((0, 0), (0, LANES - D_NOPE), (0, 0))),
        w_uv=_pack_heads(w_uv[l], D_V, alternate=True).astype(_BF16),
        w_uv_t=w_uv[l].astype(_BF16).T,
        qgain=pad96(q_norm[l]) * q_scale,
        qgain_t=(pad96(q_norm[l]) * q_scale).T,
        kgain=pad96(k_norm[l]),
        w_pool=w_pool[l].astype(_BF16),
        pscale=pool_scale[l][None, :].astype(_F32),
        w_o_pool=w_o[l][:POOL_WIDTH].astype(_BF16),
        w_o_attn=w_o[l][POOL_WIDTH:].astype(_BF16),
        gffn=norm_ffn[l][None, :].astype(_F32),
        w_gate=w_gate[l].astype(_BF16),
        w_up=w_up[l].astype(_BF16),
        w_down=w_down[l].astype(_BF16),
    )


def _rope_tables(pos0, t_len):
    half = D_ROPE // 2
    inv = ROPE_THETA ** (-jnp.arange(half, dtype=_F32) / half)
    ang = (pos0 + jnp.arange(t_len, dtype=jnp.int32)).astype(_F32)[:, None] * inv[None, :]
    cos, sin = jnp.cos(ang), jnp.sin(ang)
    cos32 = jnp.concatenate([cos, cos], axis=1)
    sin32 = jnp.concatenate([-sin, sin], axis=1)
    ones = jnp.ones((t_len, D_NOPE), _F32)
    zeros = jnp.zeros((t_len, D_NOPE), _F32)
    tail1 = jnp.ones((t_len, LANES - D_QK), _F32)
    tail0 = jnp.zeros((t_len, LANES - D_QK), _F32)
    cosp = jnp.concatenate([ones, cos32, tail1], axis=1)
    sinp = jnp.concatenate([zeros, sin32, tail0], axis=1)
    cosk = jnp.concatenate([cos32, ones, tail1], axis=1)
    sink = jnp.concatenate([sin32, zeros, tail0], axis=1)
    return (cosp, sinp, cosk, sink, cos.T, sin.T)


def _perm_consts():
    idx = jnp.arange(D_ROPE)
    place = jnp.zeros((D_ROPE, LANES), _F32).at[idx, D_NOPE + idx].set(1.0)
    return dict(place=place.astype(_BF16))


def kernel(x_prompt, x_sample, cache_latent, cache_krope, state_pool, meta_tokens, norm_mix, w_in, q_a_norm, w_uq, kv_a_norm, w_uk, w_uv, q_norm, k_norm, w_pool, pool_scale, w_o, norm_ffn, w_gate, w_up, w_down):
    depth = norm_mix.shape[0]
    bp, s_len, d_model = x_prompt.shape
    bs, t_len, _ = x_sample.shape
    past = cache_latent.shape[2] - N_META
    consts = _perm_consts()
    tabs_meta = _rope_tables(0, N_META)
    tabs_prompt = _rope_tables(N_META, s_len)
    tabs_sample = _rope_tables(N_META + past, t_len)
    nb_sample = math.gcd(bs, 8)

    xm = meta_tokens.astype(_F32)[None]
    xp = x_prompt
    xs = x_sample
    zero_hist = jnp.zeros((1, HIST_ROWS, POOL_WIDTH), _F32)
    seq_bufs = tuple(jnp.zeros((depth, bp, N_META + s_len, width), _F32) for width in (KV_RANK, D_ROPE))
    meta_lat, meta_kr, pool_p, lat_s, kpe_s, pool_s = [], [], [], [], [], []
    for l in range(depth):
        lw = _layer_weights(l, norm_mix, w_in, q_a_norm, w_uq, kv_a_norm, w_uk, w_uv, q_norm, k_norm,
                            w_pool, pool_scale, w_o, norm_ffn, w_gate, w_up, w_down)
        pm = _pre_call(xm, zero_hist, tabs_meta, lw, nb=1, hist_shared=True, truncated=True)
        pp = _pre_call(xp, pm["ulast"], tabs_prompt, lw, nb=1, hist_shared=True, truncated=False,
                       feature_major=True, into=(l, depth, seq_bufs))
        seq_bufs = (pp["lat"], pp["krope"])
        hist_s = jnp.pad(state_pool[l].astype(_F32), ((0, 0), (HIST_ROWS - POOL_STATE, 0), (0, 0)))
        ps = _pre_call(xs, hist_s, tabs_sample, lw, nb=nb_sample, hist_shared=False, truncated=False)

        km = jnp.pad(pm["k"][0], ((0, 0), (0, LANES - N_META), (0, 0)))
        vm_t = jnp.stack([pm["v"][0, hd, :, (hd % 2) * D_V:(hd % 2 + 1) * D_V].T for hd in range(N_HEADS)])
        vm_t = jnp.concatenate([vm_t, jnp.ones((N_HEADS, V_ROWS - D_V, N_META), vm_t.dtype)], axis=1)
        vm_t = jnp.pad(vm_t, ((0, 0), (0, 0), (0, LANES - N_META)))
        kbound = (D_QK ** 0.5 * jnp.max(jnp.abs(lw["kgain"]))).reshape(1, 1)
        qbound = D_QK ** 0.5 * jnp.max(jnp.abs(lw["qgain"]))
        attn_args = (kbound, pp["q"], pp["k"], pp["v"], km, vm_t)
        attn_p = lax.cond(2.0 * qbound * kbound[0, 0] <= EXP2_SAFE_RANGE,
                          functools.partial(_prompt_attn_call, online=False),
                          functools.partial(_prompt_attn_call, online=True), *attn_args)
        attn_s = _sample_attn_call(l, ps["q"], ps["k"], ps["v"], cache_latent, cache_krope[l], lw, consts)
        xp = _post_call(xp, pp["pool"], attn_p, lw)
        xs = _post_call(xs, ps["pool"], attn_s, lw)
        if l + 1 < depth:
            attn_m = _meta_attn_call(pm["q"][0], pm["k"][0], pm["v"][0])
            xm = _post_call(xm, pm["pool"], attn_m[None], lw)

        meta_lat.append(pm["lat"][0])
        meta_kr.append(pm["krope"][0])
        pool_p.append(pp["ulast"][:, HIST_ROWS - POOL_STATE:])
        lat_s.append(ps["lat"])
        kpe_s.append(ps["krope"])
        pool_s.append(ps["ulast"][:, HIST_ROWS - POOL_STATE:])
    lat_p, kpe_p = _meta_rows_call(*seq_bufs, jnp.stack(meta_lat), jnp.stack(meta_kr))
    return (xp, xs, lat_p, kpe_p, jnp.stack(pool_p), jnp.stack(lat_s), jnp.stack(kpe_s), jnp.stack(pool_s))
```

```python
import functools
import math

import jax
import jax.numpy as jnp
from jax import lax
from jax.experimental import pallas as pl
from jax.experimental.pallas import tpu as pltpu

CHUNK = 64
N_META = 16
N_HEADS = 8
D_NOPE = 64
D_ROPE = 32
D_QK = D_NOPE + D_ROPE
D_V = 64
Q_RANK = 384
KV_RANK = 256
POOL_WINDOWS = (2, 4, 8, 16)
POOL_GROUP = 128
POOL_WIDTH = POOL_GROUP * len(POOL_WINDOWS)
POOL_STATE = max(POOL_WINDOWS) - 1
ROPE_THETA = 10000.0
EPS = 1e-6

LANES = 128
HIST_ROWS = 16
POOL_PAD = 8
VMEM_LIMIT_BYTES = 56 * 1024 * 1024
ROW_TILE = 512
PRE_SPLITS = 2
ATTN_HEADS_PER_STEP = 4
SAMPLE_KEY_BLOCK = 1024
V_ROWS = D_V + 16
MASKED = -1e30
EXP2_SAFE_RANGE = 120.0

_F32 = jnp.float32
_BF16 = jnp.bfloat16
_NT = (((1,), (1,)), ((), ()))


def _dot(a, b):
    return jnp.dot(a, b, preferred_element_type=_F32)


def _dot_nt(a, b):
    return lax.dot_general(a, b, _NT, preferred_element_type=_F32)


def _rms(x, g):
    return x * lax.rsqrt(jnp.mean(x * x, axis=-1, keepdims=True) + EPS) * g


def _head_norm(x, gain):
    ss = jnp.sum(x * x, axis=-1, keepdims=True)
    return x * lax.rsqrt(ss / D_QK + EPS) * gain


def _pre_kernel(x_ref, hist_ref, cosp_ref, sinp_ref, cosk_ref, sink_ref, cost_ref, sint_ref,
                gmix_ref, win_ref, gq_ref, wuq_ref, gkv_ref, wuk_ref, wuv_ref, qgain_ref, kgain_ref,
                wpool_ref, pscale_ref,
                q_ref, k_ref, v_ref, lat_ref, krope_ref, pool_ref, ulast_ref,
                ext_ref, stage_ref, *, nb, tm, hist_shared, truncated, feature_major, splits):
    t = pl.program_id(1)
    d_model = x_ref.shape[-1]
    hist0 = POOL_PAD
    for b in range(nb):
        ext_ref[b, 0:POOL_PAD, :] = jnp.zeros((POOL_PAD, POOL_WIDTH), _F32)

        @pl.when(t == 0)
        def _():
            ext_ref[b, hist0:hist0 + HIST_ROWS, :] = hist_ref[0 if hist_shared else b]

        @pl.when(t > 0)
        def _():
            ext_ref[b, hist0:hist0 + HIST_ROWS, :] = ext_ref[b, hist0 + tm:hist0 + tm + HIST_ROWS, :]

    def row_range(sp, r0, rs):
        rows = nb * rs
        row_sl = slice(r0, r0 + rs)
        x = x_ref[:, row_sl, :].reshape(rows, d_model)
        h = _rms(x, gmix_ref[...]).astype(_BF16)
        z = _dot(h, win_ref[...])
        u = z[:, :POOL_WIDTH]
        q_lat = z[:, POOL_WIDTH:POOL_WIDTH + Q_RANK]
        kv_lat = z[:, POOL_WIDTH + Q_RANK:POOL_WIDTH + Q_RANK + KV_RANK]
        kpe = z[:, POOL_WIDTH + Q_RANK + KV_RANK:POOL_WIDTH + Q_RANK + KV_RANK + LANES]

        first = hist0 + HIST_ROWS + r0
        lo, span = first - HIST_ROWS, rs + HIST_ROWS
        for b in range(nb):
            ext_ref[b, first:first + rs, :] = u[b * rs:(b + 1) * rs]
        for b in range(nb):
            own = ext_ref[b, lo:lo + span, :]
            part = own + ext_ref[b, lo - 1:lo - 1 + span, :]
            window_sums = [part[:, :POOL_GROUP]]
            for k, w in enumerate(POOL_WINDOWS[:-1]):
                assert POOL_WINDOWS[k + 1] == 2 * w and w <= POOL_PAD
                c0 = (k + 1) * POOL_GROUP
                part = part[:, POOL_GROUP:]
                buf = stage_ref.at[sp, b, k]
                buf[0:POOL_PAD, c0:] = jnp.zeros((POOL_PAD, POOL_WIDTH - c0), _F32)
                buf[POOL_PAD:POOL_PAD + span, c0:] = part
                part = part + buf[POOL_PAD - w:POOL_PAD - w + span, c0:]
                window_sums.append(part[:, :POOL_GROUP])
            cols = []
            for g, w in enumerate(POOL_WINDOWS):
                cur = own[HIST_ROWS:, g * POOL_GROUP:(g + 1) * POOL_GROUP]
                acc = window_sums[g][HIST_ROWS:]
                if truncated:
                    row = t * tm + r0 + lax.broadcasted_iota(jnp.int32, (rs, 1), 0)
                    mean = acc / jnp.minimum(row + 1, w).astype(_F32)
                else:
                    mean = acc / float(w)
                cols.append(_dot((mean - cur).astype(_BF16), wpool_ref[g]))
            pool = jnp.concatenate(cols, axis=1) * pscale_ref[...]
            pool_ref[b, row_sl, :] = pool.astype(pool_ref.dtype)

        qn = _rms(q_lat, gq_ref[...]).astype(_BF16)
        c_kv = _rms(kv_lat, gkv_ref[...])
        into_results = len(lat_ref.shape) == 4
        lat_view = lat_ref.at[0] if into_results else lat_ref
        krope_view = krope_ref.at[0] if into_results else krope_ref
        lat_view[:, row_sl, :] = c_kv.reshape(nb, rs, KV_RANK)
        ckv_b = c_kv.astype(_BF16)
        knope = _dot(ckv_b, wuk_ref[...])

        def per_row(ref):
            tab = ref[row_sl, :]
            return tab if nb == 1 else jnp.concatenate([tab] * nb, axis=0)

        lane = lax.broadcasted_iota(jnp.int32, (1, LANES), 1)
        half = D_ROPE // 2
        partner = jnp.where(lane < half, pltpu.roll(kpe, LANES - half, 1), pltpu.roll(kpe, half, 1))
        krope = kpe * per_row(cosk_ref) + partner * per_row(sink_ref)
        krope_view[:, row_sl, :] = krope[:, :D_ROPE].reshape(nb, rs, D_ROPE)
        kpe_placed = pltpu.roll(krope, D_NOPE, 1)

        for hd in range(N_HEADS):
            kh = _head_norm(knope[:, hd * LANES:(hd + 1) * LANES] + kpe_placed, kgain_ref[...])
            k_ref[:, hd, row_sl, :] = kh.astype(k_ref.dtype).reshape(nb, rs, LANES)

        if feature_major:
            half = D_ROPE // 2
            qraw_t = _dot_nt(wuq_ref[...], qn)
            v_t = _dot_nt(wuv_ref[...], ckv_b)
            ones = jnp.ones((V_ROWS - D_V, rows), _F32)
            for hd in range(N_HEADS):
                v_h = jnp.concatenate([v_t[hd * D_V:(hd + 1) * D_V], ones], axis=0)
                v_ref[0, hd, 0, :, row_sl] = v_h.astype(v_ref.dtype)
            cos_t, sin_t, gain = cost_ref[:, row_sl], sint_ref[:, row_sl], qgain_ref[...]
            for hd in range(N_HEADS):
                base = hd * LANES
                nope = qraw_t[base:base + D_NOPE]
                x1 = qraw_t[base + D_NOPE:base + D_NOPE + half]
                x2 = qraw_t[base + D_NOPE + half:base + D_QK]
                r1 = x1 * cos_t - x2 * sin_t
                r2 = x1 * sin_t + x2 * cos_t
                ss = (jnp.sum(nope * nope, axis=0, keepdims=True) + jnp.sum(r1 * r1, axis=0, keepdims=True)
                      + jnp.sum(r2 * r2, axis=0, keepdims=True))
                scale = lax.rsqrt(ss / D_QK + EPS)
                qh = jnp.concatenate([nope * scale * gain[:D_NOPE], r1 * scale * gain[D_NOPE:D_NOPE + half],
                                      r2 * scale * gain[D_NOPE + half:D_QK],
                                      jnp.zeros((LANES - D_QK, rows), _F32)], axis=0)
                q_ref[0, hd, :, row_sl] = qh.astype(q_ref.dtype)
        else:
            cosp, sinp = per_row(cosp_ref), per_row(sinp_ref)
            qraw = _dot(qn, wuq_ref[...])
            vfull = _dot(ckv_b, wuv_ref[...])
            lane = lax.broadcasted_iota(jnp.int32, (1, LANES), 1)
            first_half = lane < D_NOPE + D_ROPE // 2
            for hd in range(N_HEADS):
                sl = slice(hd * LANES, (hd + 1) * LANES)
                qh = qraw[:, sl]
                partner = jnp.where(first_half, pltpu.roll(qh, LANES - D_ROPE // 2, 1),
                                    pltpu.roll(qh, D_ROPE // 2, 1))
                qh = _head_norm(qh * cosp + partner * sinp, qgain_ref[...])
                q_ref[:, hd, row_sl, :] = qh.astype(q_ref.dtype).reshape(nb, rs, LANES)
                v_ref[:, hd, row_sl, :] = vfull[:, sl].astype(v_ref.dtype).reshape(nb, rs, LANES)

    assert POOL_WINDOWS[0] == 2
    for sp in range(splits):
        row_range(sp, sp * (tm // splits), tm // splits)
    for b in range(nb):
        ulast_ref[b] = ext_ref[b, hist0 + tm:hist0 + tm + HIST_ROWS, :]


def _const_spec(a):
    nd = a.ndim
    return pl.BlockSpec(a.shape, lambda *_: (0,) * nd, pipeline_mode=pl.Buffered(1))


def _skip_refs(fn, start, count):
    def wrapped(*refs, **kw):
        return fn(*refs[:start], *refs[start + count:], **kw)
    return wrapped


def _pre_call(x, hist, tabs, lw, *, nb, hist_shared, truncated, feature_major=False, into=None):
    bsz, t_len, d_model = x.shape
    tm = t_len if nb > 1 else min(ROW_TILE, t_len)
    assert bsz % nb == 0 and t_len % tm == 0 and tm % HIST_ROWS == 0 and not (feature_major and nb > 1)
    grid = (bsz // nb, t_len // tm)
    fm = "_t" if feature_major else ""
    weights = (lw["gmix"], lw["w_in"], lw["gq"], lw["w_uq" + fm], lw["gkv"], lw["w_uk"], lw["w_uv" + fm],
               lw["qgain" + fm], lw["kgain"], lw["w_pool"], lw["pscale"])
    hist_spec = (pl.BlockSpec((1, HIST_ROWS, POOL_WIDTH), lambda b, t: (0, 0, 0)) if hist_shared
                 else pl.BlockSpec((nb, HIST_ROWS, POOL_WIDTH), lambda b, t: (b, 0, 0)))
    in_specs = [pl.BlockSpec((nb, tm, d_model), lambda b, t: (b, t, 0)), hist_spec]
    in_specs += [pl.BlockSpec((tm, a.shape[1]), lambda b, t: (t, 0)) for a in tabs[:4]]
    in_specs += [pl.BlockSpec((a.shape[0], tm), lambda b, t: (0, t)) for a in tabs[4:]]
    in_specs += [_const_spec(w) for w in weights]
    head_spec = pl.BlockSpec((nb, N_HEADS, tm, LANES), lambda b, t: (b, 0, t, 0))
    head_shape = jax.ShapeDtypeStruct((bsz, N_HEADS, t_len, LANES), _BF16)
    if feature_major:
        q_spec = pl.BlockSpec((1, N_HEADS, LANES, tm), lambda b, t: (b, 0, 0, t))
        q_shape = jax.ShapeDtypeStruct((bsz, N_HEADS, LANES, t_len), _BF16)
        v_spec = pl.BlockSpec((1, N_HEADS, 1, V_ROWS, tm), lambda b, t: (b, 0, t, 0, 0))
        v_shape = jax.ShapeDtypeStruct((bsz, N_HEADS, t_len // tm, V_ROWS, tm), _BF16)
    else:
        q_spec, q_shape, v_spec, v_shape = head_spec, head_shape, head_spec, head_shape

    def row_out(width, dtype):
        return (pl.BlockSpec((nb, tm, width), lambda b, t: (b, t, 0)),
                jax.ShapeDtypeStruct((bsz, t_len, width), dtype))

    splits = PRE_SPLITS if tm == ROW_TILE else 1
    body = functools.partial(_pre_kernel, nb=nb, tm=tm, hist_shared=hist_shared, truncated=truncated,
                             feature_major=feature_major, splits=splits)
    operands = [x, hist, *tabs, *weights]
    aliases = {}
    if into is None:
        lat_spec, lat_shape = row_out(KV_RANK, _F32)
        kr_spec, kr_shape = row_out(D_ROPE, _F32)
    else:
        layer, depth, bufs = into
        assert nb == 1

        def seq_out(width):
            block = tuple(pl.Element(n) for n in (1, 1, tm, width))
            return (pl.BlockSpec(block, lambda b, t: (layer, b, (1 + t * (tm // N_META)) * N_META, 0)),
                    jax.ShapeDtypeStruct((depth, bsz, N_META + t_len, width), _F32))

        lat_spec, lat_shape = seq_out(KV_RANK)
        kr_spec, kr_shape = seq_out(D_ROPE)
        body = _skip_refs(body, len(operands), len(bufs))
        aliases = {len(operands): 3, len(operands) + 1: 4}
        in_specs += [pl.BlockSpec(memory_space=pl.ANY)] * len(bufs)
        operands += list(bufs)
    pool_spec, pool_shape = row_out(POOL_WIDTH, _BF16)
    out_specs = [q_spec, head_spec, v_spec, lat_spec, kr_spec, pool_spec,
                 pl.BlockSpec((nb, HIST_ROWS, POOL_WIDTH), lambda b, t: (b, 0, 0))]
    out_shape = [q_shape, head_shape, v_shape, lat_shape, kr_shape, pool_shape,
                 jax.ShapeDtypeStruct((bsz, HIST_ROWS, POOL_WIDTH), _F32)]
    q, k, v, lat, krope, pool, ulast = pl.pallas_call(
        body, grid=grid, in_specs=in_specs, out_specs=out_specs, out_shape=out_shape,
        input_output_aliases=aliases,
        scratch_shapes=[pltpu.VMEM((nb, POOL_PAD + HIST_ROWS + tm, POOL_WIDTH), _F32),
                        pltpu.VMEM((splits, nb, len(POOL_WINDOWS) - 1, POOL_PAD + HIST_ROWS + tm // splits, POOL_WIDTH),
                                   _F32)],
        compiler_params=pltpu.CompilerParams(dimension_semantics=("arbitrary", "arbitrary"),
                                             vmem_limit_bytes=VMEM_LIMIT_BYTES),
        name="pre_mixer",
    )(*operands)
    return dict(q=q, k=k, v=v, lat=lat, krope=krope, pool=pool, ulast=ulast)


def _meta_rows_kernel(lat_any, kr_any, mlat_ref, mkr_ref, lat_ref, kr_ref):
    del lat_any, kr_any
    lat_ref[0, 0] = mlat_ref[0]
    kr_ref[0, 0] = mkr_ref[0]


def _meta_rows_call(lat_all, kr_all, meta_lat, meta_kr):
    depth, bsz = lat_all.shape[:2]

    def src(width):
        return pl.BlockSpec((1, N_META, width), lambda l, b: (l, 0, 0))

    def dst(width):
        return pl.BlockSpec((1, 1, N_META, width), lambda l, b: (l, b, 0, 0))

    return pl.pallas_call(
        _meta_rows_kernel, grid=(depth, bsz),
        in_specs=[pl.BlockSpec(memory_space=pl.ANY)] * 2 + [src(KV_RANK), src(D_ROPE)],
        out_specs=[dst(KV_RANK), dst(D_ROPE)],
        out_shape=[jax.ShapeDtypeStruct(lat_all.shape, lat_all.dtype), jax.ShapeDtypeStruct(kr_all.shape, kr_all.dtype)],
        input_output_aliases={0: 0, 1: 1},
        compiler_params=pltpu.CompilerParams(dimension_semantics=("arbitrary", "arbitrary")),
        name="meta_rows",
    )(lat_all, kr_all, meta_lat, meta_kr)


def _post_kernel(x_ref, pool_ref, attn_ref, wop_ref, woa_ref, gffn_ref, wg_ref, wu_ref, wd_ref, o_ref):
    x1 = x_ref[...] + _dot(pool_ref[...], wop_ref[...]) + _dot(attn_ref[...], woa_ref[...])
    hn = _rms(x1, gffn_ref[...]).astype(_BF16)
    g = _dot(hn, wg_ref[...])
    up = _dot(hn, wu_ref[...])
    act = (g * jax.nn.sigmoid(g) * up).astype(_BF16)
    o_ref[...] = x1 + _dot(act, wd_ref[...])


def _post_call(x, pool, attn, lw):
    shape = x.shape
    d_model = shape[-1]
    x2 = x.reshape(-1, d_model)
    n = x2.shape[0]
    tm = min(ROW_TILE, n)
    assert n % tm == 0
    weights = (lw["w_o_pool"], lw["w_o_attn"], lw["gffn"], lw["w_gate"], lw["w_up"], lw["w_down"])

    def rows(width):
        return pl.BlockSpec((tm, width), lambda i: (i, 0))

    out = pl.pallas_call(
        _post_kernel, grid=(n // tm,),
        in_specs=[rows(d_model), rows(POOL_WIDTH), rows(N_HEADS * D_V)] + [_const_spec(w) for w in weights],
        out_specs=rows(d_model), out_shape=jax.ShapeDtypeStruct((n, d_model), _F32),
        compiler_params=pltpu.CompilerParams(dimension_semantics=("arbitrary",),
                                             vmem_limit_bytes=VMEM_LIMIT_BYTES),
        name="post_ffn",
    )(x2, pool.reshape(n, POOL_WIDTH), attn.reshape(n, N_HEADS * D_V), *weights)
    return out.reshape(shape)


def _attn_kernel(kbound_ref, q_ref, q_next_ref, k_ref, v_ref, km_ref, vm_ref, o_ref,
                 acc_ref, m_ref, s_meta_ref, s_first_ref, s_even_ref, s_odd_ref, *, tile, online):
    qi = pl.program_id(2)

    def scores(hh, k_rows, queries_ref=q_ref):
        return _dot(k_rows, queries_ref[0, hh])

    def consume(hh, s_t, v_cols, mask):
        if mask is not None:
            s_t = jnp.where(mask, s_t, MASKED)
        m_prev = m_ref[hh]
        m_next = jnp.maximum(m_prev, jnp.max(s_t, axis=0, keepdims=True)) if online else m_prev
        p_t = jnp.exp2(s_t - m_next).astype(_BF16)
        pv = _dot(v_cols, p_t)
        if online:
            acc_ref[hh] = jnp.exp2(m_prev - m_next) * acc_ref[hh] + pv
            m_ref[hh] = m_next
        else:
            acc_ref[hh] += pv

    def key_tile(hh, j):
        return k_ref[0, hh, pl.ds(pl.multiple_of(j * tile, tile), tile), :]

    def value_tile(hh, j):
        return v_ref[0, hh, j]

    heads = range(q_ref.shape[1])
    for hh in heads:
        acc_ref[hh] = jnp.zeros(acc_ref.shape[1:], _F32)
        if online:
            m_ref[hh] = jnp.full((1, tile), MASKED, _F32)
        else:
            qf = q_ref[0, hh].astype(_F32)
            m_ref[hh] = jnp.sqrt(jnp.sum(qf * qf, axis=0, keepdims=True)) * kbound_ref[...]
        s_meta_ref[hh] = scores(hh, km_ref[hh])

    @pl.when(qi == 0)
    def _():
        for hh in heads:
            s_even_ref[hh] = scores(hh, key_tile(hh, 0))

    def step(j, s_cur_ref, s_next_ref):
        for hh in heads:
            s_next_ref[hh] = scores(hh, key_tile(hh, j + 1))
            consume(hh, s_cur_ref[hh], value_tile(hh, j), None)

    @pl.when(qi >= 1)
    def _():
        step(0, s_first_ref, s_odd_ref)

    def two_steps(j):
        step(j, s_odd_ref, s_even_ref)
        step(j + 1, s_even_ref, s_odd_ref)

    def four_steps(jj, carry):
        two_steps(4 * jj + 1)
        two_steps(4 * jj + 3)
        return carry

    n_full = jnp.maximum(qi - 1, 0)
    rest = n_full % 4
    lax.fori_loop(0, n_full // 4, four_steps, 0)

    @pl.when(rest >= 2)
    def _():
        two_steps(n_full - rest + 1)

    @pl.when(rest % 2 == 1)
    def _():
        step(n_full, s_odd_ref, s_even_ref)

    key_chunk = lax.broadcasted_iota(jnp.int32, (tile, tile), 0) // CHUNK
    query_chunk = lax.broadcasted_iota(jnp.int32, (tile, tile), 1) // CHUNK
    meta_mask = lax.broadcasted_iota(jnp.int32, (km_ref.shape[1], tile), 0) < N_META
    mask = jnp.concatenate([query_chunk >= key_chunk, meta_mask], axis=0)
    for parity, s_fin_ref in enumerate((s_even_ref, s_odd_ref)):
        @pl.when(qi % 2 == parity)
        def _():
            for hh in heads:
                s_first_ref[hh] = scores(hh, key_tile(hh, 0), q_next_ref)
                s_t = jnp.concatenate([s_fin_ref[hh], s_meta_ref[hh]], axis=0)
                v_cols = jnp.concatenate([value_tile(hh, qi), vm_ref[hh]], axis=1)
                consume(hh, s_t, v_cols, mask)

    out_t = jnp.concatenate([acc_ref[hh, :D_V] / acc_ref[hh, D_V:D_V + 1] for hh in heads], axis=0)
    o_ref[0] = out_t.T.astype(o_ref.dtype)


def _prompt_attn_call(kbound, q, k, v, km, vm, *, online):
    bsz, n_heads, _, s_len = q.shape
    tile = v.shape[-1]
    n_tiles = s_len // tile
    hps = ATTN_HEADS_PER_STEP
    assert s_len % tile == 0 and tile % CHUNK == 0 and n_heads % hps == 0 and (hps * D_V) % LANES == 0
    grid = (bsz, n_heads // hps, n_tiles)
    return pl.pallas_call(
        functools.partial(_attn_kernel, tile=tile, online=online), grid=grid,
        in_specs=[pl.BlockSpec((1, 1), lambda b, p, i: (0, 0)),
                  pl.BlockSpec((1, hps, LANES, tile), lambda b, p, i: (b, p, 0, i)),
                  pl.BlockSpec((1, hps, LANES, tile), lambda b, p, i: (b, p, 0, jnp.minimum(i + 1, n_tiles - 1))),
                  pl.BlockSpec((1, hps, s_len, LANES), lambda b, p, i: (b, p, 0, 0)),
                  pl.BlockSpec((1, hps, n_tiles, V_ROWS, tile), lambda b, p, i: (b, p, 0, 0, 0)),
                  pl.BlockSpec((hps, LANES, LANES), lambda b, p, i: (p, 0, 0)),
                  pl.BlockSpec((hps, V_ROWS, LANES), lambda b, p, i: (p, 0, 0))],
        out_specs=pl.BlockSpec((1, tile, hps * D_V), lambda b, p, i: (b, i, p)),
        out_shape=jax.ShapeDtypeStruct((bsz, s_len, n_heads * D_V), _BF16),
        scratch_shapes=[pltpu.VMEM((hps, V_ROWS, tile), _F32), pltpu.VMEM((hps, 1, tile), _F32),
                        pltpu.VMEM((hps, LANES, tile), _F32)]
                       + [pltpu.VMEM((hps, tile, tile), _F32)] * 3,
        compiler_params=pltpu.CompilerParams(dimension_semantics=("arbitrary",) * 3,
                                             vmem_limit_bytes=VMEM_LIMIT_BYTES),
        name="prompt_attn",
    )(kbound, q, q, k, v, km, vm)


def _meta_attn_kernel(q_ref, k_ref, v_ref, o_ref):
    for pair in range(N_HEADS // 2):
        out = None
        for hd in (2 * pair, 2 * pair + 1):
            s = _dot_nt(q_ref[hd], k_ref[hd])
            p = jnp.exp2(s - jnp.max(s, axis=1, keepdims=True))
            o = _dot(p.astype(_BF16), v_ref[hd]) / jnp.sum(p, axis=1, keepdims=True)
            out = o if out is None else out + o
        o_ref[:, pair * LANES:(pair + 1) * LANES] = out.astype(o_ref.dtype)


def _meta_attn_call(q, k, v):
    n_rows = q.shape[1]
    return pl.pallas_call(
        _meta_attn_kernel, out_shape=jax.ShapeDtypeStruct((n_rows, N_HEADS * D_V), _BF16),
        name="meta_attn",
    )(q, k, v)


def _row_sums_t(x):
    ones = jnp.ones((8, x.shape[1]), _BF16)
    hi = x.astype(_BF16)
    lo = (x - hi.astype(_F32)).astype(_BF16)
    return (_dot_nt(ones, hi) + _dot_nt(ones, lo))[0:1]


def _sample_attn_kernel(q_ref, kn_ref, vn_ref, lat_ref, kpe_ref, wukt_ref, wukq_ref, wuv_ref, kgain_ref, place_ref,
                        o_ref, *, past):
    t_len = q_ref.shape[2]
    kgain = kgain_ref[...]
    qg = [(q_ref[0, hd].astype(_F32) * kgain).astype(_BF16) for hd in range(N_HEADS)]
    q_all = jnp.concatenate(qg, axis=0)
    q_lat = jnp.concatenate([_dot(qg[hd], wukq_ref[hd]) for hd in range(N_HEADS)], axis=0).astype(_BF16)
    q_rope = _dot_nt(q_all, place_ref[...]).astype(_BF16)

    def cached_scores(lo, hi):
        lat = lat_ref[0, 0, lo:hi].astype(_BF16)
        kpe = kpe_ref[0, lo:hi]
        s = _dot_nt(q_lat, lat) + _dot_nt(q_rope, kpe.astype(_BF16))
        ss_rope = _row_sums_t(kpe * kpe)
        rows = []
        for pair in range(N_HEADS // 2):
            k_t = _dot_nt(wukt_ref[pair * LANES:(pair + 1) * LANES], lat)
            for hh in range(2):
                hd = 2 * pair + hh
                k_h = k_t[hh * D_NOPE:(hh + 1) * D_NOPE]
                r = lax.rsqrt((jnp.sum(k_h * k_h, axis=0, keepdims=True) + ss_rope) / D_QK + EPS)
                rows.append(s[hd * t_len:(hd + 1) * t_len] * r)
        return jnp.concatenate(rows, axis=0), lat

    block = min(SAMPLE_KEY_BLOCK, past)
    bounds = [(0, N_META)] + [(lo, lo + block) for lo in range(N_META, N_META + past, block)]
    m = jnp.full((N_HEADS * t_len, 1), MASKED, _F32)
    den = jnp.zeros((N_HEADS * t_len, 1), _F32)
    p_lat = jnp.zeros((N_HEADS * t_len, KV_RANK), _F32)
    for lo, hi in bounds:
        s, lat = cached_scores(lo, hi)
        m_next = jnp.maximum(m, jnp.max(s, axis=1, keepdims=True))
        alpha = jnp.exp2(m - m_next)
        p = jnp.exp2(s - m_next)
        den = alpha * den + jnp.sum(p, axis=1, keepdims=True)
        p_lat = alpha * p_lat + _dot(p.astype(_BF16), lat)
        m = m_next
    tq = lax.broadcasted_iota(jnp.int32, (t_len, t_len), 0)
    tk = lax.broadcasted_iota(jnp.int32, (t_len, t_len), 1)
    new_mask = (past + tq) // CHUNK >= (past + tk) // CHUNK
    s_new = jnp.concatenate([jnp.where(new_mask, _dot_nt(q_ref[0, hd], kn_ref[0, hd]), MASKED)
                             for hd in range(N_HEADS)], axis=0)
    m_next = jnp.maximum(m, jnp.max(s_new, axis=1, keepdims=True))
    alpha = jnp.exp2(m - m_next)
    p_new = jnp.exp2(s_new - m_next)
    den = alpha * den + jnp.sum(p_new, axis=1, keepdims=True)
    p_lat = (alpha * p_lat).astype(_BF16)
    p_new = p_new.astype(_BF16)
    for pair in range(N_HEADS // 2):
        out = None
        for hd in (2 * pair, 2 * pair + 1):
            rows = slice(hd * t_len, (hd + 1) * t_len)
            o = (_dot(p_lat[rows], wuv_ref[:, hd * LANES:(hd + 1) * LANES]) + _dot(p_new[rows], vn_ref[0, hd])) / den[rows]
            out = o if out is None else out + o
        o_ref[0, :, pair * LANES:(pair + 1) * LANES] = out.astype(o_ref.dtype)


def _sample_attn_call(layer, q, k_new, v_new, cache_lat, cache_kpe, lw, consts):
    bsz, n_heads, t_len, _ = q.shape
    l_cache = cache_lat.shape[2]
    past = l_cache - N_META
    assert past % min(SAMPLE_KEY_BLOCK, past) == 0 and past % LANES == 0 and t_len % HIST_ROWS == 0
    head_spec = pl.BlockSpec((1, n_heads, t_len, LANES), lambda b: (b, 0, 0, 0))
    weights = (lw["w_uk_t"], lw["w_uk_q"], lw["w_uv"], lw["kgain"], consts["place"])
    return pl.pallas_call(
        functools.partial(_sample_attn_kernel, past=past), grid=(bsz,),
        in_specs=[head_spec, head_spec, head_spec,
                  pl.BlockSpec((1, 1, l_cache, KV_RANK), lambda b: (layer, b, 0, 0)),
                  pl.BlockSpec((1, l_cache, D_ROPE), lambda b: (b, 0, 0))]
                 + [_const_spec(w) for w in weights],
        out_specs=pl.BlockSpec((1, t_len, n_heads * D_V), lambda b: (b, 0, 0)),
        out_shape=jax.ShapeDtypeStruct((bsz, t_len, n_heads * D_V), _BF16),
        compiler_params=pltpu.CompilerParams(dimension_semantics=("arbitrary",),
                                             vmem_limit_bytes=VMEM_LIMIT_BYTES),
        name="sample_attn",
    )(q, k_new, v_new, cache_lat, cache_kpe, *weights)


def _pack_heads(w, width, alternate=False):
    kdim = w.shape[0]
    w3 = w.reshape(kdim, N_HEADS, width)
    low = jnp.pad(w3, ((0, 0), (0, 0), (0, LANES - width)))
    if alternate:
        high = jnp.pad(w3, ((0, 0), (0, 0), (LANES - width, 0)))
        odd = (jnp.arange(N_HEADS) % 2 == 1)[None, :, None]
        low = jnp.where(odd, high, low)
    return low.reshape(kdim, N_HEADS * LANES)


def _layer_weights(l, norm_mix, w_in, q_a_norm, w_uq, kv_a_norm, w_uk, w_uv, q_norm, k_norm,
                   w_pool, pool_scale, w_o, norm_ffn, w_gate, w_up, w_down):
    d_in = w_in.shape[2]
    d_in_pad = -(-d_in // (2 * LANES)) * (2 * LANES)
    pad96 = lambda g: jnp.pad(g.astype(_F32), (0, LANES - D_QK))[None, :]
    q_scale = D_QK ** -0.5 * math.log2(math.e)
    return dict(
        gmix=norm_mix[l][None, :].astype(_F32),
        w_in=jnp.pad(w_in[l], ((0, 0), (0, d_in_pad - d_in))).astype(_BF16),
        gq=q_a_norm[l][None, :].astype(_F32),
        w_uq=_pack_heads(w_uq[l], D_QK).astype(_BF16),
        w_uq_t=_pack_heads(w_uq[l], D_QK).astype(_BF16).T,
        gkv=kv_a_norm[l][None, :].astype(_F32),
        w_uk=_pack_heads(w_uk[l], D_NOPE).astype(_BF16),
        w_uk_t=w_uk[l].astype(_BF16).T,
        w_uk_q=jnp.pad(w_uk[l].astype(_BF16).T.reshape(N_HEADS, D_NOPE, KV_RANK),
                       ((0, 0), (0, LANES - D_NOPE), (0, 0))),
        w_uv=_pack_heads(w_uv[l], D_V, alternate=True).astype(_BF16),
        w_uv_t=w_uv[l].astype(_BF16).T,
        qgain=pad96(q_norm[l]) * q_scale,
        qgain_t=(pad96(q_norm[l]) * q_scale).T,
        kgain=pad96(k_norm[l]),
        w_pool=w_pool[l].astype(_BF16),
        pscale=pool_scale[l][None, :].astype(_F32),
        w_o_pool=w_o[l][:POOL_WIDTH].astype(_BF16),
        w_o_attn=w_o[l][POOL_WIDTH:].astype(_BF16),
        gffn=norm_ffn[l][None, :].astype(_F32),
        w_gate=w_gate[l].astype(_BF16),
        w_up=w_up[l].astype(_BF16),
        w_down=w_down[l].astype(_BF16),
    )


def _rope_tables(pos0, t_len):
    half = D_ROPE // 2
    inv = ROPE_THETA ** (-jnp.arange(half, dtype=_F32) / half)
    ang = (pos0 + jnp.arange(t_len, dtype=jnp.int32)).astype(_F32)[:, None] * inv[None, :]
    cos, sin = jnp.cos(ang), jnp.sin(ang)
    cos32 = jnp.concatenate([cos, cos], axis=1)
    sin32 = jnp.concatenate([-sin, sin], axis=1)
    ones = jnp.ones((t_len, D_NOPE), _F32)
    zeros = jnp.zeros((t_len, D_NOPE), _F32)
    tail1 = jnp.ones((t_len, LANES - D_QK), _F32)
    tail0 = jnp.zeros((t_len, LANES - D_QK), _F32)
    cosp = jnp.concatenate([ones, cos32, tail1], axis=1)
    sinp = jnp.concatenate([zeros, sin32, tail0], axis=1)
    cosk = jnp.concatenate([cos32, ones, tail1], axis=1)
    sink = jnp.concatenate([sin32, zeros, tail0], axis=1)
    return (cosp, sinp, cosk, sink, cos.T, sin.T)


def _perm_consts():
    idx = jnp.arange(D_ROPE)
    place = jnp.zeros((D_ROPE, LANES), _F32).at[idx, D_NOPE + idx].set(1.0)
    return dict(place=place.astype(_BF16))


def kernel(x_prompt, x_sample, cache_latent, cache_krope, state_pool, meta_tokens, norm_mix, w_in, q_a_norm, w_uq, kv_a_norm, w_uk, w_uv, q_norm, k_norm, w_pool, pool_scale, w_o, norm_ffn, w_gate, w_up, w_down):
    depth = norm_mix.shape[0]
    bp, s_len, d_model = x_prompt.shape
    bs, t_len, _ = x_sample.shape
    past = cache_latent.shape[2] - N_META
    consts = _perm_consts()
    tabs_meta = _rope_tables(0, N_META)
    tabs_prompt = _rope_tables(N_META, s_len)
    tabs_sample = _rope_tables(N_META + past, t_len)
    nb_sample = math.gcd(bs, 8)

    xm = meta_tokens.astype(_F32)[None]
    xp = x_prompt
    xs = x_sample
    zero_hist = jnp.zeros((1, HIST_ROWS, POOL_WIDTH), _F32)
    seq_bufs = tuple(jnp.zeros((depth, bp, N_META + s_len, width), _F32) for width in (KV_RANK, D_ROPE))
    meta_lat, meta_kr, pool_p, lat_s, kpe_s, pool_s = [], [], [], [], [], []
    for l in range(depth):
        lw = _layer_weights(l, norm_mix, w_in, q_a_norm, w_uq, kv_a_norm, w_uk, w_uv, q_norm, k_norm,
                            w_pool, pool_scale, w_o, norm_ffn, w_gate, w_up, w_down)
        pm = _pre_call(xm, zero_hist, tabs_meta, lw, nb=1, hist_shared=True, truncated=True)
        pp = _pre_call(xp, pm["ulast"], tabs_prompt, lw, nb=1, hist_shared=True, truncated=False,
                       feature_major=True, into=(l, depth, seq_bufs))
        seq_bufs = (pp["lat"], pp["krope"])
        hist_s = jnp.pad(state_pool[l].astype(_F32), ((0, 0), (HIST_ROWS - POOL_STATE, 0), (0, 0)))
        ps = _pre_call(xs, hist_s, tabs_sample, lw, nb=nb_sample, hist_shared=False, truncated=False)

        km = jnp.pad(pm["k"][0], ((0, 0), (0, LANES - N_META), (0, 0)))
        vm_t = jnp.stack([pm["v"][0, hd, :, (hd % 2) * D_V:(hd % 2 + 1) * D_V].T for hd in range(N_HEADS)])
        vm_t = jnp.concatenate([vm_t, jnp.ones((N_HEADS, V_ROWS - D_V, N_META), vm_t.dtype)], axis=1)
        vm_t = jnp.pad(vm_t, ((0, 0), (0, 0), (0, LANES - N_META)))
        kbound = (D_QK ** 0.5 * jnp.max(jnp.abs(lw["kgain"]))).reshape(1, 1)
        qbound = D_QK ** 0.5 * jnp.max(jnp.abs(lw["qgain"]))
        attn_args = (kbound, pp["q"], pp["k"], pp["v"], km, vm_t)
        attn_p = lax.cond(2.0 * qbound * kbound[0, 0] <= EXP2_SAFE_RANGE,
                          functools.partial(_prompt_attn_call, online=False),
                          functools.partial(_prompt_attn_call, online=True), *attn_args)
        attn_s = _sample_attn_call(l, ps["q"], ps["k"], ps["v"], cache_latent, cache_krope[l], lw, consts)
        xp = _post_call(xp, pp["pool"], attn_p, lw)
        xs = _post_call(xs, ps["pool"], attn_s, lw)
        if l + 1 < depth:
            attn_m = _meta_attn_call(pm["q"][0], pm["k"][0], pm["v"][0])
            xm = _post_call(xm, pm["pool"], attn_m[None], lw)

        meta_lat.append(pm["lat"][0])
        meta_kr.append(pm["krope"][0])
        pool_p.append(pp["ulast"][:, HIST_ROWS - POOL_STATE:])
        lat_s.append(ps["lat"])
        kpe_s.append(ps["krope"])
        pool_s.append(ps["ulast"][:, HIST_ROWS - POOL_STATE:])
    lat_p, kpe_p = _meta_rows_call(*seq_bufs, jnp.stack(meta_lat), jnp.stack(meta_kr))
    return (xp, xs, lat_p, kpe_p, jnp.stack(pool_p), jnp.stack(lat_s), jnp.stack(kpe_s), jnp.stack(pool_s))
```

```python
import functools
import math

import jax
import jax.numpy as jnp
from jax import lax
from jax.experimental import pallas as pl
from jax.experimental.pallas import tpu as pltpu

CHUNK = 64
N_META = 16
N_HEADS = 8
D_NOPE = 64
D_ROPE = 32
D_QK = D_NOPE + D_ROPE
D_V = 64
Q_RANK = 384
KV_RANK = 256
POOL_WINDOWS = (2, 4, 8, 16)
POOL_GROUP = 128
POOL_WIDTH = POOL_GROUP * len(POOL_WINDOWS)
POOL_STATE = max(POOL_WINDOWS) - 1
ROPE_THETA = 10000.0
EPS = 1e-6

LANES = 128
HIST_ROWS = 16
POOL_PAD = 8
VMEM_LIMIT_BYTES = 56 * 1024 * 1024
ROW_TILE = 512
PRE_SPLITS = 2
ATTN_HEADS_PER_STEP = 4
SAMPLE_KEY_BLOCK = 2048
V_ROWS = D_V + 16
MASKED = -1e30
EXP2_SAFE_RANGE = 120.0

_F32 = jnp.float32
_BF16 = jnp.bfloat16
_NT = (((1,), (1,)), ((), ()))


def _dot(a, b):
    return jnp.dot(a, b, preferred_element_type=_F32)


def _dot_nt(a, b):
    return lax.dot_general(a, b, _NT, preferred_element_type=_F32)


def _rms(x, g):
    return x * lax.rsqrt(jnp.mean(x * x, axis=-1, keepdims=True) + EPS) * g


def _head_norm(x, gain):
    ss = jnp.sum(x * x, axis=-1, keepdims=True)
    return x * lax.rsqrt(ss / D_QK + EPS) * gain


def _pre_kernel(x_ref, hist_ref, cosp_ref, sinp_ref, cosk_ref, sink_ref, cost_ref, sint_ref,
                gmix_ref, win_ref, gq_ref, wuq_ref, gkv_ref, wuk_ref, wuv_ref, qgain_ref, kgain_ref,
                wpool_ref, pscale_ref,
                q_ref, k_ref, v_ref, lat_ref, krope_ref, pool_ref, ulast_ref,
                ext_ref, stage_ref, *, nb, tm, hist_shared, truncated, feature_major, splits):
    t = pl.program_id(1)
    d_model = x_ref.shape[-1]
    hist0 = POOL_PAD
    for b in range(nb):
        ext_ref[b, 0:POOL_PAD, :] = jnp.zeros((POOL_PAD, POOL_WIDTH), _F32)

        @pl.when(t == 0)
        def _():
            ext_ref[b, hist0:hist0 + HIST_ROWS, :] = hist_ref[0 if hist_shared else b]

        @pl.when(t > 0)
        def _():
            ext_ref[b, hist0:hist0 + HIST_ROWS, :] = ext_ref[b, hist0 + tm:hist0 + tm + HIST_ROWS, :]

    def row_range(sp, r0, rs):
        rows = nb * rs
        row_sl = slice(r0, r0 + rs)
        x = x_ref[:, row_sl, :].reshape(rows, d_model)
        h = _rms(x, gmix_ref[...]).astype(_BF16)
        z = _dot(h, win_ref[...])
        u = z[:, :POOL_WIDTH]
        q_lat = z[:, POOL_WIDTH:POOL_WIDTH + Q_RANK]
        kv_lat = z[:, POOL_WIDTH + Q_RANK:POOL_WIDTH + Q_RANK + KV_RANK]
        kpe = z[:, POOL_WIDTH + Q_RANK + KV_RANK:POOL_WIDTH + Q_RANK + KV_RANK + LANES]

        first = hist0 + HIST_ROWS + r0
        lo, span = first - HIST_ROWS, rs + HIST_ROWS
        for b in range(nb):
            ext_ref[b, first:first + rs, :] = u[b * rs:(b + 1) * rs]
        for b in range(nb):
            own = ext_ref[b, lo:lo + span, :]
            part = own + ext_ref[b, lo - 1:lo - 1 + span, :]
            window_sums = [part[:, :POOL_GROUP]]
            for k, w in enumerate(POOL_WINDOWS[:-1]):
                assert POOL_WINDOWS[k + 1] == 2 * w and w <= POOL_PAD
                c0 = (k + 1) * POOL_GROUP
                part = part[:, POOL_GROUP:]
                buf = stage_ref.at[sp, b, k]
                buf[0:POOL_PAD, c0:] = jnp.zeros((POOL_PAD, POOL_WIDTH - c0), _F32)
                buf[POOL_PAD:POOL_PAD + span, c0:] = part
                part = part + buf[POOL_PAD - w:POOL_PAD - w + span, c0:]
                window_sums.append(part[:, :POOL_GROUP])
            cols = []
            for g, w in enumerate(POOL_WINDOWS):
                cur = own[HIST_ROWS:, g * POOL_GROUP:(g + 1) * POOL_GROUP]
                acc = window_sums[g][HIST_ROWS:]
                if truncated:
                    row = t * tm + r0 + lax.broadcasted_iota(jnp.int32, (rs, 1), 0)
                    mean = acc / jnp.minimum(row + 1, w).astype(_F32)
                else:
                    mean = acc / float(w)
                cols.append(_dot((mean - cur).astype(_BF16), wpool_ref[g]))
            pool = jnp.concatenate(cols, axis=1) * pscale_ref[...]
            pool_ref[b, row_sl, :] = pool.astype(pool_ref.dtype)

        qn = _rms(q_lat, gq_ref[...]).astype(_BF16)
        c_kv = _rms(kv_lat, gkv_ref[...])
        into_results = len(lat_ref.shape) == 4
        lat_view = lat_ref.at[0] if into_results else lat_ref
        krope_view = krope_ref.at[0] if into_results else krope_ref
        lat_view[:, row_sl, :] = c_kv.reshape(nb, rs, KV_RANK)
        ckv_b = c_kv.astype(_BF16)
        knope = _dot(ckv_b, wuk_ref[...])

        def per_row(ref):
            tab = ref[row_sl, :]
            return tab if nb == 1 else jnp.concatenate([tab] * nb, axis=0)

        lane = lax.broadcasted_iota(jnp.int32, (1, LANES), 1)
        half = D_ROPE // 2
        partner = jnp.where(lane < half, pltpu.roll(kpe, LANES - half, 1), pltpu.roll(kpe, half, 1))
        krope = kpe * per_row(cosk_ref) + partner * per_row(sink_ref)
        krope_view[:, row_sl, :] = krope[:, :D_ROPE].reshape(nb, rs, D_ROPE)
        kpe_placed = pltpu.roll(krope, D_NOPE, 1)

        for hd in range(N_HEADS):
            kh = _head_norm(knope[:, hd * LANES:(hd + 1) * LANES] + kpe_placed, kgain_ref[...])
            k_ref[:, hd, row_sl, :] = kh.astype(k_ref.dtype).reshape(nb, rs, LANES)

        if feature_major:
            half = D_ROPE // 2
            qraw_t = _dot_nt(wuq_ref[...], qn)
            v_t = _dot_nt(wuv_ref[...], ckv_b)
            ones = jnp.ones((V_ROWS - D_V, rows), _F32)
            for hd in range(N_HEADS):
                v_h = jnp.concatenate([v_t[hd * D_V:(hd + 1) * D_V], ones], axis=0)
                v_ref[0, hd, 0, :, row_sl] = v_h.astype(v_ref.dtype)
            cos_t, sin_t, gain = cost_ref[:, row_sl], sint_ref[:, row_sl], qgain_ref[...]
            for hd in range(N_HEADS):
                base = hd * LANES
                nope = qraw_t[base:base + D_NOPE]
                x1 = qraw_t[base + D_NOPE:base + D_NOPE + half]
                x2 = qraw_t[base + D_NOPE + half:base + D_QK]
                r1 = x1 * cos_t - x2 * sin_t
                r2 = x1 * sin_t + x2 * cos_t
                ss = (jnp.sum(nope * nope, axis=0, keepdims=True) + jnp.sum(r1 * r1, axis=0, keepdims=True)
                      + jnp.sum(r2 * r2, axis=0, keepdims=True))
                scale = lax.rsqrt(ss / D_QK + EPS)
                qh = jnp.concatenate([nope * scale * gain[:D_NOPE], r1 * scale * gain[D_NOPE:D_NOPE + half],
                                      r2 * scale * gain[D_NOPE + half:D_QK],
                                      jnp.zeros((LANES - D_QK, rows), _F32)], axis=0)
                q_ref[0, hd, :, row_sl] = qh.astype(q_ref.dtype)
        else:
            cosp, sinp = per_row(cosp_ref), per_row(sinp_ref)
            qraw = _dot(qn, wuq_ref[...])
            vfull = _dot(ckv_b, wuv_ref[...])
            lane = lax.broadcasted_iota(jnp.int32, (1, LANES), 1)
            first_half = lane < D_NOPE + D_ROPE // 2
            for hd in range(N_HEADS):
                sl = slice(hd * LANES, (hd + 1) * LANES)
                qh = qraw[:, sl]
                partner = jnp.where(first_half, pltpu.roll(qh, LANES - D_ROPE // 2, 1),
                                    pltpu.roll(qh, D_ROPE // 2, 1))
                qh = _head_norm(qh * cosp + partner * sinp, qgain_ref[...])
                q_ref[:, hd, row_sl, :] = qh.astype(q_ref.dtype).reshape(nb, rs, LANES)
                v_ref[:, hd, row_sl, :] = vfull[:, sl].astype(v_ref.dtype).reshape(nb, rs, LANES)

    assert POOL_WINDOWS[0] == 2
    for sp in range(splits):
        row_range(sp, sp * (tm // splits), tm // splits)
    for b in range(nb):
        ulast_ref[b] = ext_ref[b, hist0 + tm:hist0 + tm + HIST_ROWS, :]


def _const_spec(a):
    nd = a.ndim
    return pl.BlockSpec(a.shape, lambda *_: (0,) * nd, pipeline_mode=pl.Buffered(1))


def _skip_refs(fn, start, count):
    def wrapped(*refs, **kw):
        return fn(*refs[:start], *refs[start + count:], **kw)
    return wrapped


def _pre_call(x, hist, tabs, lw, *, nb, hist_shared, truncated, feature_major=False, into=None):
    bsz, t_len, d_model = x.shape
    tm = t_len if nb > 1 else min(ROW_TILE, t_len)
    assert bsz % nb == 0 and t_len % tm == 0 and tm % HIST_ROWS == 0 and not (feature_major and nb > 1)
    grid = (bsz // nb, t_len // tm)
    fm = "_t" if feature_major else ""
    weights = (lw["gmix"], lw["w_in"], lw["gq"], lw["w_uq" + fm], lw["gkv"], lw["w_uk"], lw["w_uv" + fm],
               lw["qgain" + fm], lw["kgain"], lw["w_pool"], lw["pscale"])
    hist_spec = (pl.BlockSpec((1, HIST_ROWS, POOL_WIDTH), lambda b, t: (0, 0, 0)) if hist_shared
                 else pl.BlockSpec((nb, HIST_ROWS, POOL_WIDTH), lambda b, t: (b, 0, 0)))
    in_specs = [pl.BlockSpec((nb, tm, d_model), lambda b, t: (b, t, 0)), hist_spec]
    in_specs += [pl.BlockSpec((tm, a.shape[1]), lambda b, t: (t, 0)) for a in tabs[:4]]
    in_specs += [pl.BlockSpec((a.shape[0], tm), lambda b, t: (0, t)) for a in tabs[4:]]
    in_specs += [_const_spec(w) for w in weights]
    head_spec = pl.BlockSpec((nb, N_HEADS, tm, LANES), lambda b, t: (b, 0, t, 0))
    head_shape = jax.ShapeDtypeStruct((bsz, N_HEADS, t_len, LANES), _BF16)
    if feature_major:
        q_spec = pl.BlockSpec((1, N_HEADS, LANES, tm), lambda b, t: (b, 0, 0, t))
        q_shape = jax.ShapeDtypeStruct((bsz, N_HEADS, LANES, t_len), _BF16)
        v_spec = pl.BlockSpec((1, N_HEADS, 1, V_ROWS, tm), lambda b, t: (b, 0, t, 0, 0))
        v_shape = jax.ShapeDtypeStruct((bsz, N_HEADS, t_len // tm, V_ROWS, tm), _BF16)
    else:
        q_spec, q_shape, v_spec, v_shape = head_spec, head_shape, head_spec, head_shape

    def row_out(width, dtype):
        return (pl.BlockSpec((nb, tm, width), lambda b, t: (b, t, 0)),
                jax.ShapeDtypeStruct((bsz, t_len, width), dtype))

    splits = PRE_SPLITS if tm == ROW_TILE else 1
    body = functools.partial(_pre_kernel, nb=nb, tm=tm, hist_shared=hist_shared, truncated=truncated,
                             feature_major=feature_major, splits=splits)
    operands = [x, hist, *tabs, *weights]
    aliases = {}
    if into is None:
        lat_spec, lat_shape = row_out(KV_RANK, _F32)
        kr_spec, kr_shape = row_out(D_ROPE, _F32)
    else:
        layer, depth, bufs = into
        assert nb == 1

        def seq_out(width):
            block = tuple(pl.Element(n) for n in (1, 1, tm, width))
            return (pl.BlockSpec(block, lambda b, t: (layer, b, (1 + t * (tm // N_META)) * N_META, 0)),
                    jax.ShapeDtypeStruct((depth, bsz, N_META + t_len, width), _F32))

        lat_spec, lat_shape = seq_out(KV_RANK)
        kr_spec, kr_shape = seq_out(D_ROPE)
        body = _skip_refs(body, len(operands), len(bufs))
        aliases = {len(operands): 3, len(operands) + 1: 4}
        in_specs += [pl.BlockSpec(memory_space=pl.ANY)] * len(bufs)
        operands += list(bufs)
    pool_spec, pool_shape = row_out(POOL_WIDTH, _BF16)
    out_specs = [q_spec, head_spec, v_spec, lat_spec, kr_spec, pool_spec,
                 pl.BlockSpec((nb, HIST_ROWS, POOL_WIDTH), lambda b, t: (b, 0, 0))]
    out_shape = [q_shape, head_shape, v_shape, lat_shape, kr_shape, pool_shape,
                 jax.ShapeDtypeStruct((bsz, HIST_ROWS, POOL_WIDTH), _F32)]
    q, k, v, lat, krope, pool, ulast = pl.pallas_call(
        body, grid=grid, in_specs=in_specs, out_specs=out_specs, out_shape=out_shape,
        input_output_aliases=aliases,
        scratch_shapes=[pltpu.VMEM((nb, POOL_PAD + HIST_ROWS + tm, POOL_WIDTH), _F32),
                        pltpu.VMEM((splits, nb, len(POOL_WINDOWS) - 1, POOL_PAD + HIST_ROWS + tm // splits, POOL_WIDTH),
                                   _F32)],
        compiler_params=pltpu.CompilerParams(dimension_semantics=("arbitrary", "arbitrary"),
                                             vmem_limit_bytes=VMEM_LIMIT_BYTES),
        name="pre_mixer",
    )(*operands)
    return dict(q=q, k=k, v=v, lat=lat, krope=krope, pool=pool, ulast=ulast)


def _meta_rows_kernel(lat_any, kr_any, mlat_ref, mkr_ref, lat_ref, kr_ref):
    del lat_any, kr_any
    lat_ref[0, 0] = mlat_ref[0]
    kr_ref[0, 0] = mkr_ref[0]


def _meta_rows_call(lat_all, kr_all, meta_lat, meta_kr):
    depth, bsz = lat_all.shape[:2]

    def src(width):
        return pl.BlockSpec((1, N_META, width), lambda l, b: (l, 0, 0))

    def dst(width):
        return pl.BlockSpec((1, 1, N_META, width), lambda l, b: (l, b, 0, 0))

    return pl.pallas_call(
        _meta_rows_kernel, grid=(depth, bsz),
        in_specs=[pl.BlockSpec(memory_space=pl.ANY)] * 2 + [src(KV_RANK), src(D_ROPE)],
        out_specs=[dst(KV_RANK), dst(D_ROPE)],
        out_shape=[jax.ShapeDtypeStruct(lat_all.shape, lat_all.dtype), jax.ShapeDtypeStruct(kr_all.shape, kr_all.dtype)],
        input_output_aliases={0: 0, 1: 1},
        compiler_params=pltpu.CompilerParams(dimension_semantics=("arbitrary", "arbitrary")),
        name="meta_rows",
    )(lat_all, kr_all, meta_lat, meta_kr)


def _post_kernel(x_ref, pool_ref, attn_ref, wop_ref, woa_ref, gffn_ref, wg_ref, wu_ref, wd_ref, o_ref):
    x1 = x_ref[...] + _dot(pool_ref[...], wop_ref[...]) + _dot(attn_ref[...], woa_ref[...])
    hn = _rms(x1, gffn_ref[...]).astype(_BF16)
    g = _dot(hn, wg_ref[...])
    up = _dot(hn, wu_ref[...])
    act = (g * jax.nn.sigmoid(g) * up).astype(_BF16)
    o_ref[...] = x1 + _dot(act, wd_ref[...])


def _post_call(x, pool, attn, lw):
    shape = x.shape
    d_model = shape[-1]
    x2 = x.reshape(-1, d_model)
    n = x2.shape[0]
    tm = min(ROW_TILE, n)
    assert n % tm == 0
    weights = (lw["w_o_pool"], lw["w_o_attn"], lw["gffn"], lw["w_gate"], lw["w_up"], lw["w_down"])

    def rows(width):
        return pl.BlockSpec((tm, width), lambda i: (i, 0))

    out = pl.pallas_call(
        _post_kernel, grid=(n // tm,),
        in_specs=[rows(d_model), rows(POOL_WIDTH), rows(N_HEADS * D_V)] + [_const_spec(w) for w in weights],
        out_specs=rows(d_model), out_shape=jax.ShapeDtypeStruct((n, d_model), _F32),
        compiler_params=pltpu.CompilerParams(dimension_semantics=("arbitrary",),
                                             vmem_limit_bytes=VMEM_LIMIT_BYTES),
        name="post_ffn",
    )(x2, pool.reshape(n, POOL_WIDTH), attn.reshape(n, N_HEADS * D_V), *weights)
    return out.reshape(shape)


def _attn_kernel(kbound_ref, q_ref, q_next_ref, k_ref, v_ref, km_ref, vm_ref, o_ref,
                 acc_ref, m_ref, s_meta_ref, s_first_ref, s_even_ref, s_odd_ref, *, tile, online):
    qi = pl.program_id(2)

    def scores(hh, k_rows, queries_ref=q_ref):
        return _dot(k_rows, queries_ref[0, hh])

    def consume(hh, s_t, v_cols, mask):
        if mask is not None:
            s_t = jnp.where(mask, s_t, MASKED)
        m_prev = m_ref[hh]
        m_next = jnp.maximum(m_prev, jnp.max(s_t, axis=0, keepdims=True)) if online else m_prev
        p_t = jnp.exp2(s_t - m_next).astype(_BF16)
        pv = _dot(v_cols, p_t)
        if online:
            acc_ref[hh] = jnp.exp2(m_prev - m_next) * acc_ref[hh] + pv
            m_ref[hh] = m_next
        else:
            acc_ref[hh] += pv

    def key_tile(hh, j):
        return k_ref[0, hh, pl.ds(pl.multiple_of(j * tile, tile), tile), :]

    def value_tile(hh, j):
        return v_ref[0, hh, j]

    heads = range(q_ref.shape[1])
    for hh in heads:
        acc_ref[hh] = jnp.zeros(acc_ref.shape[1:], _F32)
        if online:
            m_ref[hh] = jnp.full((1, tile), MASKED, _F32)
        else:
            qf = q_ref[0, hh].astype(_F32)
            m_ref[hh] = jnp.sqrt(jnp.sum(qf * qf, axis=0, keepdims=True)) * kbound_ref[...]
        s_meta_ref[hh] = scores(hh, km_ref[hh])

    @pl.when(qi == 0)
    def _():
        for hh in heads:
            s_even_ref[hh] = scores(hh, key_tile(hh, 0))

    def step(j, s_cur_ref, s_next_ref):
        for hh in heads:
            s_next_ref[hh] = scores(hh, key_tile(hh, j + 1))
            consume(hh, s_cur_ref[hh], value_tile(hh, j), None)

    @pl.when(qi >= 1)
    def _():
        step(0, s_first_ref, s_odd_ref)

    def two_steps(j):
        step(j, s_odd_ref, s_even_ref)
        step(j + 1, s_even_ref, s_odd_ref)

    def four_steps(jj, carry):
        two_steps(4 * jj + 1)
        two_steps(4 * jj + 3)
        return carry

    n_full = jnp.maximum(qi - 1, 0)
    rest = n_full % 4
    lax.fori_loop(0, n_full // 4, four_steps, 0)

    @pl.when(rest >= 2)
    def _():
        two_steps(n_full - rest + 1)

    @pl.when(rest % 2 == 1)
    def _():
        step(n_full, s_odd_ref, s_even_ref)

    key_chunk = lax.broadcasted_iota(jnp.int32, (tile, tile), 0) // CHUNK
    query_chunk = lax.broadcasted_iota(jnp.int32, (tile, tile), 1) // CHUNK
    meta_mask = lax.broadcasted_iota(jnp.int32, (km_ref.shape[1], tile), 0) < N_META
    mask = jnp.concatenate([query_chunk >= key_chunk, meta_mask], axis=0)
    for parity, s_fin_ref in enumerate((s_even_ref, s_odd_ref)):
        @pl.when(qi % 2 == parity)
        def _():
            for hh in heads:
                s_first_ref[hh] = scores(hh, key_tile(hh, 0), q_next_ref)
                s_t = jnp.concatenate([s_fin_ref[hh], s_meta_ref[hh]], axis=0)
                v_cols = jnp.concatenate([value_tile(hh, qi), vm_ref[hh]], axis=1)
                consume(hh, s_t, v_cols, mask)

    out_t = jnp.concatenate([acc_ref[hh, :D_V] / acc_ref[hh, D_V:D_V + 1] for hh in heads], axis=0)
    o_ref[0] = out_t.T.astype(o_ref.dtype)


def _prompt_attn_call(kbound, q, k, v, km, vm, *, online):
    bsz, n_heads, _, s_len = q.shape
    tile = v.shape[-1]
    n_tiles = s_len // tile
    hps = ATTN_HEADS_PER_STEP
    assert s_len % tile == 0 and tile % CHUNK == 0 and n_heads % hps == 0 and (hps * D_V) % LANES == 0
    grid = (bsz, n_heads // hps, n_tiles)
    return pl.pallas_call(
        functools.partial(_attn_kernel, tile=tile, online=online), grid=grid,
        in_specs=[pl.BlockSpec((1, 1), lambda b, p, i: (0, 0)),
                  pl.BlockSpec((1, hps, LANES, tile), lambda b, p, i: (b, p, 0, i)),
                  pl.BlockSpec((1, hps, LANES, tile), lambda b, p, i: (b, p, 0, jnp.minimum(i + 1, n_tiles - 1))),
                  pl.BlockSpec((1, hps, s_len, LANES), lambda b, p, i: (b, p, 0, 0)),
                  pl.BlockSpec((1, hps, n_tiles, V_ROWS, tile), lambda b, p, i: (b, p, 0, 0, 0)),
                  pl.BlockSpec((hps, LANES, LANES), lambda b, p, i: (p, 0, 0)),
                  pl.BlockSpec((hps, V_ROWS, LANES), lambda b, p, i: (p, 0, 0))],
        out_specs=pl.BlockSpec((1, tile, hps * D_V), lambda b, p, i: (b, i, p)),
        out_shape=jax.ShapeDtypeStruct((bsz, s_len, n_heads * D_V), _BF16),
        scratch_shapes=[pltpu.VMEM((hps, V_ROWS, tile), _F32), pltpu.VMEM((hps, 1, tile), _F32),
                        pltpu.VMEM((hps, LANES, tile), _F32)]
                       + [pltpu.VMEM((hps, tile, tile), _F32)] * 3,
        compiler_params=pltpu.CompilerParams(dimension_semantics=("arbitrary",) * 3,
                                             vmem_limit_bytes=VMEM_LIMIT_BYTES),
        name="prompt_attn",
    )(kbound, q, q, k, v, km, vm)


def _meta_attn_kernel(q_ref, k_ref, v_ref, o_ref):
    for pair in range(N_HEADS // 2):
        out = None
        for hd in (2 * pair, 2 * pair + 1):
            s = _dot_nt(q_ref[hd], k_ref[hd])
            p = jnp.exp2(s - jnp.max(s, axis=1, keepdims=True))
            o = _dot(p.astype(_BF16), v_ref[hd]) / jnp.sum(p, axis=1, keepdims=True)
            out = o if out is None else out + o
        o_ref[:, pair * LANES:(pair + 1) * LANES] = out.astype(o_ref.dtype)


def _meta_attn_call(q, k, v):
    n_rows = q.shape[1]
    return pl.pallas_call(
        _meta_attn_kernel, out_shape=jax.ShapeDtypeStruct((n_rows, N_HEADS * D_V), _BF16),
        name="meta_attn",
    )(q, k, v)


def _row_sums_t(x):
    ones = jnp.ones((8, x.shape[1]), _BF16)
    hi = x.astype(_BF16)
    lo = (x - hi.astype(_F32)).astype(_BF16)
    return (_dot_nt(ones, hi) + _dot_nt(ones, lo))[0:1]


def _sample_attn_kernel(q_ref, kn_ref, vn_ref, lat_ref, kpe_ref, wukt_ref, wukq_ref, wuv_ref, kgain_ref, place_ref,
                        o_ref, *, past):
    t_len = q_ref.shape[2]
    kgain = kgain_ref[...]
    qg = [(q_ref[0, hd].astype(_F32) * kgain).astype(_BF16) for hd in range(N_HEADS)]
    q_all = jnp.concatenate(qg, axis=0)
    q_lat = jnp.concatenate([_dot(qg[hd], wukq_ref[hd]) for hd in range(N_HEADS)], axis=0).astype(_BF16)
    q_rope = _dot_nt(q_all, place_ref[...]).astype(_BF16)

    def cached_scores(lo, hi):
        lat = lat_ref[0, 0, lo:hi].astype(_BF16)
        kpe = kpe_ref[0, lo:hi]
        s = _dot_nt(q_lat, lat) + _dot_nt(q_rope, kpe.astype(_BF16))
        ss_rope = _row_sums_t(kpe * kpe)
        rows = []
        for pair in range(N_HEADS // 2):
            k_t = _dot_nt(wukt_ref[pair * LANES:(pair + 1) * LANES], lat)
            for hh in range(2):
                hd = 2 * pair + hh
                k_h = k_t[hh * D_NOPE:(hh + 1) * D_NOPE]
                r = lax.rsqrt((jnp.sum(k_h * k_h, axis=0, keepdims=True) + ss_rope) / D_QK + EPS)
                rows.append(s[hd * t_len:(hd + 1) * t_len] * r)
        return jnp.concatenate(rows, axis=0), lat

    block = min(SAMPLE_KEY_BLOCK, past)
    bounds = [(0, N_META)] + [(lo, lo + block) for lo in range(N_META, N_META + past, block)]
    m = jnp.full((N_HEADS * t_len, 1), MASKED, _F32)
    den = jnp.zeros((N_HEADS * t_len, 1), _F32)
    p_lat = jnp.zeros((N_HEADS * t_len, KV_RANK), _F32)
    for lo, hi in bounds:
        s, lat = cached_scores(lo, hi)
        m_next = jnp.maximum(m, jnp.max(s, axis=1, keepdims=True))
        alpha = jnp.exp2(m - m_next)
        p = jnp.exp2(s - m_next)
        den = alpha * den + jnp.sum(p, axis=1, keepdims=True)
        p_lat = alpha * p_lat + _dot(p.astype(_BF16), lat)
        m = m_next
    tq = lax.broadcasted_iota(jnp.int32, (t_len, t_len), 0)
    tk = lax.broadcasted_iota(jnp.int32, (t_len, t_len), 1)
    new_mask = (past + tq) // CHUNK >= (past + tk) // CHUNK
    s_new = jnp.concatenate([jnp.where(new_mask, _dot_nt(q_ref[0, hd], kn_ref[0, hd]), MASKED)
                             for hd in range(N_HEADS)], axis=0)
    m_next = jnp.maximum(m, jnp.max(s_new, axis=1, keepdims=True))
    alpha = jnp.exp2(m - m_next)
    p_new = jnp.exp2(s_new - m_next)
    den = alpha * den + jnp.sum(p_new, axis=1, keepdims=True)
    p_lat = (alpha * p_lat).astype(_BF16)
    p_new = p_new.astype(_BF16)
    for pair in range(N_HEADS // 2):
        out = None
        for hd in (2 * pair, 2 * pair + 1):
            rows = slice(hd * t_len, (hd + 1) * t_len)
            o = (_dot(p_lat[rows], wuv_ref[:, hd * LANES:(hd + 1) * LANES]) + _dot(p_new[rows], vn_ref[0, hd])) / den[rows]
            out = o if out is None else out + o
        o_ref[0, :, pair * LANES:(pair + 1) * LANES] = out.astype(o_ref.dtype)


def _sample_attn_call(layer, q, k_new, v_new, cache_lat, cache_kpe, lw, consts):
    bsz, n_heads, t_len, _ = q.shape
    l_cache = cache_lat.shape[2]
    past = l_cache - N_META
    assert past % min(SAMPLE_KEY_BLOCK, past) == 0 and past % LANES == 0 and t_len % HIST_ROWS == 0
    head_spec = pl.BlockSpec((1, n_heads, t_len, LANES), lambda b: (b, 0, 0, 0))
    weights = (lw["w_uk_t"], lw["w_uk_q"], lw["w_uv"], lw["kgain"], consts["place"])
    return pl.pallas_call(
        functools.partial(_sample_attn_kernel, past=past), grid=(bsz,),
        in_specs=[head_spec, head_spec, head_spec,
                  pl.BlockSpec((1, 1, l_cache, KV_RANK), lambda b: (layer, b, 0, 0)),
                  pl.BlockSpec((1, l_cache, D_ROPE), lambda b: (b, 0, 0))]
                 + [_const_spec(w) for w in weights],
        out_specs=pl.BlockSpec((1, t_len, n_heads * D_V), lambda b: (b, 0, 0)),
        out_shape=jax.ShapeDtypeStruct((bsz, t_len, n_heads * D_V), _BF16),
        compiler_params=pltpu.CompilerParams(dimension_semantics=("arbitrary",),
                                             vmem_limit_bytes=VMEM_LIMIT_BYTES),
        name="sample_attn",
    )(q, k_new, v_new, cache_lat, cache_kpe, *weights)


def _pack_heads(w, width, alternate=False):
    kdim = w.shape[0]
    w3 = w.reshape(kdim, N_HEADS, width)
    low = jnp.pad(w3, ((0, 0), (0, 0), (0, LANES - width)))
    if alternate:
        high = jnp.pad(w3, ((0, 0), (0, 0), (LANES - width, 0)))
        odd = (jnp.arange(N_HEADS) % 2 == 1)[None, :, None]
        low = jnp.where(odd, high, low)
    return low.reshape(kdim, N_HEADS * LANES)


def _layer_weights(l, norm_mix, w_in, q_a_norm, w_uq, kv_a_norm, w_uk, w_uv, q_norm, k_norm,
                   w_pool, pool_scale, w_o, norm_ffn, w_gate, w_up, w_down):
    d_in = w_in.shape[2]
    d_in_pad = -(-d_in // (2 * LANES)) * (2 * LANES)
    pad96 = lambda g: jnp.pad(g.astype(_F32), (0, LANES - D_QK))[None, :]
    q_scale = D_QK ** -0.5 * math.log2(math.e)
    return dict(
        gmix=norm_mix[l][None, :].astype(_F32),
        w_in=jnp.pad(w_in[l], ((0, 0), (0, d_in_pad - d_in))).astype(_BF16),
        gq=q_a_norm[l][None, :].astype(_F32),
        w_uq=_pack_heads(w_uq[l], D_QK).astype(_BF16),
        w_uq_t=_pack_heads(w_uq[l], D_QK).astype(_BF16).T,
        gkv=kv_a_norm[l][None, :].astype(_F32),
        w_uk=_pack_heads(w_uk[l], D_NOPE).astype(_BF16),
        w_uk_t=w_uk[l].astype(_BF16).T,
        w_uk_q=jnp.pad(w_uk[l].astype(_BF16).T.reshape(N_HEADS, D_NOPE, KV_RANK),
                       ((0, 0), (0, LANES - D_NOPE), (0, 0))),
        w_uv=_pack_heads(w_uv[l], D_V, alternate=True).astype(_BF16),
        w_uv_t=w_uv[l].astype(_BF16).T,
        qgain=pad96(q_norm[l]) * q_scale,
        qgain_t=(pad96(q_norm[l]) * q_scale).T,
        kgain=pad96(k_norm[l]),
        w_pool=w_pool[l].astype(_BF16),
        pscale=pool_scale[l][None, :].astype(_F32),
        w_o_pool=w_o[l][:POOL_WIDTH].astype(_BF16),
        w_o_attn=w_o[l][POOL_WIDTH:].astype(_BF16),
        gffn=norm_ffn[l][None, :].astype(_F32),
        w_gate=w_gate[l].astype(_BF16),
        w_up=w_up[l].astype(_BF16),
        w_down=w_down[l].astype(_BF16),
    )


def _rope_tables(pos0, t_len):
    half = D_ROPE // 2
    inv = ROPE_THETA ** (-jnp.arange(half, dtype=_F32) / half)
    ang = (pos0 + jnp.arange(t_len, dtype=jnp.int32)).astype(_F32)[:, None] * inv[None, :]
    cos, sin = jnp.cos(ang), jnp.sin(ang)
    cos32 = jnp.concatenate([cos, cos], axis=1)
    sin32 = jnp.concatenate([-sin, sin], axis=1)
    ones = jnp.ones((t_len, D_NOPE), _F32)
    zeros = jnp.zeros((t_len, D_NOPE), _F32)
    tail1 = jnp.ones((t_len, LANES - D_QK), _F32)
    tail0 = jnp.zeros((t_len, LANES - D_QK), _F32)
    cosp = jnp.concatenate([ones, cos32, tail1], axis=1)
    sinp = jnp.concatenate([zeros, sin32, tail0], axis=1)
    cosk = jnp.concatenate([cos32, ones, tail1], axis=1)
    sink = jnp.concatenate([sin32, zeros, tail0], axis=1)
    return (cosp, sinp, cosk, sink, cos.T, sin.T)


def _perm_consts():
    idx = jnp.arange(D_ROPE)
    place = jnp.zeros((D_ROPE, LANES), _F32).at[idx, D_NOPE + idx].set(1.0)
    return dict(place=place.astype(_BF16))


def kernel(x_prompt, x_sample, cache_latent, cache_krope, state_pool, meta_tokens, norm_mix, w_in, q_a_norm, w_uq, kv_a_norm, w_uk, w_uv, q_norm, k_norm, w_pool, pool_scale, w_o, norm_ffn, w_gate, w_up, w_down):
    depth = norm_mix.shape[0]
    bp, s_len, d_model = x_prompt.shape
    bs, t_len, _ = x_sample.shape
    past = cache_latent.shape[2] - N_META
    consts = _perm_consts()
    tabs_meta = _rope_tables(0, N_META)
    tabs_prompt = _rope_tables(N_META, s_len)
    tabs_sample = _rope_tables(N_META + past, t_len)
    nb_sample = math.gcd(bs, 8)

    xm = meta_tokens.astype(_F32)[None]
    xp = x_prompt
    xs = x_sample
    zero_hist = jnp.zeros((1, HIST_ROWS, POOL_WIDTH), _F32)
    seq_bufs = tuple(jnp.zeros((depth, bp, N_META + s_len, width), _F32) for width in (KV_RANK, D_ROPE))
    meta_lat, meta_kr, pool_p, lat_s, kpe_s, pool_s = [], [], [], [], [], []
    for l in range(depth):
        lw = _layer_weights(l, norm_mix, w_in, q_a_norm, w_uq, kv_a_norm, w_uk, w_uv, q_norm, k_norm,
                            w_pool, pool_scale, w_o, norm_ffn, w_gate, w_up, w_down)
        pm = _pre_call(xm, zero_hist, tabs_meta, lw, nb=1, hist_shared=True, truncated=True)
        pp = _pre_call(xp, pm["ulast"], tabs_prompt, lw, nb=1, hist_shared=True, truncated=False,
                       feature_major=True, into=(l, depth, seq_bufs))
        seq_bufs = (pp["lat"], pp["krope"])
        hist_s = jnp.pad(state_pool[l].astype(_F32), ((0, 0), (HIST_ROWS - POOL_STATE, 0), (0, 0)))
        ps = _pre_call(xs, hist_s, tabs_sample, lw, nb=nb_sample, hist_shared=False, truncated=False)

        km = jnp.pad(pm["k"][0], ((0, 0), (0, LANES - N_META), (0, 0)))
        vm_t = jnp.stack([pm["v"][0, hd, :, (hd % 2) * D_V:(hd % 2 + 1) * D_V].T for hd in range(N_HEADS)])
        vm_t = jnp.concatenate([vm_t, jnp.ones((N_HEADS, V_ROWS - D_V, N_META), vm_t.dtype)], axis=1)
        vm_t = jnp.pad(vm_t, ((0, 0), (0, 0), (0, LANES - N_META)))
        kbound = (D_QK ** 0.5 * jnp.max(jnp.abs(lw["kgain"]))).reshape(1, 1)
        qbound = D_QK ** 0.5 * jnp.max(jnp.abs(lw["qgain"]))
        attn_args = (kbound, pp["q"], pp["k"], pp["v"], km, vm_t)
        attn_p = lax.cond(2.0 * qbound * kbound[0, 0] <= EXP2_SAFE_RANGE,
                          functools.partial(_prompt_attn_call, online=False),
                          functools.partial(_prompt_attn_call, online=True), *attn_args)
        attn_s = _sample_attn_call(l, ps["q"], ps["k"], ps["v"], cache_latent, cache_krope[l], lw, consts)
        xp = _post_call(xp, pp["pool"], attn_p, lw)
        xs = _post_call(xs, ps["pool"], attn_s, lw)
        if l + 1 < depth:
            attn_m = _meta_attn_call(pm["q"][0], pm["k"][0], pm["v"][0])
            xm = _post_call(xm, pm["pool"], attn_m[None], lw)

        meta_lat.append(pm["lat"][0])
        meta_kr.append(pm["krope"][0])
        pool_p.append(pp["ulast"][:, HIST_ROWS - POOL_STATE:])
        lat_s.append(ps["lat"])
        kpe_s.append(ps["krope"])
        pool_s.append(ps["ulast"][:, HIST_ROWS - POOL_STATE:])
    lat_p, kpe_p = _meta_rows_call(*seq_bufs, jnp.stack(meta_lat), jnp.stack(meta_kr))
    return (xp, xs, lat_p, kpe_p, jnp.stack(pool_p), jnp.stack(lat_s), jnp.stack(kpe_s), jnp.stack(pool_s))
```
